```python
import math
import jax, jax.numpy as jnp
from jax import lax
import numpy as np

D_MODEL = 1024
BATCH = 4
SEQ = 4096
DEPTH = 4
DEC_BATCH = 32
DEC_SEQ = 4
PAST_LEN = 8192
PAGE_SIZE = 128

HEAD_DIM = 64
MIX_WIDTH = D_MODEL
CONV_WIDTH = MIX_WIDTH // 4
RWKV_WIDTH = MIX_WIDTH // 4
ATT_WIDTH = MIX_WIDTH - CONV_WIDTH - RWKV_WIDTH
N_RWKV_HEADS = RWKV_WIDTH // HEAD_DIM
N_ATT_HEADS = ATT_WIDTH // HEAD_DIM
CONV_K = 31
DECAY_LORA = 64
AAA_LORA = 64
GATE_LORA = 128
CONV_COLS = 2 * CONV_WIDTH
RWKV_COLS = 3 * RWKV_WIDTH + DECAY_LORA + AAA_LORA + GATE_LORA
ATT_COLS = 3 * ATT_WIDTH
IN_COLS = CONV_COLS + RWKV_COLS + ATT_COLS
DILATIONS = ((128, 1), (512, 4), (2048, 16))
MAX_WINDOW = max(w for w, _ in DILATIONS)
BLOCK = 128
N_REL_BUCKETS = 32
REL_EXACT = N_REL_BUCKETS // 2
REL_MAX_DIST = MAX_WINDOW
D_FF = -(-8 * D_MODEL // (3 * 256)) * 256
RMS_EPS = 1e-6
LN_EPS = 1e-5
RWKV_GN_EPS = HEAD_DIM * 1e-5
DECAY_SCALE = math.exp(-0.5)
NEG_INF = -1e30

kernel_name = 'hymba_conv_rwkv7_dilated_attn_step'


def rms_norm(x, g):
    xf = x.astype(jnp.float32)
    y = xf * lax.rsqrt(jnp.mean(xf * xf, -1, keepdims=True) + RMS_EPS)
    return (y * g).astype(x.dtype)


def layer_norm(x, g, b):
    xf = x.astype(jnp.float32)
    mu = jnp.mean(xf, -1, keepdims=True)
    var = jnp.mean(jnp.square(xf - mu), -1, keepdims=True)
    return ((xf - mu) * lax.rsqrt(var + LN_EPS) * g + b).astype(x.dtype)


def rel_bucket(dist):
    d = jnp.maximum(dist, 1).astype(jnp.float32)
    large = REL_EXACT + (jnp.log(d / REL_EXACT) / math.log(REL_MAX_DIST / REL_EXACT)
                         * (N_REL_BUCKETS - REL_EXACT)).astype(jnp.int32)
    large = jnp.minimum(large, N_REL_BUCKETS - 1)
    return jnp.where(dist < REL_EXACT, dist, large)


def depthwise_causal_conv(zc, w, b):
    y = lax.conv_general_dilated(zc, w[:, None, :].astype(zc.dtype), window_strides=(1,), padding='VALID',
                                 dimension_numbers=('NWC', 'WIO', 'NWC'), feature_group_count=zc.shape[-1])
    return y + b


def wkv_scan(s0, r, w, k, v, kk, a):
    def step(s, inp):
        r_t, w_t, k_t, v_t, kk_t, a_t = inp
        s_kk = jnp.einsum('nhvk,nhk->nhv', s, kk_t)
        s = (s * w_t[:, :, None, :] - s_kk[..., None] * (kk_t * a_t)[:, :, None, :]
             + v_t[..., None] * k_t[:, :, None, :])
        return s, jnp.einsum('nhvk,nhk->nhv', s, r_t)
    seqs = tuple(jnp.moveaxis(z, 1, 0) for z in (r, w, k, v, kk, a))
    s, y = lax.scan(step, s0, seqs)
    return jnp.moveaxis(y, 0, 1), s


def rwkv_mixer(u, prev, s0, p):
    n, t = u.shape[:2]
    uf = u.astype(jnp.float32)
    u_prev = jnp.concatenate([prev.astype(jnp.float32)[:, None], uf[:, :-1]], axis=1)
    xs = uf + (u_prev - uf) * p['rwkv_mu']
    o = 3 * RWKV_WIDTH
    r = xs[..., :RWKV_WIDTH]
    k = xs[..., RWKV_WIDTH:2 * RWKV_WIDTH]
    v = xs[..., 2 * RWKV_WIDTH:o]
    w_lo = xs[..., o:o + DECAY_LORA]
    a_lo = xs[..., o + DECAY_LORA:o + DECAY_LORA + AAA_LORA]
    g_lo = xs[..., o + DECAY_LORA + AAA_LORA:]
    w_log = -DECAY_SCALE * jax.nn.sigmoid(p['rwkv_w0'] + jnp.tanh(w_lo) @ p['rwkv_w2'])
    a = jax.nn.sigmoid(p['rwkv_a0'] + a_lo @ p['rwkv_a2'])
    g = jax.nn.sigmoid(g_lo) @ p['rwkv_g2']
    heads = lambda z: z.reshape(n, t, N_RWKV_HEADS, HEAD_DIM).astype(jnp.float32)
    kk = heads(k * p['rwkv_k_k'])
    kk = kk / jnp.maximum(jnp.sqrt(jnp.sum(kk * kk, -1, keepdims=True)), 1e-12)
    k = k * (1.0 + (a - 1.0) * p['rwkv_k_a'])
    rh, kh, vh, ah = heads(r), heads(k), heads(v), heads(a)
    y, s = wkv_scan(s0, rh, heads(jnp.exp(w_log)), kh, vh, kk, ah)
    mu = jnp.mean(y, -1, keepdims=True)
    var = jnp.mean(jnp.square(y - mu), -1, keepdims=True)
    y = ((y - mu) * lax.rsqrt(var + RWKV_GN_EPS)).reshape(n, t, RWKV_WIDTH) * p['rwkv_ln_g'] + p['rwkv_ln_b']
    bonus = jnp.sum(rh * kh * p['rwkv_r_k'], -1, keepdims=True) * vh
    y = (y + bonus.reshape(n, t, RWKV_WIDTH)) * g
    return y.astype(u.dtype), u[:, -1], s


def merge_by_denominator(outs, maxes, sums):
    m = jnp.stack(maxes)
    l = jnp.stack(sums)
    o = jnp.stack(outs)
    wts = l * jnp.exp(m - jnp.max(m, 0, keepdims=True))
    return jnp.sum(wts[..., None] * o, 0) / jnp.sum(wts, 0)[..., None]


def dilated_attn_prompt(q, k, v, rel_bias):
    n, t = q.shape[:2]
    qf, kf, vf = (z.astype(jnp.float32) for z in (q, k, v))
    scale = HEAD_DIM ** -0.5
    qi = jnp.arange(BLOCK)[:, None]
    kj = jnp.arange(2 * BLOCK)[None, :]
    delta = qi + BLOCK - kj
    outs, maxes, sums = [], [], []
    for window, dil in DILATIONS:
        span = window // dil
        unit = dil * BLOCK
        tp = -(-t // unit) * unit
        nb = tp // unit

        def to_blocks(z):
            z = jnp.pad(z, ((0, 0), (0, tp - t), (0, 0), (0, 0)))
            return z.reshape(n, nb, BLOCK, dil, N_ATT_HEADS, HEAD_DIM)

        def with_prev(z):
            prev = jnp.concatenate([jnp.zeros_like(z[:, :1]), z[:, :-1]], axis=1)
            return jnp.concatenate([prev, z], axis=2)

        qb = to_blocks(qf)
        kw = with_prev(to_blocks(kf))
        vw = with_prev(to_blocks(vf))
        band = (delta >= 0) & (delta <= span)
        first = (jnp.arange(nb)[:, None] > 0) | (jnp.arange(2 * BLOCK)[None, :] >= BLOCK)
        mask = band[None] & first[:, None, :]
        bias = rel_bias[rel_bucket(jnp.maximum(delta, 0) * dil)].astype(jnp.float32)
        s = (jnp.einsum('nbqrhd,nbkrhd->nbrhqk', qb, kw) * scale
             + jnp.transpose(bias, (2, 0, 1))[None, None, None])
        s = jnp.where(mask[None, :, None, None], s, NEG_INF)
        m = jnp.max(s, -1)
        pr = jnp.exp(s - m[..., None])
        l = jnp.sum(pr, -1)
        o = jnp.einsum('nbrhqk,nbkrhd->nbqrhd', pr, vw) / jnp.transpose(l, (0, 1, 4, 2, 3))[..., None]
        outs.append(o.reshape(n, tp, N_ATT_HEADS, HEAD_DIM)[:, :t])
        maxes.append(jnp.transpose(m, (0, 1, 4, 2, 3)).reshape(n, tp, N_ATT_HEADS)[:, :t])
        sums.append(jnp.transpose(l, (0, 1, 4, 2, 3)).reshape(n, tp, N_ATT_HEADS)[:, :t])
    return merge_by_denominator(outs, maxes, sums)


def dilated_attn_sample(q, k, v, k_buf, v_buf, rel_bias):
    n, s = q.shape[:2]
    w = k_buf.shape[1]
    kc = jnp.concatenate([k_buf.astype(jnp.float32), k.astype(jnp.float32)], axis=1)
    vc = jnp.concatenate([v_buf.astype(jnp.float32), v.astype(jnp.float32)], axis=1)
    qf = q.astype(jnp.float32)
    scale = HEAD_DIM ** -0.5
    pos = jnp.arange(s)[:, None]
    outs, maxes, sums = [], [], []
    for window, dil in DILATIONS:
        steps = jnp.arange(window // dil + 1)
        idx = w + pos - steps[None, :] * dil
        valid = idx >= 0
        idx = jnp.maximum(idx, 0)
        kg = kc[:, idx]
        vg = vc[:, idx]
        bias = rel_bias[rel_bucket(steps * dil)].astype(jnp.float32).T
        sc = jnp.einsum('nshd,nsmhd->nshm', qf, kg) * scale + bias[None, None]
        sc = jnp.where(valid[None, :, None, :], sc, NEG_INF)
        m = jnp.max(sc, -1)
        pr = jnp.exp(sc - m[..., None])
        l = jnp.sum(pr, -1)
        outs.append(jnp.einsum('nshm,nsmhd->nshd', pr, vg) / l[..., None])
        maxes.append(m)
        sums.append(l)
    return merge_by_denominator(outs, maxes, sums)


def block(x, c, p, rel_bias, state):
    n, t = x.shape[:2]
    mod = (jax.nn.silu(c) @ p['w_ada'] + p['b_ada']).reshape(n, 6, 1, D_MODEL)
    shift1, scale1, gate1, shift2, scale2, gate2 = (mod[:, j] for j in range(6))
    h = rms_norm(x, p['g_pre_mix']) * (1 + scale1) + shift1
    u = h @ p['w_in']
    u_conv = u[..., :CONV_COLS]
    u_rwkv = u[..., CONV_COLS:CONV_COLS + RWKV_COLS]
    u_att = u[..., CONV_COLS + RWKV_COLS:]
    if state is None:
        conv_buf = jnp.zeros((n, CONV_K - 1, CONV_WIDTH), u.dtype)
        shift_prev = jnp.zeros((n, RWKV_COLS), u.dtype)
        wkv0 = jnp.zeros((n, N_RWKV_HEADS, HEAD_DIM, HEAD_DIM), jnp.float32)
    else:
        conv_buf, shift_prev, wkv0, k_buf, v_buf = state
    z = u_conv[..., :CONV_WIDTH] * jax.nn.sigmoid(u_conv[..., CONV_WIDTH:])
    zc = jnp.concatenate([conv_buf.astype(z.dtype), z], axis=1)
    y_a = jax.nn.silu(layer_norm(depthwise_causal_conv(zc, p['conv_w'], p['conv_b']),
                                 p['conv_ln_g'], p['conv_ln_b'])).astype(x.dtype)
    y_b, new_shift, new_wkv = rwkv_mixer(u_rwkv, shift_prev, wkv0.astype(jnp.float32), p)
    q, k, v = (u_att[..., i * ATT_WIDTH:(i + 1) * ATT_WIDTH].reshape(n, t, N_ATT_HEADS, HEAD_DIM)
               for i in range(3))
    if state is None:
        y_c = dilated_attn_prompt(q, k, v, rel_bias)
        win = min(MAX_WINDOW, t)
        new_k, new_v = k[:, t - win:], v[:, t - win:]
    else:
        y_c = dilated_attn_sample(q, k, v, k_buf, v_buf, rel_bias)
        new_k, new_v = k, v
    mix = jnp.concatenate([y_a, y_b.astype(x.dtype), y_c.reshape(n, t, ATT_WIDTH).astype(x.dtype)], -1) @ p['w_out']
    x = x + gate1 * rms_norm(mix, p['g_post_mix'])
    h = rms_norm(x, p['g_pre_ffn']) * (1 + scale2) + shift2
    gu = h @ p['w_ffn_in']
    f = (jax.nn.silu(gu[..., :D_FF]) * gu[..., D_FF:]) @ p['w_ffn_out']
    x = x + gate2 * rms_norm(f, p['g_post_ffn'])
    return x, (zc[:, -(CONV_K - 1):], new_shift, new_wkv.astype(x.dtype), new_k, new_v)


def setup_inputs(seed: int = 0) -> dict:
    key = jax.random.key(seed)
    ks = iter(jax.random.split(key, 40))
    nrm = lambda shape, s: jax.random.normal(next(ks), shape, jnp.float32) * s
    gain = lambda shape: 1.0 + nrm(shape, 0.02)
    win_buf = min(MAX_WINDOW, PAST_LEN)
    return {
        'x_prompt': nrm((BATCH, SEQ, D_MODEL), 1.0),
        'x_sample': nrm((DEC_BATCH, DEC_SEQ, D_MODEL), 1.0),
        'c_prompt': nrm((BATCH, D_MODEL), 1.0),
        'c_sample': nrm((DEC_BATCH, D_MODEL), 1.0),
        'cache_conv': nrm((DEPTH, DEC_BATCH, CONV_K - 1, CONV_WIDTH), 0.5),
        'state_shift': nrm((DEPTH, DEC_BATCH, RWKV_COLS), 1.0),
        'state_wkv': nrm((DEPTH, DEC_BATCH, N_RWKV_HEADS, HEAD_DIM, HEAD_DIM), 0.3),
        'cache_k_win': nrm((DEPTH, DEC_BATCH, win_buf, N_ATT_HEADS, HEAD_DIM), 1.0),
        'cache_v_win': nrm((DEPTH, DEC_BATCH, win_buf, N_ATT_HEADS, HEAD_DIM), 1.0),
        'w_ada': nrm((DEPTH, D_MODEL, 6 * D_MODEL), 0.5 * D_MODEL ** -0.5),
        'b_ada': nrm((DEPTH, 6 * D_MODEL), 0.02),
        'g_pre_mix': gain((DEPTH, D_MODEL)),
        'g_post_mix': gain((DEPTH, D_MODEL)),
        'g_pre_ffn': gain((DEPTH, D_MODEL)),
        'g_post_ffn': gain((DEPTH, D_MODEL)),
        'w_in': nrm((DEPTH, D_MODEL, IN_COLS), D_MODEL ** -0.5),
        'w_out': nrm((DEPTH, MIX_WIDTH, D_MODEL), MIX_WIDTH ** -0.5),
        'conv_w': nrm((DEPTH, CONV_K, CONV_WIDTH), CONV_K ** -0.5),
        'conv_b': nrm((DEPTH, CONV_WIDTH), 0.01),
        'conv_ln_g': gain((DEPTH, CONV_WIDTH)),
        'conv_ln_b': nrm((DEPTH, CONV_WIDTH), 0.01),
        'rwkv_mu': jax.random.uniform(next(ks), (DEPTH, RWKV_COLS), jnp.float32),
        'rwkv_w0': nrm((DEPTH, RWKV_WIDTH), 0.5),
        'rwkv_w2': nrm((DEPTH, DECAY_LORA, RWKV_WIDTH), 0.5 * DECAY_LORA ** -0.5),
        'rwkv_a0': nrm((DEPTH, RWKV_WIDTH), 0.5),
        'rwkv_a2': nrm((DEPTH, AAA_LORA, RWKV_WIDTH), 0.5 * AAA_LORA ** -0.5),
        'rwkv_g2': nrm((DEPTH, GATE_LORA, RWKV_WIDTH), GATE_LORA ** -0.5),
        'rwkv_k_k': 1.0 + nrm((DEPTH, RWKV_WIDTH), 0.1),
        'rwkv_k_a': 1.0 + nrm((DEPTH, RWKV_WIDTH), 0.1),
        'rwkv_r_k': nrm((DEPTH, N_RWKV_HEADS, HEAD_DIM), 0.1),
        'rwkv_ln_g': gain((DEPTH, RWKV_WIDTH)),
        'rwkv_ln_b': nrm((DEPTH, RWKV_WIDTH), 0.01),
        'rel_bias': nrm((N_REL_BUCKETS, N_ATT_HEADS), 0.2),
        'w_ffn_in': nrm((DEPTH, D_MODEL, 2 * D_FF), D_MODEL ** -0.5),
        'w_ffn_out': nrm((DEPTH, D_FF, D_MODEL), D_FF ** -0.5),
    }


def reference(x_prompt, x_sample, c_prompt, c_sample, cache_conv, state_shift, state_wkv, cache_k_win, cache_v_win,
              w_ada, b_ada, g_pre_mix, g_post_mix, g_pre_ffn, g_post_ffn, w_in, w_out,
              conv_w, conv_b, conv_ln_g, conv_ln_b,
              rwkv_mu, rwkv_w0, rwkv_w2, rwkv_a0, rwkv_a2, rwkv_g2, rwkv_k_k, rwkv_k_a, rwkv_r_k,
              rwkv_ln_g, rwkv_ln_b, rel_bias, w_ffn_in, w_ffn_out):
    yp, ys = x_prompt, x_sample
    conv_p, conv_s, shift_p, shift_s, wkv_p, wkv_s = [], [], [], [], [], []
    kp, ks_, vp, vs = [], [], [], []
    for li in range(DEPTH):
        p = dict(w_ada=w_ada[li], b_ada=b_ada[li], g_pre_mix=g_pre_mix[li], g_post_mix=g_post_mix[li],
                 g_pre_ffn=g_pre_ffn[li], g_post_ffn=g_post_ffn[li], w_in=w_in[li], w_out=w_out[li],
                 conv_w=conv_w[li], conv_b=conv_b[li], conv_ln_g=conv_ln_g[li], conv_ln_b=conv_ln_b[li],
                 rwkv_mu=rwkv_mu[li], rwkv_w0=rwkv_w0[li], rwkv_w2=rwkv_w2[li], rwkv_a0=rwkv_a0[li],
                 rwkv_a2=rwkv_a2[li], rwkv_g2=rwkv_g2[li], rwkv_k_k=rwkv_k_k[li], rwkv_k_a=rwkv_k_a[li],
                 rwkv_r_k=rwkv_r_k[li], rwkv_ln_g=rwkv_ln_g[li], rwkv_ln_b=rwkv_ln_b[li],
                 w_ffn_in=w_ffn_in[li], w_ffn_out=w_ffn_out[li])
        yp, (cb, sh, wk, kn, vn) = block(yp, c_prompt, p, rel_bias, None)
        conv_p.append(cb); shift_p.append(sh); wkv_p.append(wk); kp.append(kn); vp.append(vn)
        ys, (cb, sh, wk, kn, vn) = block(ys, c_sample, p, rel_bias,
                                         (cache_conv[li], state_shift[li], state_wkv[li],
                                          cache_k_win[li], cache_v_win[li]))
        conv_s.append(cb); shift_s.append(sh); wkv_s.append(wk); ks_.append(kn); vs.append(vn)
    return (yp, ys,
            jnp.stack(conv_p), jnp.stack(conv_s),
            jnp.stack(shift_p), jnp.stack(shift_s),
            jnp.stack(wkv_p), jnp.stack(wkv_s),
            jnp.stack(kp), jnp.stack(ks_),
            jnp.stack(vp), jnp.stack(vs))
```

```python
import functools
import math

import numpy as np
import jax
import jax.numpy as jnp
from jax import lax
from jax.experimental import pallas as pl
from jax.experimental.pallas import tpu as pltpu

F32 = jnp.float32
BF16 = jnp.bfloat16

D_MODEL = 1024
HEAD_DIM = 64
CONV_WIDTH = 256
RWKV_WIDTH = 256
ATT_WIDTH = 512
N_RWKV_HEADS = RWKV_WIDTH // HEAD_DIM
N_ATT_HEADS = ATT_WIDTH // HEAD_DIM
CONV_K = 31
DECAY_LORA = 64
AAA_LORA = 64
GATE_LORA = 128
CONV_COLS = 2 * CONV_WIDTH
RWKV_COLS = 3 * RWKV_WIDTH + DECAY_LORA + AAA_LORA + GATE_LORA
ATT_COLS = 3 * ATT_WIDTH
IN_COLS = CONV_COLS + RWKV_COLS + ATT_COLS
DILATIONS = ((128, 1), (512, 4), (2048, 16))
MAX_WINDOW = 2048
BLOCK = 128
N_REL_BUCKETS = 32
REL_EXACT = N_REL_BUCKETS // 2
REL_MAX_DIST = MAX_WINDOW
D_FF = 2816
RMS_EPS = 1e-6
LN_EPS = 1e-5
RWKV_GN_EPS = HEAD_DIM * 1e-5
DECAY_SCALE = math.exp(-0.5)
NEG_INF = -1e30

VMEM_LIMIT_BYTES = 56 * 1024 * 1024
SUBLANES = 8
CONV_HALO = 32
RWKV_CHUNK = 64

NN = (((1,), (0,)), ((), ()))
NT = (((1,), (1,)), ((), ()))


def _cparams(sem):
    return pltpu.CompilerParams(dimension_semantics=sem, vmem_limit_bytes=VMEM_LIMIT_BYTES)


def _split_bf16(x, n):
    pieces = []
    r = x
    for i in range(n):
        p = r.astype(BF16)
        pieces.append(p)
        if i + 1 < n:
            r = r - p.astype(F32)
    return pieces


def _mm(a, b, dims=NN, pa=2, pb=2):
    ap = [a] if a.dtype == BF16 else _split_bf16(a, pa)
    bp = [b] if b.dtype == BF16 else _split_bf16(b, pb)
    order = max(len(ap), len(bp))
    out = None
    for i, x in enumerate(ap):
        for j, y in enumerate(bp):
            if i + j < order:
                t = lax.dot_general(x, y, dims, preferred_element_type=F32)
                out = t if out is None else out + t
    return out


def _dot1(a, b, dims=NN):
    return lax.dot_general(a.astype(BF16), b.astype(BF16), dims, preferred_element_type=F32)


def _sigmoid(x):
    return 1.0 / (1.0 + jnp.exp(-x))


def _silu(x):
    return x * _sigmoid(x)


def _ada_kernel(c_ref, w_ref, b_ref, o_ref):
    a = _silu(c_ref[...])
    o_ref[...] = _dot1(a, w_ref[...]) + b_ref[...]


def _ada_modulation(c_all, w_ada, b_ada):
    depth = w_ada.shape[0]
    rows = c_all.shape[0]
    tn = D_MODEL
    return pl.pallas_call(
        _ada_kernel,
        out_shape=jax.ShapeDtypeStruct((depth, rows, 6 * D_MODEL), F32),
        grid=(depth, 6 * D_MODEL // tn),
        in_specs=[
            pl.BlockSpec((rows, D_MODEL), lambda l, j: (0, 0)),
            pl.BlockSpec((None, D_MODEL, tn), lambda l, j: (l, 0, j)),
            pl.BlockSpec((None, 1, tn), lambda l, j: (l, 0, j)),
        ],
        out_specs=pl.BlockSpec((None, rows, tn), lambda l, j: (l, 0, j)),
        compiler_params=_cparams(("arbitrary", "arbitrary")),
        name="ada_modulation",
    )(c_all, w_ada, b_ada.reshape(depth, 1, 6 * D_MODEL))


def _in_proj_kernel(x_ref, sc_ref, sh_ref, g_ref, w_ref, uc_ref, ur_ref, q_ref, k_ref, v_ref):
    x = x_ref[...]
    ms = jnp.mean(x * x, axis=-1, keepdims=True)
    h = x * lax.rsqrt(ms + RMS_EPS) * g_ref[...]
    h = h * (1.0 + sc_ref[...]) + sh_ref[...]
    hb = h.astype(BF16)
    o = 0
    for ref, width in ((uc_ref, CONV_COLS), (ur_ref, RWKV_COLS), (q_ref, ATT_WIDTH), (k_ref, ATT_WIDTH),
                       (v_ref, ATT_WIDTH)):
        ref[...] = lax.dot_general(hb, w_ref[:, o:o + width], NN, preferred_element_type=F32)
        o += width


def _mod_spec(mod, tiles_per_group):
    _, r, d = mod.shape
    return pl.BlockSpec((None, r, d), lambda i: (i // tiles_per_group, 0, 0))


def _in_proj(x2, scale, shift, g, w_bf, tm, tiles_per_group):
    m = x2.shape[0]
    widths = (CONV_COLS, RWKV_COLS, ATT_WIDTH, ATT_WIDTH, ATT_WIDTH)
    return pl.pallas_call(
        _in_proj_kernel,
        out_shape=tuple(jax.ShapeDtypeStruct((m, w), F32) for w in widths),
        grid=(m // tm,),
        in_specs=[
            pl.BlockSpec((tm, D_MODEL), lambda i: (i, 0)),
            _mod_spec(scale, tiles_per_group),
            _mod_spec(shift, tiles_per_group),
            pl.BlockSpec((1, D_MODEL), lambda i: (0, 0)),
            pl.BlockSpec((D_MODEL, IN_COLS), lambda i: (0, 0)),
        ],
        out_specs=tuple(pl.BlockSpec((tm, w), lambda i: (i, 0)) for w in widths),
        compiler_params=_cparams(("arbitrary",)),
        name="in_proj",
    )(x2, scale, shift, g.reshape(1, D_MODEL), w_bf)


def _conv_core(zs_ref, w_ref, b_ref, lg_ref, lb_ref, y_ref, tt):
    off = CONV_HALO - (CONV_K - 1)
    sub = min(tt, 64)
    for r0 in range(0, tt, sub):
        acc = jnp.zeros((sub, CONV_WIDTH), F32)
        for j in range(CONV_K):
            acc = acc + w_ref[j:j + 1, :] * zs_ref[pl.ds(r0 + off + j, sub), :]
        y = acc + b_ref[...]
        mu = jnp.mean(y, axis=-1, keepdims=True)
        yc = y - mu
        var = jnp.mean(yc * yc, axis=-1, keepdims=True)
        yn = yc * lax.rsqrt(var + LN_EPS) * lg_ref[...] + lb_ref[...]
        y_ref[r0:r0 + sub, :] = _silu(yn)


def _glu(u):
    return u[:, :CONV_WIDTH] * _sigmoid(u[:, CONV_WIDTH:])


def _conv_prompt_kernel(u_ref, uh_ref, w_ref, b_ref, lg_ref, lb_ref, y_ref, zt_ref, zs_ref, *, tt):
    j = pl.program_id(1)
    zh = _glu(uh_ref[...])
    zs_ref[0:CONV_HALO, :] = jnp.where(j > 0, zh, 0.0)
    zs_ref[CONV_HALO:CONV_HALO + tt, :] = _glu(u_ref[...])
    _conv_core(zs_ref, w_ref, b_ref, lg_ref, lb_ref, y_ref, tt)
    zt_ref[...] = zs_ref[tt:tt + CONV_HALO, :]


def _conv_sample_kernel(u_ref, zh_ref, w_ref, b_ref, lg_ref, lb_ref, y_ref, z_ref, zs_ref, *, tt):
    z = _glu(u_ref[...])
    zs_ref[0:CONV_HALO, :] = zh_ref[...]
    zs_ref[CONV_HALO:CONV_HALO + tt, :] = z
    _conv_core(zs_ref, w_ref, b_ref, lg_ref, lb_ref, y_ref, tt)
    z_ref[...] = z


def _conv_param_specs(nargs_grid):
    cmap = (lambda n, j: (0, 0)) if nargs_grid == 2 else (lambda n: (0, 0))
    return [
        pl.BlockSpec((CONV_HALO, CONV_WIDTH), cmap),
        pl.BlockSpec((1, CONV_WIDTH), cmap),
        pl.BlockSpec((1, CONV_WIDTH), cmap),
        pl.BlockSpec((1, CONV_WIDTH), cmap),
    ]


def _conv_params(conv_w, conv_b, ln_g, ln_b):
    w = jnp.pad(conv_w, ((0, CONV_HALO - CONV_K), (0, 0)))
    return w, conv_b.reshape(1, -1), ln_g.reshape(1, -1), ln_b.reshape(1, -1)


def _conv_prompt(uc, n, t, params, tt=512):
    u3 = uc.reshape(n, t, CONV_COLS)
    hb = tt // CONV_HALO
    y, zt = pl.pallas_call(
        functools.partial(_conv_prompt_kernel, tt=tt),
        out_shape=(jax.ShapeDtypeStruct((n, t, CONV_WIDTH), F32),
                   jax.ShapeDtypeStruct((n, CONV_HALO, CONV_WIDTH), F32)),
        grid=(n, t // tt),
        in_specs=[
            pl.BlockSpec((None, tt, CONV_COLS), lambda b, j: (b, j, 0)),
            pl.BlockSpec((None, CONV_HALO, CONV_COLS), lambda b, j: (b, jnp.maximum(j * hb - 1, 0), 0)),
        ] + _conv_param_specs(2),
        out_specs=(pl.BlockSpec((None, tt, CONV_WIDTH), lambda b, j: (b, j, 0)),
                   pl.BlockSpec((None, CONV_HALO, CONV_WIDTH), lambda b, j: (b, 0, 0))),
        scratch_shapes=[pltpu.VMEM((CONV_HALO + tt, CONV_WIDTH), F32)],
        compiler_params=_cparams(("arbitrary", "arbitrary")),
        name="conv_prompt",
    )(u3, u3, *params)
    return y.reshape(n * t, CONV_WIDTH), zt[:, CONV_HALO - (CONV_K - 1):]


def _conv_sample(uc, cache, n, t, params):
    tt = SUBLANES
    u3 = jnp.pad(uc.reshape(n, t, CONV_COLS), ((0, 0), (0, tt - t), (0, 0)))
    zh = jnp.pad(cache, ((0, 0), (CONV_HALO - (CONV_K - 1), 0), (0, 0)))
    y, z = pl.pallas_call(
        functools.partial(_conv_sample_kernel, tt=tt),
        out_shape=(jax.ShapeDtypeStruct((n, tt, CONV_WIDTH), F32),
                   jax.ShapeDtypeStruct((n, tt, CONV_WIDTH), F32)),
        grid=(n,),
        in_specs=[
            pl.BlockSpec((None, tt, CONV_COLS), lambda b: (b, 0, 0)),
            pl.BlockSpec((None, CONV_HALO, CONV_WIDTH), lambda b: (b, 0, 0)),
        ] + _conv_param_specs(1),
        out_specs=(pl.BlockSpec((None, tt, CONV_WIDTH), lambda b: (b, 0, 0)),
                   pl.BlockSpec((None, tt, CONV_WIDTH), lambda b: (b, 0, 0))),
        scratch_shapes=[pltpu.VMEM((CONV_HALO + tt, CONV_WIDTH), F32)],
        compiler_params=_cparams(("arbitrary",)),
        name="conv_sample",
    )(u3, zh, *params)
    new_cache = jnp.concatenate([cache[:, t:], z[:, :t]], axis=1)
    return y[:, :t].reshape(n * t, CONV_WIDTH), new_cache


def _rwkv_consts(tr, c):
    hc = N_RWKV_HEADS * c
    rows = np.arange(tr)
    tri = ((rows[:, None] // c == rows[None, :] // c) & (rows[None, :] <= rows[:, None])).astype(np.float32)
    lanes = np.arange(RWKV_WIDTH)
    bd = (lanes[:, None] // HEAD_DIM == lanes[None, :] // HEAD_DIM).astype(np.float32)
    hrow = np.arange(hc)
    hm = (hrow[:, None] // c == lanes[None, :] // HEAD_DIM).astype(np.float32)
    same = hrow[:, None] // c == hrow[None, :] // c
    mstrict = (same & (hrow[None, :] < hrow[:, None])).astype(np.float32)
    mincl = (same & (hrow[None, :] <= hrow[:, None])).astype(np.float32)
    return (jnp.asarray(tri, BF16), jnp.asarray(bd, BF16), jnp.asarray(hm), jnp.asarray(mstrict),
            jnp.asarray(mincl))


def _rwkv_kernel(u_ref, up_ref, sh0_ref, s0_ref, mu_ref, w0_ref, wwa_ref, a0_ref, g2_ref, kk_ref, ka_ref, rk_ref,
                 lg_ref, lb_ref, tri_ref, bd_ref, hm_ref, ms_ref, mi_ref,
                 y_ref, so_ref,
                 us_ref, pt_ref, bt_ref, kt_ref, rt_ref, vv_ref, gg_ref, bon_ref, gate_ref, yy_ref, s_ref,
                 *, tr, c, t_valid):
    j = pl.program_id(1)
    hc = N_RWKV_HEADS * c

    @pl.when(j == 0)
    def _():
        s_ref[...] = s0_ref[...]

    u = u_ref[...]
    prev = jnp.where(j > 0, up_ref[...], jnp.broadcast_to(sh0_ref[...], (SUBLANES, RWKV_COLS)))
    us_ref[0:SUBLANES, :] = prev
    us_ref[SUBLANES:SUBLANES + tr, :] = u
    u_prev = us_ref[pl.ds(SUBLANES - 1, tr), :]
    xs = u + (u_prev - u) * mu_ref[...]
    o = 3 * RWKV_WIDTH
    r = xs[:, :RWKV_WIDTH]
    k = xs[:, RWKV_WIDTH:2 * RWKV_WIDTH]
    v = xs[:, 2 * RWKV_WIDTH:o]
    lo = xs[:, o:o + DECAY_LORA + AAA_LORA]
    lane = lax.broadcasted_iota(jnp.int32, lo.shape, 1)
    lo = jnp.where(lane < DECAY_LORA, jnp.tanh(lo), lo)
    wa = _mm(lo, wwa_ref[...])
    g = _mm(_sigmoid(xs[:, o + DECAY_LORA + AAA_LORA:]), g2_ref[...])
    lw = -DECAY_SCALE * _sigmoid(w0_ref[...] + wa[:, :RWKV_WIDTH])
    a = _sigmoid(a0_ref[...] + wa[:, RWKV_WIDTH:])
    bd = bd_ref[...]
    kkr = k * kk_ref[...]
    ss = _mm(kkr * kkr, bd, pa=3)
    kk = kkr / jnp.maximum(jnp.sqrt(ss), 1e-12)
    k2 = k * (1.0 + (a - 1.0) * ka_ref[...])
    if t_valid < tr:
        valid = lax.broadcasted_iota(jnp.int32, (tr, RWKV_WIDTH), 0) < t_valid
        zero = jnp.zeros_like(r)
        r, k2, v, kk, lw = (jnp.where(valid, z, zero) for z in (r, k2, v, kk, lw))
    gi = _mm(tri_ref[...], lw, pb=3)
    e_neg = jnp.exp(-gi)
    pt_ref[...] = -kk * jnp.exp(gi - lw)
    bt_ref[...] = kk * a * e_neg
    kt_ref[...] = k2 * e_neg
    rt_ref[...] = r * jnp.exp(gi)
    vv_ref[...] = v
    gg_ref[...] = gi
    bon_ref[...] = _mm(r * k2 * rk_ref[...], bd, pa=3) * v
    gate_ref[...] = g

    hm = hm_ref[...]
    mstrict = ms_ref[...] > 0.5
    mincl = mi_ref[...] > 0.5
    eye = (lax.broadcasted_iota(jnp.int32, (RWKV_WIDTH, RWKV_WIDTH), 0)
           == lax.broadcasted_iota(jnp.int32, (RWKV_WIDTH, RWKV_WIDTH), 1))
    eye_hc = (lax.broadcasted_iota(jnp.int32, (hc, hc), 0) == lax.broadcasted_iota(jnp.int32, (hc, hc), 1))
    n_sq = int(round(math.log2(c))) - 1

    def stack(x):
        return jnp.concatenate([x] * N_RWKV_HEADS, axis=0) * hm

    def chunk(ci, carry):
        sl = pl.ds(pl.multiple_of(ci * c, c), c)
        ps, bs, ks, rs, vs = (stack(ref[sl, :]) for ref in (pt_ref, bt_ref, kt_ref, rt_ref, vv_ref))
        gc = jnp.exp(gg_ref[pl.ds(ci * c + c - 1, 1), :])
        bh = bs * gc
        kh = ks * gc
        zero = jnp.zeros((hc, hc), F32)
        lm = jnp.where(mstrict, _mm(ps, bs, NT), zero)
        mm_ = jnp.where(mstrict, _mm(ps, ks, NT), zero)
        qb = jnp.where(mincl, _mm(rs, bs, NT), zero)
        qk = jnp.where(mincl, _mm(rs, ks, NT), zero)
        tinv = jnp.where(eye_hc, 1.0, 0.0) + lm
        lk = lm
        for _ in range(n_sq):
            lk = _mm(lk, lk)
            tinv = tinv + _mm(lk, tinv)
        mv = _mm(mm_, vs)
        p2 = _mm(tinv, ps)
        u0 = _mm(tinv, mv)
        abd = _mm(p2.T, bh) + jnp.where(eye, jnp.broadcast_to(gc, (RWKV_WIDTH, RWKV_WIDTH)), 0.0)
        dbd = _mm(u0.T, bh) + _mm(vs.T, kh)
        s0 = s_ref[...]
        us = _mm(p2, s0, NT) + u0
        ys = _mm(rs, s0, NT) + _mm(qb, us) + _mm(qk, vs)
        y = ys[0:c]
        for h in range(1, N_RWKV_HEADS):
            y = y + ys[h * c:(h + 1) * c]
        yy_ref[sl, :] = y
        s_ref[...] = _mm(s0, abd) + dbd
        return carry

    lax.fori_loop(0, tr // c, chunk, 0)

    y = yy_ref[...]
    inv = 1.0 / HEAD_DIM
    mu = _mm(y, bd, pa=3) * inv
    yc = y - mu
    var = _mm(yc * yc, bd, pa=3) * inv
    yn = yc * lax.rsqrt(var + RWKV_GN_EPS) * lg_ref[...] + lb_ref[...]
    y_ref[...] = (yn + bon_ref[...]) * gate_ref[...]
    so_ref[...] = s_ref[...]


def _rwkv_params(p):
    z = jnp.zeros((DECAY_LORA, RWKV_WIDTH), F32)
    wwa = jnp.concatenate([jnp.concatenate([p['rwkv_w2'], z], axis=1),
                           jnp.concatenate([z, p['rwkv_a2']], axis=1)], axis=0)
    row = lambda x: x.reshape(1, -1)
    return (row(p['rwkv_mu']), row(p['rwkv_w0']), wwa, row(p['rwkv_a0']), p['rwkv_g2'], row(p['rwkv_k_k']),
            row(p['rwkv_k_a']), row(p['rwkv_r_k']), row(p['rwkv_ln_g']), row(p['rwkv_ln_b']))


def _rwkv(u3, shift0, s0bd, params, tr, c, t_valid):
    n, t, _ = u3.shape
    consts = _rwkv_consts(tr, c)
    hb = tr // SUBLANES
    const2 = lambda b, j: (0, 0)
    full = lambda arr: pl.BlockSpec(arr.shape, const2)
    tile = lambda: pltpu.VMEM((tr, RWKV_WIDTH), F32)
    y, s_out = pl.pallas_call(
        functools.partial(_rwkv_kernel, tr=tr, c=c, t_valid=t_valid),
        out_shape=(jax.ShapeDtypeStruct((n, t, RWKV_WIDTH), F32),
                   jax.ShapeDtypeStruct((n, RWKV_WIDTH, RWKV_WIDTH), F32)),
        grid=(n, t // tr),
        in_specs=[
            pl.BlockSpec((None, tr, RWKV_COLS), lambda b, j: (b, j, 0)),
            pl.BlockSpec((None, SUBLANES, RWKV_COLS), lambda b, j: (b, jnp.maximum(j * hb - 1, 0), 0)),
            pl.BlockSpec((None, 1, RWKV_COLS), lambda b, j: (b, 0, 0)),
            pl.BlockSpec((None, RWKV_WIDTH, RWKV_WIDTH), lambda b, j: (b, 0, 0)),
        ] + [full(x) for x in params] + [full(x) for x in consts],
        out_specs=(pl.BlockSpec((None, tr, RWKV_WIDTH), lambda b, j: (b, j, 0)),
                   pl.BlockSpec((None, RWKV_WIDTH, RWKV_WIDTH), lambda b, j: (b, 0, 0))),
        scratch_shapes=[pltpu.VMEM((SUBLANES + tr, RWKV_COLS), F32)] + [tile() for _ in range(9)]
                       + [pltpu.VMEM((RWKV_WIDTH, RWKV_WIDTH), F32)],
        compiler_params=_cparams(("arbitrary", "arbitrary")),
        name="rwkv7",
    )(u3, u3, shift0, s0bd, *params, *consts)
    return y, s_out


def _state_to_bd(s):
    n = s.shape[0]
    eye = jnp.eye(N_RWKV_HEADS, dtype=s.dtype)
    bd = s[:, :, :, None, :] * eye[None, :, None, :, None]
    return bd.reshape(n, RWKV_WIDTH, RWKV_WIDTH)


def _bd_to_state(bd):
    n = bd.shape[0]
    b5 = bd.reshape(n, N_RWKV_HEADS, HEAD_DIM, N_RWKV_HEADS, HEAD_DIM)
    return jnp.stack([b5[:, h, :, h, :] for h in range(N_RWKV_HEADS)], axis=1)


def _rel_bucket_np(dist):
    d = np.maximum(dist, 1).astype(np.float32)
    large = REL_EXACT + (np.log(d / np.float32(REL_EXACT)) / np.float32(math.log(REL_MAX_DIST / REL_EXACT))
                         * np.float32(N_REL_BUCKETS - REL_EXACT)).astype(np.int32)
    large = np.minimum(large, N_REL_BUCKETS - 1)
    return np.where(dist < REL_EXACT, dist, large)


def _attn_prompt_kernel(*refs, first, last):
    if first:
        q_ref, kc_ref, kp_ref, vc_ref, vp_ref, bias_ref, acc_o, ml_o = refs
    else:
        q_ref, kc_ref, kp_ref, vc_ref, vp_ref, bias_ref, acc_i, ml_i, acc_o, ml_o = refs
    b = pl.program_id(2)
    q = q_ref[...] * (HEAD_DIM ** -0.5)
    kcat = jnp.concatenate([kp_ref[...], kc_ref[...]], axis=0).astype(BF16)
    vcat = jnp.concatenate([vp_ref[...], vc_ref[...]], axis=0).astype(BF16)
    row = lax.broadcasted_iota(jnp.int32, (BLOCK, 2 * BLOCK), 0)
    col = lax.broadcasted_iota(jnp.int32, (BLOCK, 2 * BLOCK), 1)
    delta = row + BLOCK - col
    valid = (delta >= 0) & (delta <= BLOCK) & ((col >= BLOCK) | (b > 0))
    lane = lax.broadcasted_iota(jnp.int32, (BLOCK, 2 * HEAD_DIM), 1)
    lo_half = lane < HEAD_DIM
    lane_kv = lax.broadcasted_iota(jnp.int32, (2 * BLOCK, 2 * HEAD_DIM), 1) < HEAD_DIM
    lane_ml = lax.broadcasted_iota(jnp.int32, (BLOCK, BLOCK), 1)
    ml_new = jnp.zeros((BLOCK, BLOCK), F32)
    if not first:
        ml_old = ml_i[...]
    for pair in range(N_ATT_HEADS // 2):
        ls = slice(pair * 2 * HEAD_DIM, (pair + 1) * 2 * HEAD_DIM)
        qp = q[:, ls]
        kp = kcat[:, ls]
        vp = vcat[:, ls]
        o_pair = None
        m_pair = []
        l_pair = []
        for half in range(2):
            h = 2 * pair + half
            keep = lo_half if half == 0 else ~lo_half
            keep_kv = lane_kv if half == 0 else ~lane_kv
            qh = jnp.where(keep, qp, 0.0).astype(BF16)
            s = lax.dot_general(qh, kp, NT, preferred_element_type=F32) + bias_ref[h]
            s = jnp.where(valid, s, NEG_INF)
            m = jnp.max(s, axis=-1, keepdims=True)
            if not first:
                m_old = ml_old[:, h:h + 1]
                l_old = ml_old[:, N_ATT_HEADS + h:N_ATT_HEADS + h + 1]
                m_new = jnp.maximum(m, m_old)
                alpha = jnp.exp(m_old - m_new)
            else:
                m_new = m
            p = jnp.exp(s - m_new)
            l = jnp.sum(p, axis=-1, keepdims=True)
            if not first:
                l = l + alpha * l_old
            vh = jnp.where(keep_kv, vp, jnp.zeros_like(vp))
            o = lax.dot_general(p.astype(BF16), vh, NN, preferred_element_type=F32)
            if not first:
                o = o + jnp.where(keep, alpha, 0.0) * acc_i[:, ls]
            if last:
                o = o * jnp.where(keep, 1.0 / l, 0.0)
            o_pair = o if o_pair is None else o_pair + o
            ml_new = jnp.where(lane_ml == h, m_new, ml_new)
            ml_new = jnp.where(lane_ml == N_ATT_HEADS + h, l, ml_new)
        acc_o[:, ls] = o_pair
    ml_o[...] = ml_new


def _attn_bias_tables(rel_bias):
    qi = np.arange(BLOCK)[:, None]
    kj = np.arange(2 * BLOCK)[None, :]
    delta = np.maximum(qi + BLOCK - kj, 0)
    tabs = []
    for _, dil in DILATIONS:
        idx = _rel_bucket_np((delta * dil).astype(np.int32))
        tabs.append(jnp.transpose(rel_bias[idx], (2, 0, 1)))
    return tabs


def _attn_prompt(q, k, v, bias_tabs, n, t):
    acc = ml = None
    for bi, (window, dil) in enumerate(DILATIONS):
        first = bi == 0
        last = bi == len(DILATIONS) - 1
        td = t // dil
        nb = td // BLOCK
        view = lambda x, w: x.reshape(n, td, dil * w)
        cur = lambda w: pl.BlockSpec((None, BLOCK, w), lambda a, r, b: (a, b, r))
        prv = lambda w: pl.BlockSpec((None, BLOCK, w), lambda a, r, b: (a, jnp.maximum(b - 1, 0), r))
        ins = [view(q, ATT_WIDTH), view(k, ATT_WIDTH), view(k, ATT_WIDTH), view(v, ATT_WIDTH), view(v, ATT_WIDTH),
               bias_tabs[bi]]
        specs = [cur(ATT_WIDTH), cur(ATT_WIDTH), prv(ATT_WIDTH), cur(ATT_WIDTH), prv(ATT_WIDTH),
                 pl.BlockSpec((N_ATT_HEADS, BLOCK, 2 * BLOCK), lambda a, r, b: (0, 0, 0))]
        aliases = {}
        if not first:
            ins += [view(acc, ATT_WIDTH), view(ml, BLOCK)]
            specs += [cur(ATT_WIDTH), cur(BLOCK)]
            aliases = {6: 0, 7: 1}
        acc, ml = pl.pallas_call(
            functools.partial(_attn_prompt_kernel, first=first, last=last),
            out_shape=(jax.ShapeDtypeStruct((n, td, dil * ATT_WIDTH), F32),
                       jax.ShapeDtypeStruct((n, td, dil * BLOCK), F32)),
            grid=(n, dil, nb),
            in_specs=specs,
            out_specs=(cur(ATT_WIDTH), cur(BLOCK)),
            input_output_aliases=aliases,
            compiler_params=_cparams(("arbitrary", "arbitrary", "arbitrary")),
            name=f"attn_prompt_d{dil}",
        )(*ins)
        acc = acc.reshape(n * t, ATT_WIDTH)
        ml = ml.reshape(n * t, BLOCK)
    return acc


def _attn_sample_kernel(q_ref, kn_ref, vn_ref, k1_ref, k4_ref, k16_ref, v1_ref, v4_ref, v16_ref,
                        bias_ref, bself_ref, hm_ref, o_ref, kt_ref, vt_ref, *, s_new):
    scale = HEAD_DIM ** -0.5
    hm = hm_ref[...]
    tail = BLOCK
    kt_ref[0:tail, :] = k1_ref[...]
    vt_ref[0:tail, :] = v1_ref[...]
    kt_ref[tail:tail + SUBLANES, :] = jnp.zeros((SUBLANES, ATT_WIDTH), F32)
    vt_ref[tail:tail + SUBLANES, :] = jnp.zeros((SUBLANES, ATT_WIDTH), F32)
    kt_ref[tail:tail + s_new, :] = kn_ref[...]
    vt_ref[tail:tail + s_new, :] = vn_ref[...]
    strided = {4: (k4_ref, v4_ref), 16: (k16_ref, v16_ref)}
    for s in range(s_new):
        qm = (q_ref[s:s + 1, :] * scale) * hm
        kself = kn_ref[s:s + 1, :]
        vself = vn_ref[s:s + 1, :]
        s_self = jnp.sum(qm * kself, axis=-1, keepdims=True)
        parts = []
        for bi, (window, dil) in enumerate(DILATIONS):
            if dil == 1:
                kg = kt_ref[pl.ds(s, BLOCK), :]
                vg = vt_ref[pl.ds(s, BLOCK), :]
            else:
                kd_ref, vd_ref = strided[dil]
                kg = kd_ref[:, s * ATT_WIDTH:(s + 1) * ATT_WIDTH]
                vg = vd_ref[:, s * ATT_WIDTH:(s + 1) * ATT_WIDTH]
            sc = _mm(qm, kg, NT) + bias_ref[bi]
            sself = s_self + bself_ref[bi][:, 0:1]
            m = jnp.maximum(jnp.max(sc, axis=-1, keepdims=True), sself)
            p = jnp.exp(sc - m)
            pself = jnp.exp(sself - m)
            l = jnp.sum(p, axis=-1, keepdims=True) + pself
            o = _mm(p, vg) + pself * vself
            parts.append((o, m, l))
        m_all = functools.reduce(jnp.maximum, [m for _, m, _ in parts])
        num = None
        den = None
        for o, m, l in parts:
            wgt = jnp.exp(m - m_all)
            num = wgt * o if num is None else num + wgt * o
            den = wgt * l if den is None else den + wgt * l
        out = (num / den) * hm
        o_ref[s:s + 1, :] = jnp.sum(out, axis=0, keepdims=True)


def _attn_sample_bias(rel_bias):
    steps = BLOCK - np.arange(BLOCK)
    tabs, selfs = [], []
    for _, dil in DILATIONS:
        idx = _rel_bucket_np((steps * dil).astype(np.int32))
        tabs.append(rel_bias[idx].T)
        selfs.append(jnp.broadcast_to(rel_bias[_rel_bucket_np(np.zeros((1,), np.int32))].T, (N_ATT_HEADS, BLOCK)))
    return jnp.stack(tabs), jnp.stack(selfs)


def _attn_sample(q, k, v, k_cache, v_cache, li, bias, bself, n, s_new):
    depth, _, win, _ = k_cache.shape
    assert win == MAX_WINDOW and s_new <= 4
    lanes = np.arange(ATT_WIDTH)
    hm = jnp.asarray((np.arange(N_ATT_HEADS)[:, None] == lanes[None, :] // HEAD_DIM).astype(np.float32))
    new = pl.BlockSpec((None, s_new, ATT_WIDTH), lambda b: (b, 0, 0))
    views, specs = [], []
    for cache in (k_cache, v_cache):
        for _, dil in DILATIONS:
            rows = win // dil
            width = min(dil, 4) * ATT_WIDTH
            views.append(cache.reshape(depth, n, rows, dil * ATT_WIDTH))
            specs.append(pl.BlockSpec((None, None, BLOCK, width),
                                      lambda b, rb=rows // BLOCK - 1: (li, b, rb, 0)))
    r3 = lambda x: x.reshape(n, s_new, ATT_WIDTH)
    out = pl.pallas_call(
        functools.partial(_attn_sample_kernel, s_new=s_new),
        out_shape=jax.ShapeDtypeStruct((n, s_new, ATT_WIDTH), F32),
        grid=(n,),
        in_specs=[new, new, new] + specs + [
            pl.BlockSpec(bias.shape, lambda b: (0, 0, 0)),
            pl.BlockSpec(bself.shape, lambda b: (0, 0, 0)),
            pl.BlockSpec(hm.shape, lambda b: (0, 0))],
        out_specs=new,
        scratch_shapes=[pltpu.VMEM((BLOCK + SUBLANES, ATT_WIDTH), F32),
                        pltpu.VMEM((BLOCK + SUBLANES, ATT_WIDTH), F32)],
        compiler_params=_cparams(("arbitrary",)),
        name="attn_sample",
    )(r3(q), r3(k), r3(v), *views, bias, bself, hm)
    return out.reshape(n * s_new, ATT_WIDTH)


def _out_proj_kernel(ya_ref, yb_ref, yc_ref, x_ref, gate_ref, g_ref, w_ref, o_ref):
    a = CONV_WIDTH
    b = CONV_WIDTH + RWKV_WIDTH
    mix = (_dot1(ya_ref[...], w_ref[0:a, :]) + _dot1(yb_ref[...], w_ref[a:b, :])
           + _dot1(yc_ref[...], w_ref[b:, :]))
    ms = jnp.mean(mix * mix, axis=-1, keepdims=True)
    o_ref[...] = x_ref[...] + gate_ref[...] * (mix * lax.rsqrt(ms + RMS_EPS) * g_ref[...])


def _out_proj(ya, yb, yc, x2, gate, g, w_bf, tm, tiles_per_group):
    m = x2.shape[0]
    rows = lambda w: pl.BlockSpec((tm, w), lambda i: (i, 0))
    return pl.pallas_call(
        _out_proj_kernel,
        out_shape=jax.ShapeDtypeStruct((m, D_MODEL), F32),
        grid=(m // tm,),
        in_specs=[rows(CONV_WIDTH), rows(RWKV_WIDTH), rows(ATT_WIDTH), rows(D_MODEL),
                  _mod_spec(gate, tiles_per_group),
                  pl.BlockSpec((1, D_MODEL), lambda i: (0, 0)),
                  pl.BlockSpec((D_MODEL, D_MODEL), lambda i: (0, 0))],
        out_specs=rows(D_MODEL),
        compiler_params=_cparams(("arbitrary",)),
        name="out_proj",
    )(ya, yb, yc, x2, gate, g.reshape(1, D_MODEL), w_bf)


def _ffn_kernel(x_ref, sc_ref, sh_ref, gate_ref, gpre_ref, gpost_ref, wg_ref, wu_ref, wo_ref, o_ref, h_ref, acc_ref):
    f = pl.program_id(1)

    @pl.when(f == 0)
    def _():
        x = x_ref[...]
        ms = jnp.mean(x * x, axis=-1, keepdims=True)
        h = x * lax.rsqrt(ms + RMS_EPS) * gpre_ref[...]
        h_ref[...] = (h * (1.0 + sc_ref[...]) + sh_ref[...]).astype(BF16)
        acc_ref[...] = jnp.zeros_like(acc_ref)

    hb = h_ref[...]
    gg = lax.dot_general(hb, wg_ref[...], NN, preferred_element_type=F32)
    uu = lax.dot_general(hb, wu_ref[...], NN, preferred_element_type=F32)
    act = (_silu(gg) * uu).astype(BF16)
    acc_ref[...] += lax.dot_general(act, wo_ref[...], NN, preferred_element_type=F32)

    @pl.when(f == pl.num_programs(1) - 1)
    def _():
        y = acc_ref[...]
        ms = jnp.mean(y * y, axis=-1, keepdims=True)
        o_ref[...] = x_ref[...] + gate_ref[...] * (y * lax.rsqrt(ms + RMS_EPS) * gpost_ref[...])


def _ffn(x2, scale, shift, gate, g_pre, g_post, wg, wu, wo, tm, tiles_per_group, tf=256):
    m = x2.shape[0]
    mod = lambda arr: pl.BlockSpec((None,) + arr.shape[1:], lambda i, f: (i // tiles_per_group, 0, 0))
    vec = pl.BlockSpec((1, D_MODEL), lambda i, f: (0, 0))
    return pl.pallas_call(
        _ffn_kernel,
        out_shape=jax.ShapeDtypeStruct((m, D_MODEL), F32),
        grid=(m // tm, D_FF // tf),
        in_specs=[pl.BlockSpec((tm, D_MODEL), lambda i, f: (i, 0)),
                  mod(scale), mod(shift), mod(gate), vec, vec,
                  pl.BlockSpec((D_MODEL, tf), lambda i, f: (0, f)),
                  pl.BlockSpec((D_MODEL, tf), lambda i, f: (0, f)),
                  pl.BlockSpec((tf, D_MODEL), lambda i, f: (f, 0))],
        out_specs=pl.BlockSpec((tm, D_MODEL), lambda i, f: (i, 0)),
        scratch_shapes=[pltpu.VMEM((tm, D_MODEL), BF16), pltpu.VMEM((tm, D_MODEL), F32)],
        compiler_params=_cparams(("arbitrary", "arbitrary")),
        name="ffn",
    )(x2, scale, shift, gate, g_pre.reshape(1, D_MODEL), g_post.reshape(1, D_MODEL), wg, wu, wo)


def _layer(x2, n, t, mods, lp, att_bias, state, li):
    shift1, scale1, gate1, shift2, scale2, gate2 = mods
    prompt = state is None
    tm = 512 if prompt else n * t
    tpg = t // tm if prompt else 1
    uc, ur, q, k, v = _in_proj(x2, scale1, shift1, lp['g_pre_mix'], lp['w_in_bf'], tm, tpg)
    conv_params = _conv_params(lp['conv_w'], lp['conv_b'], lp['conv_ln_g'], lp['conv_ln_b'])
    rwkv_params = _rwkv_params(lp)
    ur3 = ur.reshape(n, t, RWKV_COLS)
    new_shift = ur3[:, -1]
    if prompt:
        ya, conv_cache = _conv_prompt(uc, n, t, conv_params)
        shift0 = jnp.zeros((n, 1, RWKV_COLS), F32)
        s0bd = jnp.zeros((n, RWKV_WIDTH, RWKV_WIDTH), F32)
        yb, s_out = _rwkv(ur3, shift0, s0bd, rwkv_params, tr=512, c=RWKV_CHUNK, t_valid=512)
        yc = _attn_prompt(q, k, v, att_bias['prompt'], n, t)
        win = min(MAX_WINDOW, t)
        new_k = k.reshape(n, t, N_ATT_HEADS, HEAD_DIM)[:, t - win:]
        new_v = v.reshape(n, t, N_ATT_HEADS, HEAD_DIM)[:, t - win:]
    else:
        cache_conv, state_shift, state_wkv, k_cache, v_cache = state
        ya, conv_cache = _conv_sample(uc, cache_conv[li], n, t, conv_params)
        tr = 32
        ur_pad = jnp.pad(ur3, ((0, 0), (0, tr - t), (0, 0)))
        yb, s_out = _rwkv(ur_pad, state_shift[li][:, None, :], _state_to_bd(state_wkv[li]), rwkv_params,
                          tr=tr, c=tr, t_valid=t)
        yb = yb[:, :t]
        yc = _attn_sample(q, k, v, k_cache, v_cache, li, att_bias['sample'], att_bias['sample_self'], n, t)
        new_k = k.reshape(n, t, N_ATT_HEADS, HEAD_DIM)
        new_v = v.reshape(n, t, N_ATT_HEADS, HEAD_DIM)
    yb = yb.reshape(n * t, RWKV_WIDTH)
    x2 = _out_proj(ya, yb, yc, x2, gate1, lp['g_post_mix'], lp['w_out_bf'], tm, tpg)
    tmf = 1024 if prompt else n * t
    x2 = _ffn(x2, scale2, shift2, gate2, lp['g_pre_ffn'], lp['g_post_ffn'], lp['w_ffn_g_bf'], lp['w_ffn_u_bf'],
              lp['w_ffn_o_bf'], tmf, t // tmf if prompt else 1)
    return x2, (conv_cache, new_shift, _bd_to_state(s_out), new_k, new_v)


def kernel(x_prompt, x_sample, c_prompt, c_sample, cache_conv, state_shift, state_wkv, cache_k_win, cache_v_win,
           w_ada, b_ada, g_pre_mix, g_post_mix, g_pre_ffn, g_post_ffn, w_in, w_out,
           conv_w, conv_b, conv_ln_g, conv_ln_b,
           rwkv_mu, rwkv_w0, rwkv_w2, rwkv_a0, rwkv_a2, rwkv_g2, rwkv_k_k, rwkv_k_a, rwkv_r_k,
           rwkv_ln_g, rwkv_ln_b, rel_bias, w_ffn_in, w_ffn_out):
    depth = w_in.shape[0]
    nb, tp, _ = x_prompt.shape
    ns, ts, _ = x_sample.shape
    rows = nb + ns
    rows_pad = -(-rows // SUBLANES) * SUBLANES
    c_all = jnp.pad(jnp.concatenate([c_prompt, c_sample], axis=0), ((0, rows_pad - rows), (0, 0)))
    mod = _ada_modulation(c_all, w_ada, b_ada)
    mod = mod.reshape(depth, rows_pad, 6, D_MODEL)

    att_bias = {'prompt': _attn_bias_tables(rel_bias)}
    att_bias['sample'], att_bias['sample_self'] = _attn_sample_bias(rel_bias)
    k_cache = cache_k_win.reshape(cache_k_win.shape[:3] + (ATT_WIDTH,))
    v_cache = cache_v_win.reshape(cache_v_win.shape[:3] + (ATT_WIDTH,))
    state = (cache_conv, state_shift, state_wkv, k_cache, v_cache)

    yp = x_prompt.reshape(nb * tp, D_MODEL)
    ys = x_sample.reshape(ns * ts, D_MODEL)
    outs_p, outs_s = [], []
    for li in range(depth):
        lp = dict(g_pre_mix=g_pre_mix[li], g_post_mix=g_post_mix[li], g_pre_ffn=g_pre_ffn[li],
                  g_post_ffn=g_post_ffn[li],
                  w_in_bf=w_in[li].astype(BF16), w_out_bf=w_out[li].astype(BF16),
                  w_ffn_g_bf=w_ffn_in[li, :, :D_FF].astype(BF16), w_ffn_u_bf=w_ffn_in[li, :, D_FF:].astype(BF16),
                  w_ffn_o_bf=w_ffn_out[li].astype(BF16),
                  conv_w=conv_w[li], conv_b=conv_b[li], conv_ln_g=conv_ln_g[li], conv_ln_b=conv_ln_b[li],
                  rwkv_mu=rwkv_mu[li], rwkv_w0=rwkv_w0[li], rwkv_w2=rwkv_w2[li], rwkv_a0=rwkv_a0[li],
                  rwkv_a2=rwkv_a2[li], rwkv_g2=rwkv_g2[li], rwkv_k_k=rwkv_k_k[li], rwkv_k_a=rwkv_k_a[li],
                  rwkv_r_k=rwkv_r_k[li], rwkv_ln_g=rwkv_ln_g[li], rwkv_ln_b=rwkv_ln_b[li])
        mods_p = tuple(mod[li, :nb, j][:, None, :] for j in range(6))
        mods_s = tuple(jnp.repeat(mod[li, nb:rows, j], ts, axis=0)[None] for j in range(6))
        yp, st_p = _layer(yp, nb, tp, mods_p, lp, att_bias, None, li)
        ys, st_s = _layer(ys, ns, ts, mods_s, lp, att_bias, state, li)
        outs_p.append(st_p)
        outs_s.append(st_s)
    stack = lambda outs, i: jnp.stack([o[i] for o in outs])
    return (yp.reshape(nb, tp, D_MODEL), ys.reshape(ns, ts, D_MODEL),
            stack(outs_p, 0), stack(outs_s, 0),
            stack(outs_p, 1), stack(outs_s, 1),
            stack(outs_p, 2), stack(outs_s, 2),
            stack(outs_p, 3), stack(outs_s, 3),
            stack(outs_p, 4), stack(outs_s, 4))
```

```python
import functools
import math

import numpy as np
import jax
import jax.numpy as jnp
from jax import lax
from jax.experimental import pallas as pl
from jax.experimental.pallas import tpu as pltpu

F32 = jnp.float32
BF16 = jnp.bfloat16

D_MODEL = 1024
HEAD_DIM = 64
CONV_WIDTH = 256
RWKV_WIDTH = 256
ATT_WIDTH = 512
N_RWKV_HEADS = RWKV_WIDTH // HEAD_DIM
N_ATT_HEADS = ATT_WIDTH // HEAD_DIM
CONV_K = 31
DECAY_LORA = 64
AAA_LORA = 64
GATE_LORA = 128
CONV_COLS = 2 * CONV_WIDTH
RWKV_COLS = 3 * RWKV_WIDTH + DECAY_LORA + AAA_LORA + GATE_LORA
ATT_COLS = 3 * ATT_WIDTH
IN_COLS = CONV_COLS + RWKV_COLS + ATT_COLS
DILATIONS = ((128, 1), (512, 4), (2048, 16))
MAX_WINDOW = 2048
BLOCK = 128
N_REL_BUCKETS = 32
REL_EXACT = N_REL_BUCKETS // 2
REL_MAX_DIST = MAX_WINDOW
D_FF = 2816
RMS_EPS = 1e-6
LN_EPS = 1e-5
RWKV_GN_EPS = HEAD_DIM * 1e-5
DECAY_SCALE = math.exp(-0.5)
NEG_INF = -1e30

VMEM_LIMIT_BYTES = 56 * 1024 * 1024
SUBLANES = 8
LANES = 128
CONV_HALO = 32
RWKV_CHUNK = HEAD_DIM
N_PAIRS = ATT_WIDTH // LANES
ATT_UNIT = BLOCK * max(d for _, d in DILATIONS)

NN = (((1,), (0,)), ((), ()))
NT = (((1,), (1,)), ((), ()))


def _cparams(sem):
    return pltpu.CompilerParams(dimension_semantics=sem, vmem_limit_bytes=VMEM_LIMIT_BYTES)


def _split_bf16(x, n):
    if x.dtype == BF16:
        return [x]
    pieces = []
    r = x
    for i in range(n):
        p = r.astype(BF16)
        pieces.append(p)
        if i + 1 < n:
            r = r - p.astype(F32)
    return pieces


def _mmp(ap, bp, dims=NN):
    order = max(len(ap), len(bp))
    out = None
    for i, x in enumerate(ap):
        for j, y in enumerate(bp):
            if i + j < order:
                t = lax.dot_general(x, y, dims, preferred_element_type=F32)
                out = t if out is None else out + t
    return out


def _mm(a, b, dims=NN, pa=2, pb=2):
    return _mmp(_split_bf16(a, pa), _split_bf16(b, pb), dims)


def _dot1(a, b, dims=NN):
    return lax.dot_general(a.astype(BF16), b.astype(BF16), dims, preferred_element_type=F32)


def _sigmoid(x):
    return 1.0 / (1.0 + jnp.exp(-x))


def _silu(x):
    return x * _sigmoid(x)


def _rel_bucket_np(dist):
    d = np.maximum(dist, 1).astype(np.float32)
    large = REL_EXACT + (np.log(d / np.float32(REL_EXACT)) / np.float32(math.log(REL_MAX_DIST / REL_EXACT))
                         * np.float32(N_REL_BUCKETS - REL_EXACT)).astype(np.int32)
    large = np.minimum(large, N_REL_BUCKETS - 1)
    return np.where(dist < REL_EXACT, dist, large)


def _bias_rows(rel_bias, dist):
    idx = _rel_bucket_np(np.asarray(dist, np.int32))
    onehot = jnp.asarray(np.eye(N_REL_BUCKETS, dtype=np.float32)[idx.reshape(-1)])
    rows = jnp.dot(onehot, rel_bias, precision=lax.Precision.HIGHEST)
    return rows.reshape(idx.shape + (rel_bias.shape[1],))


def _ada_kernel(c_ref, w_ref, b_ref, o_ref):
    a = _silu(c_ref[...])
    o_ref[...] = _dot1(a, w_ref[...]) + b_ref[...]


def _ada_modulation(c_all, w_ada, b_ada):
    depth = w_ada.shape[0]
    rows = c_all.shape[0]
    tn = D_MODEL
    return pl.pallas_call(
        _ada_kernel,
        out_shape=jax.ShapeDtypeStruct((depth, rows, 6 * D_MODEL), F32),
        grid=(depth, 6 * D_MODEL // tn),
        in_specs=[
            pl.BlockSpec((rows, D_MODEL), lambda l, j: (0, 0)),
            pl.BlockSpec((None, D_MODEL, tn), lambda l, j: (l, 0, j)),
            pl.BlockSpec((None, 1, tn), lambda l, j: (l, 0, j)),
        ],
        out_specs=pl.BlockSpec((None, rows, tn), lambda l, j: (l, 0, j)),
        compiler_params=_cparams(("arbitrary", "arbitrary")),
        name="ada_modulation",
    )(c_all, w_ada, b_ada.reshape(depth, 1, 6 * D_MODEL))


def _in_proj_kernel(x_ref, sc_ref, sh_ref, g_ref, w_ref, uc_ref, ur_ref, q_ref, k_ref, v_ref):
    x = x_ref[...]
    ms = jnp.mean(x * x, axis=-1, keepdims=True)
    h = x * lax.rsqrt(ms + RMS_EPS) * g_ref[...]
    h = h * (1.0 + sc_ref[...]) + sh_ref[...]
    hb = h.astype(BF16)
    uc_ref[...] = lax.dot_general(hb, w_ref[:, 0:CONV_COLS], NN, preferred_element_type=F32)
    o = CONV_COLS
    ur_ref[...] = lax.dot_general(hb, w_ref[:, o:o + RWKV_COLS], NN, preferred_element_type=F32)
    o += RWKV_COLS
    for ref in (q_ref, k_ref, v_ref):
        res = lax.dot_general(hb, w_ref[:, o:o + ATT_WIDTH], NN, preferred_element_type=F32)
        for p in range(N_PAIRS):
            ref[p] = res[:, p * LANES:(p + 1) * LANES]
        o += ATT_WIDTH


def _mod_spec(mod, tiles_per_group):
    _, r, d = mod.shape
    return pl.BlockSpec((None, r, d), lambda i: (i // tiles_per_group, 0, 0))


def _in_proj(x2, scale, shift, g, w_bf, tm, tiles_per_group):
    m = x2.shape[0]
    flat = lambda w: jax.ShapeDtypeStruct((m, w), F32)
    pairs = jax.ShapeDtypeStruct((N_PAIRS, m, LANES), F32)
    pair_spec = pl.BlockSpec((N_PAIRS, tm, LANES), lambda i: (0, i, 0))
    return pl.pallas_call(
        _in_proj_kernel,
        out_shape=(flat(CONV_COLS), flat(RWKV_COLS), pairs, pairs, pairs),
        grid=(m // tm,),
        in_specs=[
            pl.BlockSpec((tm, D_MODEL), lambda i: (i, 0)),
            _mod_spec(scale, tiles_per_group),
            _mod_spec(shift, tiles_per_group),
            pl.BlockSpec((1, D_MODEL), lambda i: (0, 0)),
            pl.BlockSpec((D_MODEL, IN_COLS), lambda i: (0, 0)),
        ],
        out_specs=(pl.BlockSpec((tm, CONV_COLS), lambda i: (i, 0)), pl.BlockSpec((tm, RWKV_COLS), lambda i: (i, 0)),
                   pair_spec, pair_spec, pair_spec),
        compiler_params=_cparams(("arbitrary",)),
        name="in_proj",
    )(x2, scale, shift, g.reshape(1, D_MODEL), w_bf)


def _pairs_to_rows(x):
    return jnp.transpose(x, (1, 0, 2)).reshape(x.shape[1], ATT_WIDTH)


def _rows_to_pairs(x):
    return jnp.transpose(x.reshape(x.shape[0], N_PAIRS, LANES), (1, 0, 2))


def _conv_core(zs_ref, w_ref, b_ref, lg_ref, lb_ref, y_ref, tt):
    off = CONV_HALO - (CONV_K - 1)
    sub = min(tt, 64)
    for r0 in range(0, tt, sub):
        acc = jnp.zeros((sub, CONV_WIDTH), F32)
        for j in range(CONV_K):
            acc = acc + w_ref[j:j + 1, :] * zs_ref[pl.ds(r0 + off + j, sub), :]
        y = acc + b_ref[...]
        mu = jnp.mean(y, axis=-1, keepdims=True)
        yc = y - mu
        var = jnp.mean(yc * yc, axis=-1, keepdims=True)
        yn = yc * lax.rsqrt(var + LN_EPS) * lg_ref[...] + lb_ref[...]
        y_ref[r0:r0 + sub, :] = _silu(yn)


def _glu(u):
    return u[:, :CONV_WIDTH] * _sigmoid(u[:, CONV_WIDTH:])


def _conv_prompt_kernel(u_ref, uh_ref, w_ref, b_ref, lg_ref, lb_ref, y_ref, zt_ref, zs_ref, *, tt):
    j = pl.program_id(1)
    zh = _glu(uh_ref[...])
    zs_ref[0:CONV_HALO, :] = jnp.where(j > 0, zh, 0.0)
    zs_ref[CONV_HALO:CONV_HALO + tt, :] = _glu(u_ref[...])
    _conv_core(zs_ref, w_ref, b_ref, lg_ref, lb_ref, y_ref, tt)
    zt_ref[...] = zs_ref[tt:tt + CONV_HALO, :]


def _conv_sample_kernel(u_ref, zh_ref, w_ref, b_ref, lg_ref, lb_ref, y_ref, z_ref, zs_ref, *, tt):
    z = _glu(u_ref[...])
    zs_ref[0:CONV_HALO, :] = zh_ref[...]
    zs_ref[CONV_HALO:CONV_HALO + tt, :] = z
    _conv_core(zs_ref, w_ref, b_ref, lg_ref, lb_ref, y_ref, tt)
    z_ref[...] = z


def _conv_param_specs(nargs_grid):
    cmap = (lambda n, j: (0, 0)) if nargs_grid == 2 else (lambda n: (0, 0))
    return [
        pl.BlockSpec((CONV_HALO, CONV_WIDTH), cmap),
        pl.BlockSpec((1, CONV_WIDTH), cmap),
        pl.BlockSpec((1, CONV_WIDTH), cmap),
        pl.BlockSpec((1, CONV_WIDTH), cmap),
    ]


def _conv_params(conv_w, conv_b, ln_g, ln_b):
    w = jnp.pad(conv_w, ((0, CONV_HALO - CONV_K), (0, 0)))
    return w, conv_b.reshape(1, -1), ln_g.reshape(1, -1), ln_b.reshape(1, -1)


def _conv_prompt(uc, n, t, params, tt=512):
    u3 = uc.reshape(n, t, CONV_COLS)
    hb = tt // CONV_HALO
    y, zt = pl.pallas_call(
        functools.partial(_conv_prompt_kernel, tt=tt),
        out_shape=(jax.ShapeDtypeStruct((n, t, CONV_WIDTH), F32),
                   jax.ShapeDtypeStruct((n, CONV_HALO, CONV_WIDTH), F32)),
        grid=(n, t // tt),
        in_specs=[
            pl.BlockSpec((None, tt, CONV_COLS), lambda b, j: (b, j, 0)),
            pl.BlockSpec((None, CONV_HALO, CONV_COLS), lambda b, j: (b, jnp.maximum(j * hb - 1, 0), 0)),
        ] + _conv_param_specs(2),
        out_specs=(pl.BlockSpec((None, tt, CONV_WIDTH), lambda b, j: (b, j, 0)),
                   pl.BlockSpec((None, CONV_HALO, CONV_WIDTH), lambda b, j: (b, 0, 0))),
        scratch_shapes=[pltpu.VMEM((CONV_HALO + tt, CONV_WIDTH), F32)],
        compiler_params=_cparams(("arbitrary", "arbitrary")),
        name="conv_prompt",
    )(u3, u3, *params)
    return y.reshape(n * t, CONV_WIDTH), zt[:, CONV_HALO - (CONV_K - 1):]


def _conv_sample(uc, cache, n, t, params):
    tt = SUBLANES
    u3 = jnp.pad(uc.reshape(n, t, CONV_COLS), ((0, 0), (0, tt - t), (0, 0)))
    zh = jnp.pad(cache, ((0, 0), (CONV_HALO - (CONV_K - 1), 0), (0, 0)))
    y, z = pl.pallas_call(
        functools.partial(_conv_sample_kernel, tt=tt),
        out_shape=(jax.ShapeDtypeStruct((n, tt, CONV_WIDTH), F32),
                   jax.ShapeDtypeStruct((n, tt, CONV_WIDTH), F32)),
        grid=(n,),
        in_specs=[
            pl.BlockSpec((None, tt, CONV_COLS), lambda b: (b, 0, 0)),
            pl.BlockSpec((None, CONV_HALO, CONV_WIDTH), lambda b: (b, 0, 0)),
        ] + _conv_param_specs(1),
        out_specs=(pl.BlockSpec((None, tt, CONV_WIDTH), lambda b: (b, 0, 0)),
                   pl.BlockSpec((None, tt, CONV_WIDTH), lambda b: (b, 0, 0))),
        scratch_shapes=[pltpu.VMEM((CONV_HALO + tt, CONV_WIDTH), F32)],
        compiler_params=_cparams(("arbitrary",)),
        name="conv_sample",
    )(u3, zh, *params)
    new_cache = jnp.concatenate([cache[:, t:], z[:, :t]], axis=1)
    return y[:, :t].reshape(n * t, CONV_WIDTH), new_cache


def _rwkv_consts():
    c = RWKV_CHUNK
    rows = np.arange(c)
    tri = (rows[None, :] <= rows[:, None]).astype(np.float32)
    lanes = np.arange(RWKV_WIDTH)
    bd = (lanes[:, None] // HEAD_DIM == lanes[None, :] // HEAD_DIM).astype(np.float32)
    return jnp.asarray(tri, BF16), jnp.asarray(bd, BF16)


def _stack(pieces, bd):
    return [jnp.concatenate([p] * N_RWKV_HEADS, axis=0) * bd for p in pieces]


def _rwkv_kernel(u_ref, up_ref, sh0_ref, s0_ref, mu_ref, w0_ref, wwa_ref, a0_ref, g2_ref, kk_ref, ka_ref, rk_ref,
                 lg_ref, lb_ref, tri_ref, bd_ref,
                 y_ref, so_ref,
                 us_ref, lw_ref, kn_ref, kb_ref, k2_ref, rr_ref, vv_ref, bon_ref, gate_ref, yy_ref, s_ref,
                 *, tr, t_valid, np_):
    j = pl.program_id(1)
    c = RWKV_CHUNK

    @pl.when(j == 0)
    def _():
        s_ref[...] = s0_ref[...]

    u = u_ref[...]
    prev = jnp.where(j > 0, up_ref[...], jnp.broadcast_to(sh0_ref[...], (SUBLANES, RWKV_COLS)))
    us_ref[0:SUBLANES, :] = prev
    us_ref[SUBLANES:SUBLANES + tr, :] = u
    u_prev = us_ref[pl.ds(SUBLANES - 1, tr), :]
    xs = u + (u_prev - u) * mu_ref[...]
    o = 3 * RWKV_WIDTH
    r = xs[:, :RWKV_WIDTH]
    k = xs[:, RWKV_WIDTH:2 * RWKV_WIDTH]
    v = xs[:, 2 * RWKV_WIDTH:o]
    lo = xs[:, o:o + DECAY_LORA + AAA_LORA]
    lane = lax.broadcasted_iota(jnp.int32, lo.shape, 1)
    lo = jnp.where(lane < DECAY_LORA, jnp.tanh(lo), lo)
    wa = _mm(lo, wwa_ref[...])
    g = _mm(_sigmoid(xs[:, o + DECAY_LORA + AAA_LORA:]), g2_ref[...])
    lw = -DECAY_SCALE * _sigmoid(w0_ref[...] + wa[:, :RWKV_WIDTH])
    a = _sigmoid(a0_ref[...] + wa[:, RWKV_WIDTH:])
    bd = bd_ref[...]
    kkr = k * kk_ref[...]
    ss = _mm(kkr * kkr, bd, pa=3)
    kk = kkr / jnp.maximum(jnp.sqrt(ss), 1e-12)
    k2 = k * (1.0 + (a - 1.0) * ka_ref[...])
    if t_valid < tr:
        valid = lax.broadcasted_iota(jnp.int32, (tr, RWKV_WIDTH), 0) < t_valid
        zero = jnp.zeros_like(r)
        r, k2, v, kk, lw = (jnp.where(valid, z, zero) for z in (r, k2, v, kk, lw))
    lw_ref[...] = lw
    kn_ref[...] = -kk
    kb_ref[...] = kk * a
    k2_ref[...] = k2
    rr_ref[...] = r
    vv_ref[...] = v
    bon_ref[...] = _mm(r * k2 * rk_ref[...], bd, pa=3) * v
    gate_ref[...] = g

    row = lax.broadcasted_iota(jnp.int32, (c, RWKV_WIDTH), 0)
    col = lax.broadcasted_iota(jnp.int32, (c, RWKV_WIDTH), 1) & (HEAD_DIM - 1)
    strict = col < row
    incl = col <= row
    ident = jnp.where(col == row, 1.0, 0.0)
    bd_mask = bd > 0.5
    n_steps = int(round(math.log2(c)))
    split = lambda x: _split_bf16(x, np_)

    def chunk(ci, carry):
        sl = pl.ds(pl.multiple_of(ci * c, c), c)
        lw_c = lw_ref[sl, :]
        gi = _mmp([tri_ref[...]], _split_bf16(lw_c, 3))
        e_neg = jnp.exp(-gi)
        gc = jnp.exp(gi[c - 1:c, :])
        pt = kn_ref[sl, :] * jnp.exp(gi - lw_c)
        bt = kb_ref[sl, :] * e_neg
        kt = k2_ref[sl, :] * e_neg
        rt = rr_ref[sl, :] * jnp.exp(gi)
        vc = vv_ref[sl, :]
        pp, bp, kp, rp, vp = split(pt), split(bt), split(kt), split(rt), split(vc)
        s_b, s_k, s_v, s_p = _stack(bp, bd), _stack(kp, bd), _stack(vp, bd), _stack(pp, bd)
        pr = [jnp.concatenate([x, y], axis=0) for x, y in zip(pp, rp)]
        xb = _mmp(pr, s_b, NT)
        xk = _mmp(pr, s_k, NT)
        zero = jnp.zeros((c, RWKV_WIDTH), F32)
        lm = jnp.where(strict, xb[:c], zero)
        mm_ = jnp.where(strict, xk[:c], zero)
        qb = jnp.where(incl, xb[c:], zero)
        qk = jnp.where(incl, xk[c:], zero)
        tinv = ident + lm
        lp = split(lm)
        lk = _mmp(lp, _stack(lp, bd))
        for step in range(1, n_steps):
            lp = split(lk)
            tinv = tinv + _mmp(lp, _stack(split(tinv), bd))
            if step + 1 < n_steps:
                lk = _mmp(lp, _stack(lp, bd))
        mv = _mmp(split(mm_), s_v)
        tp = split(tinv)
        p2 = _mmp(tp, s_p)
        u0 = _mmp(tp, _stack(split(mv), bd))
        s0 = s_ref[...]
        p2r = [jnp.concatenate([x, y], axis=0) for x, y in zip(split(p2), rp)]
        xs_ = _mmp(p2r, _stack(split(s0), bd), NT)
        uu = xs_[:c] + u0
        yy_ref[sl, :] = xs_[c:] + _mmp(split(qb), _stack(split(uu), bd)) + _mmp(split(qk), s_v)
        uvt = jnp.concatenate([uu, vc], axis=0).T
        bk = jnp.concatenate([bt * gc, kt * gc], axis=0)
        z = jnp.where(bd_mask, _mmp(split(uvt), split(bk)), 0.0)
        fold = z[0:c]
        for h in range(1, N_RWKV_HEADS):
            fold = fold + z[h * c:(h + 1) * c]
        s_ref[...] = s0 * gc + fold
        return carry

    lax.fori_loop(0, tr // c, chunk, 0)

    y = yy_ref[...]
    inv = 1.0 / HEAD_DIM
    mu = _mm(y, bd, pa=3) * inv
    yc = y - mu
    var = _mm(yc * yc, bd, pa=3) * inv
    yn = yc * lax.rsqrt(var + RWKV_GN_EPS) * lg_ref[...] + lb_ref[...]
    y_ref[...] = (yn + bon_ref[...]) * gate_ref[...]
    so_ref[...] = s_ref[...]


def _rwkv_params(p):
    z = jnp.zeros((DECAY_LORA, RWKV_WIDTH), F32)
    wwa = jnp.concatenate([jnp.concatenate([p['rwkv_w2'], z], axis=1),
                           jnp.concatenate([z, p['rwkv_a2']], axis=1)], axis=0)
    row = lambda x: x.reshape(1, -1)
    return (row(p['rwkv_mu']), row(p['rwkv_w0']), wwa, row(p['rwkv_a0']), p['rwkv_g2'], row(p['rwkv_k_k']),
            row(p['rwkv_k_a']), row(p['rwkv_r_k']), row(p['rwkv_ln_g']), row(p['rwkv_ln_b']))


def _rwkv(u3, shift0, s0, params, tr, t_valid, np_=2):
    n, t, _ = u3.shape
    consts = _rwkv_consts()
    hb = tr // SUBLANES
    const2 = lambda b, j: (0, 0)
    full = lambda arr: pl.BlockSpec(arr.shape, const2)
    tile = lambda: pltpu.VMEM((tr, RWKV_WIDTH), F32)
    y, s_out = pl.pallas_call(
        functools.partial(_rwkv_kernel, tr=tr, t_valid=t_valid, np_=np_),
        out_shape=(jax.ShapeDtypeStruct((n, t, RWKV_WIDTH), F32),
                   jax.ShapeDtypeStruct((n, HEAD_DIM, RWKV_WIDTH), F32)),
        grid=(n, t // tr),
        in_specs=[
            pl.BlockSpec((None, tr, RWKV_COLS), lambda b, j: (b, j, 0)),
            pl.BlockSpec((None, SUBLANES, RWKV_COLS), lambda b, j: (b, jnp.maximum(j * hb - 1, 0), 0)),
            pl.BlockSpec((None, 1, RWKV_COLS), lambda b, j: (b, 0, 0)),
            pl.BlockSpec((None, HEAD_DIM, RWKV_WIDTH), lambda b, j: (b, 0, 0)),
        ] + [full(x) for x in params] + [full(x) for x in consts],
        out_specs=(pl.BlockSpec((None, tr, RWKV_WIDTH), lambda b, j: (b, j, 0)),
                   pl.BlockSpec((None, HEAD_DIM, RWKV_WIDTH), lambda b, j: (b, 0, 0))),
        scratch_shapes=[pltpu.VMEM((SUBLANES + tr, RWKV_COLS), F32)] + [tile() for _ in range(9)]
                       + [pltpu.VMEM((HEAD_DIM, RWKV_WIDTH), F32)],
        compiler_params=_cparams(("arbitrary", "arbitrary")),
        name="rwkv7",
    )(u3, u3, shift0, s0, *params, *consts)
    return y, s_out


def _state_to_lanes(s):
    n = s.shape[0]
    return jnp.transpose(s, (0, 2, 1, 3)).reshape(n, HEAD_DIM, RWKV_WIDTH)


def _lanes_to_state(s):
    n = s.shape[0]
    return jnp.transpose(s.reshape(n, HEAD_DIM, N_RWKV_HEADS, HEAD_DIM), (0, 2, 1, 3))


def _attn_prompt_kernel(q_ref, kc_ref, kp_ref, vc_ref, vp_ref, w_ref, o_ref,
                        kcat, vcat, acc, m0, m1, l0, l1, bias_s):
    u = pl.program_id(2)
    unit = ATT_UNIT
    scale = HEAD_DIM ** -0.5
    kcat[0:unit, :] = kp_ref[...]
    kcat[unit:2 * unit, :] = kc_ref[...]
    vcat[0:unit, :] = vp_ref[...]
    vcat[unit:2 * unit, :] = vc_ref[...]
    for bi in range(len(DILATIONS)):
        for half in range(2):
            wrow = jnp.broadcast_to(w_ref[bi, half:half + 1, :], (BLOCK, 2 * BLOCK))
            bias_s[2 * bi + half] = pltpu.roll(wrow, 0, 1, stride=1, stride_axis=0)
    row = lax.broadcasted_iota(jnp.int32, (BLOCK, 2 * BLOCK), 0)
    col = lax.broadcasted_iota(jnp.int32, (BLOCK, 2 * BLOCK), 1)
    delta = row + BLOCK - col
    band = (delta >= 0) & (delta <= BLOCK)
    lo = lax.broadcasted_iota(jnp.int32, (BLOCK, LANES), 1) < HEAD_DIM
    lo_kv = lax.broadcasted_iota(jnp.int32, (2 * BLOCK, LANES), 1) < HEAD_DIM
    stats = ((m0, l0), (m1, l1))
    n_br = len(DILATIONS)
    for bi, (window, dil) in enumerate(DILATIONS):
        first = bi == 0
        last = bi == n_br - 1
        shift = int(round(math.log2(dil)))

        def body(j, carry, bi=bi, dil=dil, shift=shift, first=first, last=last):
            r = j & (dil - 1)
            blk = j >> shift
            start = blk * (BLOCK * dil) + r
            rows_q = pl.ds(start, BLOCK, stride=dil)
            rows_k = pl.ds(unit + start - BLOCK * dil, 2 * BLOCK, stride=dil)
            q = q_ref[rows_q, :] * scale
            kk = kcat[rows_k, :].astype(BF16)
            vv = vcat[rows_k, :]
            valid = band & ((col >= BLOCK) | (u > 0) | (blk > 0))
            o_pair = None
            alpha_pair = None
            linv_pair = None
            for half in range(2):
                keep = lo if half == 0 else ~lo
                keep_kv = lo_kv if half == 0 else ~lo_kv
                m_ref, l_ref = stats[half]
                qh = jnp.where(keep, q, 0.0).astype(BF16)
                s = lax.dot_general(qh, kk, NT, preferred_element_type=F32) + bias_s[2 * bi + half]
                s = jnp.where(valid, s, NEG_INF)
                m_cur = jnp.max(s, axis=-1, keepdims=True)
                if first:
                    m_new = jnp.broadcast_to(m_cur, (BLOCK, LANES))
                else:
                    m_old = m_ref[rows_q, :]
                    m_new = jnp.maximum(m_old, m_cur)
                    alpha = jnp.exp(m_old - m_new)
                p = jnp.exp(s - jnp.concatenate([m_new, m_new], axis=1))
                l_new = jnp.sum(p, axis=-1, keepdims=True)
                if first:
                    l_new = jnp.broadcast_to(l_new, (BLOCK, LANES))
                else:
                    l_new = alpha * l_ref[rows_q, :] + l_new
                    alpha_pair = jnp.where(keep, alpha, 0.0) if alpha_pair is None else jnp.where(keep, alpha,
                                                                                                 alpha_pair)
                vh = jnp.where(keep_kv, vv, 0.0).astype(BF16)
                o = lax.dot_general(p.astype(BF16), vh, NN, preferred_element_type=F32)
                o_pair = o if o_pair is None else o_pair + o
                if last:
                    linv = 1.0 / l_new
                    linv_pair = jnp.where(keep, linv, 0.0) if linv_pair is None else jnp.where(keep, linv, linv_pair)
                else:
                    m_ref[rows_q, :] = m_new
                    l_ref[rows_q, :] = l_new
            if not first:
                o_pair = o_pair + alpha_pair * acc[rows_q, :]
            if last:
                o_ref[rows_q, :] = o_pair * linv_pair
            else:
                acc[rows_q, :] = o_pair
            return carry

        lax.fori_loop(0, unit // BLOCK, body, 0)


def _attn_prompt_bias(rel_bias):
    c = np.arange(2 * BLOCK)
    tabs = []
    for _, dil in DILATIONS:
        tabs.append(_bias_rows(rel_bias, np.maximum(BLOCK - c, 0) * dil).T)
    w = jnp.stack(tabs)
    return jnp.transpose(w.reshape(len(DILATIONS), N_PAIRS, 2, 2 * BLOCK), (1, 0, 2, 3))


def _attn_prompt(q, k, v, wtab, n, t):
    unit = ATT_UNIT
    assert t % unit == 0
    r4 = lambda x: x.reshape(N_PAIRS, n, t, LANES)
    cur = pl.BlockSpec((None, None, unit, LANES), lambda a, p, u: (p, a, u, 0))
    prv = pl.BlockSpec((None, None, unit, LANES), lambda a, p, u: (p, a, jnp.maximum(u - 1, 0), 0))
    stat = lambda: pltpu.VMEM((unit, LANES), F32)
    out = pl.pallas_call(
        _attn_prompt_kernel,
        out_shape=jax.ShapeDtypeStruct((N_PAIRS, n, t, LANES), F32),
        grid=(n, N_PAIRS, t // unit),
        in_specs=[cur, cur, prv, cur, prv,
                  pl.BlockSpec((None, len(DILATIONS), 2, 2 * BLOCK), lambda a, p, u: (p, 0, 0, 0))],
        out_specs=cur,
        scratch_shapes=[pltpu.VMEM((2 * unit, LANES), F32), pltpu.VMEM((2 * unit, LANES), F32),
                        stat(), stat(), stat(), stat(), stat(),
                        pltpu.VMEM((2 * len(DILATIONS), BLOCK, 2 * BLOCK), F32)],
        compiler_params=_cparams(("arbitrary", "arbitrary", "arbitrary")),
        name="attn_prompt",
    )(r4(q), r4(k), r4(k), r4(v), r4(v), wtab)
    return out.reshape(N_PAIRS, n * t, LANES)


def _attn_sample_kernel(q_ref, kn_ref, vn_ref, kc_ref, vc_ref, bt_ref, bn_ref, cnt_ref, cn_ref, hm_ref, o_ref,
                        kn_s, vn_s, *, s_new):
    scale = HEAD_DIM ** -0.5
    hm = hm_ref[...]
    qm = jnp.concatenate([(q_ref[s:s + 1, :] * scale) * hm for s in range(s_new)], axis=0).astype(BF16)
    kn_s[...] = jnp.zeros_like(kn_s)
    vn_s[...] = jnp.zeros_like(vn_s)
    kn_s[0:s_new, :] = kn_ref[...]
    vn_s[0:s_new, :] = vn_ref[...]
    sc = lax.dot_general(qm, kc_ref[...].astype(BF16), NN, preferred_element_type=F32) + bt_ref[...]
    sn = lax.dot_general(qm, kn_s[...].astype(BF16), NT, preferred_element_type=F32) + bn_ref[...]
    cnt = cnt_ref[...]
    cn = cn_ref[...]
    sc = jnp.where(cnt > 0.0, sc, NEG_INF)
    sn = jnp.where(cn > 0.0, sn, NEG_INF)
    m = jnp.maximum(jnp.max(sc, axis=-1, keepdims=True), jnp.max(sn, axis=-1, keepdims=True))
    p = cnt * jnp.exp(sc - m)
    pn = cn * jnp.exp(sn - m)
    l = jnp.sum(p, axis=-1, keepdims=True) + jnp.sum(pn, axis=-1, keepdims=True)
    o = (lax.dot_general(p.astype(BF16), vc_ref[...].astype(BF16), NT, preferred_element_type=F32)
         + lax.dot_general(pn.astype(BF16), vn_s[...].astype(BF16), NN, preferred_element_type=F32))
    o = o / l
    for s in range(s_new):
        o_ref[s:s + 1, :] = jnp.sum(o[s * N_ATT_HEADS:(s + 1) * N_ATT_HEADS] * hm, axis=0, keepdims=True)


def _attn_sample_tables(rel_bias, s_new, win):
    s = np.arange(s_new)[:, None]
    dist_c = win + s - np.arange(win)[None, :]
    dist_n = s - np.arange(LANES)[None, :]
    cnt_c = np.zeros(dist_c.shape, np.float32)
    cnt_n = np.zeros(dist_n.shape, np.float32)
    for window, dil in DILATIONS:
        cnt_c += ((dist_c % dil == 0) & (dist_c <= window)).astype(np.float32)
        cnt_n += ((dist_n >= 0) & (dist_n % dil == 0) & (dist_n <= window)).astype(np.float32)
    rep = lambda x: jnp.asarray(np.repeat(x, N_ATT_HEADS, axis=0))
    rows = lambda b: jnp.transpose(b, (0, 2, 1)).reshape(s_new * N_ATT_HEADS, -1)
    return (rows(_bias_rows(rel_bias, dist_c)), rows(_bias_rows(rel_bias, np.maximum(dist_n, 0))),
            rep(cnt_c), rep(cnt_n))


def _attn_sample(q, k, v, k_cache, v_cache, li, tables, n, s_new):
    win = k_cache.shape[-1]
    assert win == MAX_WINDOW
    lanes = np.arange(ATT_WIDTH)
    hm = jnp.asarray((np.arange(N_ATT_HEADS)[:, None] == lanes[None, :] // HEAD_DIM).astype(np.float32))
    new = pl.BlockSpec((None, s_new, ATT_WIDTH), lambda b: (b, 0, 0))
    buf = pl.BlockSpec((None, None, ATT_WIDTH, win), lambda b: (li, b, 0, 0))
    full = lambda arr: pl.BlockSpec(arr.shape, lambda b: (0, 0))
    r3 = lambda x: x.reshape(n, s_new, ATT_WIDTH)
    out = pl.pallas_call(
        functools.partial(_attn_sample_kernel, s_new=s_new),
        out_shape=jax.ShapeDtypeStruct((n, s_new, ATT_WIDTH), F32),
        grid=(n,),
        in_specs=[new, new, new, buf, buf] + [full(x) for x in tables] + [full(hm)],
        out_specs=new,
        scratch_shapes=[pltpu.VMEM((LANES, ATT_WIDTH), F32), pltpu.VMEM((LANES, ATT_WIDTH), F32)],
        compiler_params=_cparams(("arbitrary",)),
        name="attn_sample",
    )(r3(q), r3(k), r3(v), k_cache, v_cache, *tables, hm)
    return out.reshape(n * s_new, ATT_WIDTH)


def _out_proj_kernel(ya_ref, yb_ref, yc_ref, x_ref, gate_ref, g_ref, w_ref, o_ref):
    a = CONV_WIDTH
    b = CONV_WIDTH + RWKV_WIDTH
    yc = jnp.concatenate([yc_ref[p] for p in range(N_PAIRS)], axis=1)
    mix = _dot1(ya_ref[...], w_ref[0:a, :]) + _dot1(yb_ref[...], w_ref[a:b, :]) + _dot1(yc, w_ref[b:, :])
    ms = jnp.mean(mix * mix, axis=-1, keepdims=True)
    o_ref[...] = x_ref[...] + gate_ref[...] * (mix * lax.rsqrt(ms + RMS_EPS) * g_ref[...])


def _out_proj(ya, yb, yc, x2, gate, g, w_bf, tm, tiles_per_group):
    m = x2.shape[0]
    rows = lambda w: pl.BlockSpec((tm, w), lambda i: (i, 0))
    return pl.pallas_call(
        _out_proj_kernel,
        out_shape=jax.ShapeDtypeStruct((m, D_MODEL), F32),
        grid=(m // tm,),
        in_specs=[rows(CONV_WIDTH), rows(RWKV_WIDTH),
                  pl.BlockSpec((N_PAIRS, tm, LANES), lambda i: (0, i, 0)), rows(D_MODEL),
                  _mod_spec(gate, tiles_per_group),
                  pl.BlockSpec((1, D_MODEL), lambda i: (0, 0)),
                  pl.BlockSpec((D_MODEL, D_MODEL), lambda i: (0, 0))],
        out_specs=rows(D_MODEL),
        compiler_params=_cparams(("arbitrary",)),
        name="out_proj",
    )(ya, yb, yc, x2, gate, g.reshape(1, D_MODEL), w_bf)


def _ffn_kernel(x_ref, sc_ref, sh_ref, gate_ref, gpre_ref, gpost_ref, wg_ref, wu_ref, wo_ref, o_ref, h_ref, acc_ref):
    f = pl.program_id(1)

    @pl.when(f == 0)
    def _():
        x = x_ref[...]
        ms = jnp.mean(x * x, axis=-1, keepdims=True)
        h = x * lax.rsqrt(ms + RMS_EPS) * gpre_ref[...]
        h_ref[...] = (h * (1.0 + sc_ref[...]) + sh_ref[...]).astype(BF16)
        acc_ref[...] = jnp.zeros_like(acc_ref)

    hb = h_ref[...]
    gg = lax.dot_general(hb, wg_ref[...], NN, preferred_element_type=F32)
    uu = lax.dot_general(hb, wu_ref[...], NN, preferred_element_type=F32)
    act = (_silu(gg) * uu).astype(BF16)
    acc_ref[...] += lax.dot_general(act, wo_ref[...], NN, preferred_element_type=F32)

    @pl.when(f == pl.num_programs(1) - 1)
    def _():
        y = acc_ref[...]
        ms = jnp.mean(y * y, axis=-1, keepdims=True)
        o_ref[...] = x_ref[...] + gate_ref[...] * (y * lax.rsqrt(ms + RMS_EPS) * gpost_ref[...])


def _ffn(x2, scale, shift, gate, g_pre, g_post, wg, wu, wo, tm, tiles_per_group, tf=256):
    m = x2.shape[0]
    mod = lambda arr: pl.BlockSpec((None,) + arr.shape[1:], lambda i, f: (i // tiles_per_group, 0, 0))
    vec = pl.BlockSpec((1, D_MODEL), lambda i, f: (0, 0))
    return pl.pallas_call(
        _ffn_kernel,
        out_shape=jax.ShapeDtypeStruct((m, D_MODEL), F32),
        grid=(m // tm, D_FF // tf),
        in_specs=[pl.BlockSpec((tm, D_MODEL), lambda i, f: (i, 0)),
                  mod(scale), mod(shift), mod(gate), vec, vec,
                  pl.BlockSpec((D_MODEL, tf), lambda i, f: (0, f)),
                  pl.BlockSpec((D_MODEL, tf), lambda i, f: (0, f)),
                  pl.BlockSpec((tf, D_MODEL), lambda i, f: (f, 0))],
        out_specs=pl.BlockSpec((tm, D_MODEL), lambda i, f: (i, 0)),
        scratch_shapes=[pltpu.VMEM((tm, D_MODEL), BF16), pltpu.VMEM((tm, D_MODEL), F32)],
        compiler_params=_cparams(("arbitrary", "arbitrary")),
        name="ffn",
    )(x2, scale, shift, gate, g_pre.reshape(1, D_MODEL), g_post.reshape(1, D_MODEL), wg, wu, wo)


def _heads(x_pairs, n, t):
    return _pairs_to_rows(x_pairs).reshape(n, t, N_ATT_HEADS, HEAD_DIM)


def _layer(x2, n, t, mods, lp, att, state, li):
    shift1, scale1, gate1, shift2, scale2, gate2 = mods
    prompt = state is None
    tm = 512 if prompt else n * t
    tpg = t // tm if prompt else 1
    uc, ur, q, k, v = _in_proj(x2, scale1, shift1, lp['g_pre_mix'], lp['w_in_bf'], tm, tpg)
    conv_params = _conv_params(lp['conv_w'], lp['conv_b'], lp['conv_ln_g'], lp['conv_ln_b'])
    rwkv_params = _rwkv_params(lp)
    ur3 = ur.reshape(n, t, RWKV_COLS)
    new_shift = ur3[:, -1]
    if prompt:
        ya, conv_cache = _conv_prompt(uc, n, t, conv_params)
        shift0 = jnp.zeros((n, 1, RWKV_COLS), F32)
        s0 = jnp.zeros((n, HEAD_DIM, RWKV_WIDTH), F32)
        yb, s_out = _rwkv(ur3, shift0, s0, rwkv_params, tr=512, t_valid=512)
        yc = _attn_prompt(q, k, v, att['prompt'], n, t)
        win = min(MAX_WINDOW, t)
        r4 = lambda x: x.reshape(N_PAIRS, n, t, LANES)[:, :, t - win:].reshape(N_PAIRS, n * win, LANES)
        new_k = _heads(r4(k), n, win)
        new_v = _heads(r4(v), n, win)
    else:
        cache_conv, state_shift, state_wkv, k_cache, v_cache = state
        ya, conv_cache = _conv_sample(uc, cache_conv[li], n, t, conv_params)
        tr = RWKV_CHUNK
        ur_pad = jnp.pad(ur3, ((0, 0), (0, tr - t), (0, 0)))
        yb, s_out = _rwkv(ur_pad, state_shift[li][:, None, :], _state_to_lanes(state_wkv[li]), rwkv_params,
                          tr=tr, t_valid=t)
        yb = yb[:, :t]
        q_r, k_r, v_r = _pairs_to_rows(q), _pairs_to_rows(k), _pairs_to_rows(v)
        yc = _rows_to_pairs(_attn_sample(q_r, k_r, v_r, k_cache, v_cache, li, att['sample'], n, t))
        new_k = k_r.reshape(n, t, N_ATT_HEADS, HEAD_DIM)
        new_v = v_r.reshape(n, t, N_ATT_HEADS, HEAD_DIM)
    yb = yb.reshape(n * t, RWKV_WIDTH)
    x2 = _out_proj(ya, yb, yc, x2, gate1, lp['g_post_mix'], lp['w_out_bf'], tm, tpg)
    tmf = 1024 if prompt else n * t
    x2 = _ffn(x2, scale2, shift2, gate2, lp['g_pre_ffn'], lp['g_post_ffn'], lp['w_ffn_g_bf'], lp['w_ffn_u_bf'],
              lp['w_ffn_o_bf'], tmf, t // tmf if prompt else 1)
    return x2, (conv_cache, new_shift, _lanes_to_state(s_out), new_k, new_v)


def kernel(x_prompt, x_sample, c_prompt, c_sample, cache_conv, state_shift, state_wkv, cache_k_win, cache_v_win,
           w_ada, b_ada, g_pre_mix, g_post_mix, g_pre_ffn, g_post_ffn, w_in, w_out,
           conv_w, conv_b, conv_ln_g, conv_ln_b,
           rwkv_mu, rwkv_w0, rwkv_w2, rwkv_a0, rwkv_a2, rwkv_g2, rwkv_k_k, rwkv_k_a, rwkv_r_k,
           rwkv_ln_g, rwkv_ln_b, rel_bias, w_ffn_in, w_ffn_out):
    depth = w_in.shape[0]
    nb, tp, _ = x_prompt.shape
    ns, ts, _ = x_sample.shape
    rows = nb + ns
    rows_pad = -(-rows // SUBLANES) * SUBLANES
    c_all = jnp.pad(jnp.concatenate([c_prompt, c_sample], axis=0), ((0, rows_pad - rows), (0, 0)))
    mod = _ada_modulation(c_all, w_ada, b_ada)
    mod = mod.reshape(depth, rows_pad, 6, D_MODEL)

    win = cache_k_win.shape[2]
    att = {'prompt': _attn_prompt_bias(rel_bias), 'sample': _attn_sample_tables(rel_bias, ts, win)}
    to_pos_minor = lambda c: jnp.transpose(c, (0, 1, 3, 4, 2)).reshape(depth, ns, ATT_WIDTH, win)
    state = (cache_conv, state_shift, state_wkv, to_pos_minor(cache_k_win), to_pos_minor(cache_v_win))

    yp = x_prompt.reshape(nb * tp, D_MODEL)
    ys = x_sample.reshape(ns * ts, D_MODEL)
    outs_p, outs_s = [], []
    for li in range(depth):
        lp = dict(g_pre_mix=g_pre_mix[li], g_post_mix=g_post_mix[li], g_pre_ffn=g_pre_ffn[li],
                  g_post_ffn=g_post_ffn[li],
                  w_in_bf=w_in[li].astype(BF16), w_out_bf=w_out[li].astype(BF16),
                  w_ffn_g_bf=w_ffn_in[li, :, :D_FF].astype(BF16), w_ffn_u_bf=w_ffn_in[li, :, D_FF:].astype(BF16),
                  w_ffn_o_bf=w_ffn_out[li].astype(BF16),
                  conv_w=conv_w[li], conv_b=conv_b[li], conv_ln_g=conv_ln_g[li], conv_ln_b=conv_ln_b[li],
                  rwkv_mu=rwkv_mu[li], rwkv_w0=rwkv_w0[li], rwkv_w2=rwkv_w2[li], rwkv_a0=rwkv_a0[li],
                  rwkv_a2=rwkv_a2[li], rwkv_g2=rwkv_g2[li], rwkv_k_k=rwkv_k_k[li], rwkv_k_a=rwkv_k_a[li],
                  rwkv_r_k=rwkv_r_k[li], rwkv_ln_g=rwkv_ln_g[li], rwkv_ln_b=rwkv_ln_b[li])
        mods_p = tuple(mod[li, :nb, j][:, None, :] for j in range(6))
        mods_s = tuple(jnp.repeat(mod[li, nb:rows, j], ts, axis=0)[None] for j in range(6))
        yp, st_p = _layer(yp, nb, tp, mods_p, lp, att, None, li)
        ys, st_s = _layer(ys, ns, ts, mods_s, lp, att, state, li)
        outs_p.append(st_p)
        outs_s.append(st_s)
    stack = lambda outs, i: jnp.stack([o[i] for o in outs])
    return (yp.reshape(nb, tp, D_MODEL), ys.reshape(ns, ts, D_MODEL),
            stack(outs_p, 0), stack(outs_s, 0),
            stack(outs_p, 1), stack(outs_s, 1),
            stack(outs_p, 2), stack(outs_s, 2),
            stack(outs_p, 3), stack(outs_s, 3),
            stack(outs_p, 4), stack(outs_s, 4))
```

```python
import functools
import math

import numpy as np
import jax
import jax.numpy as jnp
from jax import lax
from jax.experimental import pallas as pl
from jax.experimental.pallas import tpu as pltpu

F32 = jnp.float32
BF16 = jnp.bfloat16

D_MODEL = 1024
HEAD_DIM = 64
CONV_WIDTH = 256
RWKV_WIDTH = 256
ATT_WIDTH = 512
N_RWKV_HEADS = RWKV_WIDTH // HEAD_DIM
N_ATT_HEADS = ATT_WIDTH // HEAD_DIM
CONV_K = 31
DECAY_LORA = 64
AAA_LORA = 64
GATE_LORA = 128
CONV_COLS = 2 * CONV_WIDTH
RWKV_COLS = 3 * RWKV_WIDTH + DECAY_LORA + AAA_LORA + GATE_LORA
ATT_COLS = 3 * ATT_WIDTH
IN_COLS = CONV_COLS + RWKV_COLS + ATT_COLS
DILATIONS = ((128, 1), (512, 4), (2048, 16))
MAX_WINDOW = 2048
BLOCK = 128
N_REL_BUCKETS = 32
REL_EXACT = N_REL_BUCKETS // 2
REL_MAX_DIST = MAX_WINDOW
D_FF = 2816
RMS_EPS = 1e-6
LN_EPS = 1e-5
RWKV_GN_EPS = HEAD_DIM * 1e-5
DECAY_SCALE = math.exp(-0.5)
NEG_INF = -1e30

VMEM_LIMIT_BYTES = 56 * 1024 * 1024
SUBLANES = 8
LANES = 128
CONV_HALO = 32
RWKV_CHUNK = HEAD_DIM
N_PAIRS = ATT_WIDTH // LANES
ATT_UNIT = BLOCK * max(d for _, d in DILATIONS)
ATT_GROUP = 4

NN = (((1,), (0,)), ((), ()))
NT = (((1,), (1,)), ((), ()))


def _cparams(sem):
    return pltpu.CompilerParams(dimension_semantics=sem, vmem_limit_bytes=VMEM_LIMIT_BYTES)


def _split_bf16(x, n):
    if x.dtype == BF16:
        return [x]
    pieces = []
    r = x
    for i in range(n):
        p = r.astype(BF16)
        pieces.append(p)
        if i + 1 < n:
            r = r - p.astype(F32)
    return pieces


def _mmp(ap, bp, dims=NN):
    order = max(len(ap), len(bp))
    out = None
    for i, x in enumerate(ap):
        for j, y in enumerate(bp):
            if i + j < order:
                t = lax.dot_general(x, y, dims, preferred_element_type=F32)
                out = t if out is None else out + t
    return out


def _mm(a, b, dims=NN, pa=2, pb=2):
    return _mmp(_split_bf16(a, pa), _split_bf16(b, pb), dims)


def _dot1(a, b, dims=NN):
    return lax.dot_general(a.astype(BF16), b.astype(BF16), dims, preferred_element_type=F32)


def _lockstep(gens):
    results = [None] * len(gens)
    while any(r is None for r in results):
        results = [next(g) if r is None else r for g, r in zip(gens, results)]
    return results


def _sigmoid(x):
    return 1.0 / (1.0 + jnp.exp(-x))


def _silu(x):
    return x * _sigmoid(x)


def _rel_bucket_np(dist):
    d = np.maximum(dist, 1).astype(np.float32)
    large = REL_EXACT + (np.log(d / np.float32(REL_EXACT)) / np.float32(math.log(REL_MAX_DIST / REL_EXACT))
                         * np.float32(N_REL_BUCKETS - REL_EXACT)).astype(np.int32)
    large = np.minimum(large, N_REL_BUCKETS - 1)
    return np.where(dist < REL_EXACT, dist, large)


def _bias_rows(rel_bias, dist):
    idx = _rel_bucket_np(np.asarray(dist, np.int32))
    onehot = jnp.asarray(np.eye(N_REL_BUCKETS, dtype=np.float32)[idx.reshape(-1)])
    rows = jnp.dot(onehot, rel_bias, precision=lax.Precision.HIGHEST)
    return rows.reshape(idx.shape + (rel_bias.shape[1],))


def _ada_kernel(c_ref, w_ref, b_ref, o_ref):
    a = _silu(c_ref[...])
    o_ref[...] = _dot1(a, w_ref[...]) + b_ref[...]


def _ada_modulation(c_all, w_ada, b_ada):
    depth = w_ada.shape[0]
    rows = c_all.shape[0]
    tn = D_MODEL
    return pl.pallas_call(
        _ada_kernel,
        out_shape=jax.ShapeDtypeStruct((depth, rows, 6 * D_MODEL), F32),
        grid=(depth, 6 * D_MODEL // tn),
        in_specs=[
            pl.BlockSpec((rows, D_MODEL), lambda l, j: (0, 0)),
            pl.BlockSpec((None, D_MODEL, tn), lambda l, j: (l, 0, j)),
            pl.BlockSpec((None, 1, tn), lambda l, j: (l, 0, j)),
        ],
        out_specs=pl.BlockSpec((None, rows, tn), lambda l, j: (l, 0, j)),
        compiler_params=_cparams(("arbitrary", "arbitrary")),
        name="ada_modulation",
    )(c_all, w_ada, b_ada.reshape(depth, 1, 6 * D_MODEL))


def _in_proj_kernel(x_ref, sc_ref, sh_ref, g_ref, w_ref, uc_ref, ur_ref, q_ref, k_ref, v_ref):
    x = x_ref[...]
    ms = jnp.mean(x * x, axis=-1, keepdims=True)
    h = x * lax.rsqrt(ms + RMS_EPS) * g_ref[...]
    h = h * (1.0 + sc_ref[...]) + sh_ref[...]
    hb = h.astype(BF16)
    uc_ref[...] = lax.dot_general(hb, w_ref[:, 0:CONV_COLS], NN, preferred_element_type=F32)
    o = CONV_COLS
    ur_ref[...] = lax.dot_general(hb, w_ref[:, o:o + RWKV_COLS], NN, preferred_element_type=F32)
    o += RWKV_COLS
    for ref in (q_ref, k_ref, v_ref):
        res = lax.dot_general(hb, w_ref[:, o:o + ATT_WIDTH], NN, preferred_element_type=F32)
        for p in range(N_PAIRS):
            ref[p] = res[:, p * LANES:(p + 1) * LANES]
        o += ATT_WIDTH


def _mod_spec(mod, tiles_per_group):
    _, r, d = mod.shape
    return pl.BlockSpec((None, r, d), lambda i: (i // tiles_per_group, 0, 0))


def _in_proj(x2, scale, shift, g, w_bf, tm, tiles_per_group):
    m = x2.shape[0]
    flat = lambda w: jax.ShapeDtypeStruct((m, w), F32)
    pairs = jax.ShapeDtypeStruct((N_PAIRS, m, LANES), F32)
    pair_spec = pl.BlockSpec((N_PAIRS, tm, LANES), lambda i: (0, i, 0))
    return pl.pallas_call(
        _in_proj_kernel,
        out_shape=(flat(CONV_COLS), flat(RWKV_COLS), pairs, pairs, pairs),
        grid=(m // tm,),
        in_specs=[
            pl.BlockSpec((tm, D_MODEL), lambda i: (i, 0)),
            _mod_spec(scale, tiles_per_group),
            _mod_spec(shift, tiles_per_group),
            pl.BlockSpec((1, D_MODEL), lambda i: (0, 0)),
            pl.BlockSpec((D_MODEL, IN_COLS), lambda i: (0, 0)),
        ],
        out_specs=(pl.BlockSpec((tm, CONV_COLS), lambda i: (i, 0)), pl.BlockSpec((tm, RWKV_COLS), lambda i: (i, 0)),
                   pair_spec, pair_spec, pair_spec),
        compiler_params=_cparams(("arbitrary",)),
        name="in_proj",
    )(x2, scale, shift, g.reshape(1, D_MODEL), w_bf)


def _pairs_to_rows(x):
    return jnp.transpose(x, (1, 0, 2)).reshape(x.shape[1], ATT_WIDTH)


def _rows_to_pairs(x):
    return jnp.transpose(x.reshape(x.shape[0], N_PAIRS, LANES), (1, 0, 2))


def _conv_core(zs_ref, w_ref, b_ref, lg_ref, lb_ref, y_ref, tt):
    off = CONV_HALO - (CONV_K - 1)
    sub = min(tt, 64)
    for r0 in range(0, tt, sub):
        acc = jnp.zeros((sub, CONV_WIDTH), F32)
        for j in range(CONV_K):
            acc = acc + w_ref[j:j + 1, :] * zs_ref[pl.ds(r0 + off + j, sub), :]
        y = acc + b_ref[...]
        mu = jnp.mean(y, axis=-1, keepdims=True)
        yc = y - mu
        var = jnp.mean(yc * yc, axis=-1, keepdims=True)
        yn = yc * lax.rsqrt(var + LN_EPS) * lg_ref[...] + lb_ref[...]
        y_ref[r0:r0 + sub, :] = _silu(yn)


def _glu(u):
    return u[:, :CONV_WIDTH] * _sigmoid(u[:, CONV_WIDTH:])


def _conv_prompt_kernel(u_ref, uh_ref, w_ref, b_ref, lg_ref, lb_ref, y_ref, zt_ref, zs_ref, *, tt):
    j = pl.program_id(1)
    zh = _glu(uh_ref[...])
    zs_ref[0:CONV_HALO, :] = jnp.where(j > 0, zh, 0.0)
    zs_ref[CONV_HALO:CONV_HALO + tt, :] = _glu(u_ref[...])
    _conv_core(zs_ref, w_ref, b_ref, lg_ref, lb_ref, y_ref, tt)
    zt_ref[...] = zs_ref[tt:tt + CONV_HALO, :]


def _conv_sample_kernel(u_ref, zh_ref, w_ref, b_ref, lg_ref, lb_ref, y_ref, z_ref, zs_ref, *, tt):
    z = _glu(u_ref[...])
    zs_ref[0:CONV_HALO, :] = zh_ref[...]
    zs_ref[CONV_HALO:CONV_HALO + tt, :] = z
    _conv_core(zs_ref, w_ref, b_ref, lg_ref, lb_ref, y_ref, tt)
    z_ref[...] = z


def _conv_param_specs(nargs_grid):
    cmap = (lambda n, j: (0, 0)) if nargs_grid == 2 else (lambda n: (0, 0))
    return [
        pl.BlockSpec((CONV_HALO, CONV_WIDTH), cmap),
        pl.BlockSpec((1, CONV_WIDTH), cmap),
        pl.BlockSpec((1, CONV_WIDTH), cmap),
        pl.BlockSpec((1, CONV_WIDTH), cmap),
    ]


def _conv_params(conv_w, conv_b, ln_g, ln_b):
    w = jnp.pad(conv_w, ((0, CONV_HALO - CONV_K), (0, 0)))
    return w, conv_b.reshape(1, -1), ln_g.reshape(1, -1), ln_b.reshape(1, -1)


def _conv_prompt(uc, n, t, params, tt=512):
    u3 = uc.reshape(n, t, CONV_COLS)
    hb = tt // CONV_HALO
    y, zt = pl.pallas_call(
        functools.partial(_conv_prompt_kernel, tt=tt),
        out_shape=(jax.ShapeDtypeStruct((n, t, CONV_WIDTH), F32),
                   jax.ShapeDtypeStruct((n, CONV_HALO, CONV_WIDTH), F32)),
        grid=(n, t // tt),
        in_specs=[
            pl.BlockSpec((None, tt, CONV_COLS), lambda b, j: (b, j, 0)),
            pl.BlockSpec((None, CONV_HALO, CONV_COLS), lambda b, j: (b, jnp.maximum(j * hb - 1, 0), 0)),
        ] + _conv_param_specs(2),
        out_specs=(pl.BlockSpec((None, tt, CONV_WIDTH), lambda b, j: (b, j, 0)),
                   pl.BlockSpec((None, CONV_HALO, CONV_WIDTH), lambda b, j: (b, 0, 0))),
        scratch_shapes=[pltpu.VMEM((CONV_HALO + tt, CONV_WIDTH), F32)],
        compiler_params=_cparams(("arbitrary", "arbitrary")),
        name="conv_prompt",
    )(u3, u3, *params)
    return y.reshape(n * t, CONV_WIDTH), zt[:, CONV_HALO - (CONV_K - 1):]


def _conv_sample(uc, cache, n, t, params):
    tt = SUBLANES
    u3 = jnp.pad(uc.reshape(n, t, CONV_COLS), ((0, 0), (0, tt - t), (0, 0)))
    zh = jnp.pad(cache, ((0, 0), (CONV_HALO - (CONV_K - 1), 0), (0, 0)))
    y, z = pl.pallas_call(
        functools.partial(_conv_sample_kernel, tt=tt),
        out_shape=(jax.ShapeDtypeStruct((n, tt, CONV_WIDTH), F32),
                   jax.ShapeDtypeStruct((n, tt, CONV_WIDTH), F32)),
        grid=(n,),
        in_specs=[
            pl.BlockSpec((None, tt, CONV_COLS), lambda b: (b, 0, 0)),
            pl.BlockSpec((None, CONV_HALO, CONV_WIDTH), lambda b: (b, 0, 0)),
        ] + _conv_param_specs(1),
        out_specs=(pl.BlockSpec((None, tt, CONV_WIDTH), lambda b: (b, 0, 0)),
                   pl.BlockSpec((None, tt, CONV_WIDTH), lambda b: (b, 0, 0))),
        scratch_shapes=[pltpu.VMEM((CONV_HALO + tt, CONV_WIDTH), F32)],
        compiler_params=_cparams(("arbitrary",)),
        name="conv_sample",
    )(u3, zh, *params)
    new_cache = jnp.concatenate([cache[:, t:], z[:, :t]], axis=1)
    return y[:, :t].reshape(n * t, CONV_WIDTH), new_cache


def _rwkv_consts():
    c = RWKV_CHUNK
    rows = np.arange(c)
    tri = (rows[None, :] <= rows[:, None]).astype(np.float32)
    lanes = np.arange(RWKV_WIDTH)
    bd = (lanes[:, None] // HEAD_DIM == lanes[None, :] // HEAD_DIM).astype(np.float32)
    return jnp.asarray(tri, BF16), jnp.asarray(bd, BF16)


def _rwkv_kernel(u_ref, up_ref, sh0_ref, s0_ref, mu_ref, w0_ref, wwa_ref, a0_ref, g2_ref, kk_ref, ka_ref, rk_ref,
                 lg_ref, lb_ref, tri_ref, bd_ref,
                 y_ref, so_ref,
                 us_ref, lw_ref, kn_ref, kb_ref, k2_ref, rr_ref, vv_ref, bon_ref, gate_ref, yy_ref, s_ref,
                 *, tr, t_valid, nseq):
    j = pl.program_id(1)
    c = RWKV_CHUNK
    bd = bd_ref[...]
    o = 3 * RWKV_WIDTH

    @pl.when(j == 0)
    def _():
        s_ref[...] = s0_ref[...]

    for sq in range(nseq):
        u = u_ref[sq]
        prev = jnp.where(j > 0, up_ref[sq], jnp.broadcast_to(sh0_ref[sq], (SUBLANES, RWKV_COLS)))
        us_ref[sq, 0:SUBLANES, :] = prev
        us_ref[sq, SUBLANES:SUBLANES + tr, :] = u
        u_prev = us_ref[sq, pl.ds(SUBLANES - 1, tr), :]
        xs = u + (u_prev - u) * mu_ref[...]
        r = xs[:, :RWKV_WIDTH]
        k = xs[:, RWKV_WIDTH:2 * RWKV_WIDTH]
        v = xs[:, 2 * RWKV_WIDTH:o]
        lo = xs[:, o:o + DECAY_LORA + AAA_LORA]
        lane = lax.broadcasted_iota(jnp.int32, lo.shape, 1)
        lo = jnp.where(lane < DECAY_LORA, jnp.tanh(lo), lo)
        wa = _mm(lo, wwa_ref[...])
        g = _mm(_sigmoid(xs[:, o + DECAY_LORA + AAA_LORA:]), g2_ref[...])
        lw = -DECAY_SCALE * _sigmoid(w0_ref[...] + wa[:, :RWKV_WIDTH])
        a = _sigmoid(a0_ref[...] + wa[:, RWKV_WIDTH:])
        kkr = k * kk_ref[...]
        ss = _mm(kkr * kkr, bd, pa=3)
        kk = kkr / jnp.maximum(jnp.sqrt(ss), 1e-12)
        k2 = k * (1.0 + (a - 1.0) * ka_ref[...])
        if t_valid < tr:
            valid = lax.broadcasted_iota(jnp.int32, (tr, RWKV_WIDTH), 0) < t_valid
            zero = jnp.zeros_like(r)
            r, k2, v, kk, lw = (jnp.where(valid, z, zero) for z in (r, k2, v, kk, lw))
        lw_ref[sq] = lw
        kn_ref[sq] = -kk
        kb_ref[sq] = kk * a
        k2_ref[sq] = k2
        rr_ref[sq] = r
        vv_ref[sq] = v
        bon_ref[sq] = _mm(r * k2 * rk_ref[...], bd, pa=3) * v
        gate_ref[sq] = g

    row = lax.broadcasted_iota(jnp.int32, (c, LANES), 0)
    col = lax.broadcasted_iota(jnp.int32, (c, LANES), 1) & (HEAD_DIM - 1)
    strict = col < row
    incl = col <= row
    ident = jnp.where(col == row, 1.0, 0.0)
    bd2 = bd[0:LANES, 0:LANES]
    bd2_mask = bd2 > 0.5
    n_steps = int(round(math.log2(c)))

    def dot(a, b, dims=NN):
        return lax.dot_general(a.astype(BF16), b, dims, preferred_element_type=F32)

    def stack(x):
        xb = x.astype(BF16)
        return jnp.concatenate([xb, xb], axis=0) * bd2

    def rows2(a, b):
        return jnp.concatenate([a, b], axis=0)

    def chunk_pair(pt, bt, kt, rt, vc, gc, s0):
        zero = jnp.zeros((c, LANES), F32)
        pr = rows2(pt, rt)
        xb = dot(pr, stack(bt), NT)
        xk = dot(pr, stack(kt), NT)
        yield None
        lm = jnp.where(strict, xb[:c], zero)
        mm_ = jnp.where(strict, xk[:c], zero)
        qb = jnp.where(incl, xb[c:], zero)
        qk = jnp.where(incl, xk[c:], zero)
        tinv = ident + lm
        lk = dot(lm, stack(lm))
        mq = dot(rows2(mm_, qk), stack(vc))
        yield None
        for step in range(1, n_steps):
            if step + 1 < n_steps:
                both = dot(rows2(lk, tinv), stack(lk))
                lk = both[:c]
                tinv = tinv + both[c:]
            else:
                tinv = tinv + dot(tinv, stack(lk))
            yield None
        p2 = dot(tinv, stack(pt))
        u0 = dot(tinv, stack(mq[:c]))
        yield None
        xs_ = dot(rows2(p2, rt), stack(s0), NT)
        yield None
        uu = xs_[:c] + u0
        y = xs_[c:] + dot(qb, stack(uu)) + mq[c:]
        uvt = rows2(uu, vc).T
        bk = rows2(bt * gc, kt * gc).astype(BF16)
        z = jnp.where(bd2_mask, dot(uvt, bk), 0.0)
        yield y, s0 * gc + z[0:c] + z[c:2 * c]

    def chunk(ci, carry):
        sl = pl.ds(pl.multiple_of(ci * c, c), c)
        work = []
        for sq in range(nseq):
            lw_c = lw_ref[sq, sl, :]
            gi = _mmp([tri_ref[...]], _split_bf16(lw_c, 3))
            e_neg = jnp.exp(-gi)
            gc = jnp.exp(gi[c - 1:c, :])
            pt = kn_ref[sq, sl, :] * jnp.exp(gi - lw_c)
            bt = kb_ref[sq, sl, :] * e_neg
            kt = k2_ref[sq, sl, :] * e_neg
            rt = rr_ref[sq, sl, :] * jnp.exp(gi)
            vc = vv_ref[sq, sl, :]
            s0 = s_ref[sq]
            for p in range(RWKV_WIDTH // LANES):
                ln = slice(p * LANES, (p + 1) * LANES)
                work.append((sq, ln, tuple(x[:, ln] for x in (pt, bt, kt, rt, vc, gc, s0))))
        results = _lockstep([chunk_pair(*args) for _, _, args in work])
        done = [(sq, ln, res) for (sq, ln, _), res in zip(work, results)]
        for sq, ln, (y, s_new) in done:
            yy_ref[sq, sl, ln] = y
            s_ref[sq, :, ln] = s_new
        return carry

    lax.fori_loop(0, tr // c, chunk, 0)

    inv = 1.0 / HEAD_DIM
    for sq in range(nseq):
        y = yy_ref[sq]
        mu = _mm(y, bd, pa=3) * inv
        yc = y - mu
        var = _mm(yc * yc, bd, pa=3) * inv
        yn = yc * lax.rsqrt(var + RWKV_GN_EPS) * lg_ref[...] + lb_ref[...]
        y_ref[sq] = (yn + bon_ref[sq]) * gate_ref[sq]
    so_ref[...] = s_ref[...]


def _rwkv_params(p):
    z = jnp.zeros((DECAY_LORA, RWKV_WIDTH), F32)
    wwa = jnp.concatenate([jnp.concatenate([p['rwkv_w2'], z], axis=1),
                           jnp.concatenate([z, p['rwkv_a2']], axis=1)], axis=0)
    row = lambda x: x.reshape(1, -1)
    return (row(p['rwkv_mu']), row(p['rwkv_w0']), wwa, row(p['rwkv_a0']), p['rwkv_g2'], row(p['rwkv_k_k']),
            row(p['rwkv_k_a']), row(p['rwkv_r_k']), row(p['rwkv_ln_g']), row(p['rwkv_ln_b']))


def _rwkv(u3, shift0, s0, params, tr, t_valid, nseq):
    n, t, _ = u3.shape
    assert n % nseq == 0
    consts = _rwkv_consts()
    hb = tr // SUBLANES
    const2 = lambda b, j: (0, 0)
    full = lambda arr: pl.BlockSpec(arr.shape, const2)
    tile = lambda: pltpu.VMEM((nseq, tr, RWKV_WIDTH), F32)
    y, s_out = pl.pallas_call(
        functools.partial(_rwkv_kernel, tr=tr, t_valid=t_valid, nseq=nseq),
        out_shape=(jax.ShapeDtypeStruct((n, t, RWKV_WIDTH), F32),
                   jax.ShapeDtypeStruct((n, HEAD_DIM, RWKV_WIDTH), F32)),
        grid=(n // nseq, t // tr),
        in_specs=[
            pl.BlockSpec((nseq, tr, RWKV_COLS), lambda b, j: (b, j, 0)),
            pl.BlockSpec((nseq, SUBLANES, RWKV_COLS), lambda b, j: (b, jnp.maximum(j * hb - 1, 0), 0)),
            pl.BlockSpec((nseq, 1, RWKV_COLS), lambda b, j: (b, 0, 0)),
            pl.BlockSpec((nseq, HEAD_DIM, RWKV_WIDTH), lambda b, j: (b, 0, 0)),
        ] + [full(x) for x in params] + [full(x) for x in consts],
        out_specs=(pl.BlockSpec((nseq, tr, RWKV_WIDTH), lambda b, j: (b, j, 0)),
                   pl.BlockSpec((nseq, HEAD_DIM, RWKV_WIDTH), lambda b, j: (b, 0, 0))),
        scratch_shapes=[pltpu.VMEM((nseq, SUBLANES + tr, RWKV_COLS), F32)] + [tile() for _ in range(9)]
                       + [pltpu.VMEM((nseq, HEAD_DIM, RWKV_WIDTH), F32)],
        compiler_params=_cparams(("arbitrary", "arbitrary")),
        name="rwkv7",
    )(u3, u3, shift0, s0, *params, *consts)
    return y, s_out


def _state_to_lanes(s):
    n = s.shape[0]
    return jnp.transpose(s, (0, 2, 1, 3)).reshape(n, HEAD_DIM, RWKV_WIDTH)


def _lanes_to_state(s):
    n = s.shape[0]
    return jnp.transpose(s.reshape(n, HEAD_DIM, N_RWKV_HEADS, HEAD_DIM), (0, 2, 1, 3))


def _attn_prompt_kernel(q_ref, kc_ref, kp_ref, vc_ref, vp_ref, w_ref, o_ref,
                        kcat, vcat, acc, m0, m1, l0, l1, bias_s):
    u = pl.program_id(2)
    unit = ATT_UNIT
    scale = HEAD_DIM ** -0.5
    kcat[0:unit, :] = kp_ref[...]
    kcat[unit:2 * unit, :] = kc_ref[...]
    vcat[0:unit, :] = vp_ref[...]
    vcat[unit:2 * unit, :] = vc_ref[...]
    for bi in range(len(DILATIONS)):
        for half in range(2):
            wrow = jnp.broadcast_to(w_ref[bi, half:half + 1, :], (BLOCK, 2 * BLOCK))
            bias_s[2 * bi + half] = pltpu.roll(wrow, 0, 1, stride=1, stride_axis=0)
    row = lax.broadcasted_iota(jnp.int32, (BLOCK, 2 * BLOCK), 0)
    col = lax.broadcasted_iota(jnp.int32, (BLOCK, 2 * BLOCK), 1)
    delta = row + BLOCK - col
    band = (delta >= 0) & (delta <= BLOCK)
    lo = lax.broadcasted_iota(jnp.int32, (BLOCK, LANES), 1) < HEAD_DIM
    lo_kv = lax.broadcasted_iota(jnp.int32, (2 * BLOCK, LANES), 1) < HEAD_DIM
    stats = ((m0, l0), (m1, l1))
    n_br = len(DILATIONS)
    for bi, (window, dil) in enumerate(DILATIONS):
        first = bi == 0
        last = bi == n_br - 1
        shift = int(round(math.log2(dil)))

        def sub_block(j, bi, dil, shift, first, last):
            stores = []
            r = j & (dil - 1)
            blk = j >> shift
            start = blk * (BLOCK * dil) + r
            rows_q = pl.ds(start, BLOCK, stride=dil)
            rows_k = pl.ds(unit + start - BLOCK * dil, 2 * BLOCK, stride=dil)
            q = q_ref[rows_q, :] * scale
            kk = kcat[rows_k, :].astype(BF16)
            vv = vcat[rows_k, :]
            valid = band & ((col >= BLOCK) | (u > 0) | (blk > 0))
            keeps = (lo, ~lo)
            scores = [lax.dot_general(jnp.where(keep, q, 0.0).astype(BF16), kk, NT, preferred_element_type=F32)
                      for keep in keeps]
            old = None if first else [(m_ref[rows_q, :], l_ref[rows_q, :]) for m_ref, l_ref in stats]
            acc_old = None if first else acc[rows_q, :]
            yield None
            outs, alphas, ls = [], [], []
            for half in range(2):
                keep_kv = lo_kv if half == 0 else ~lo_kv
                s = jnp.where(valid, scores[half] + bias_s[2 * bi + half], NEG_INF)
                m_cur = jnp.max(s, axis=-1, keepdims=True)
                if first:
                    m_new = jnp.broadcast_to(m_cur, (BLOCK, LANES))
                else:
                    m_old, l_old = old[half]
                    m_new = jnp.maximum(m_old, m_cur)
                    alpha = jnp.exp(m_old - m_new)
                    alphas.append(alpha)
                p = jnp.exp(s - jnp.concatenate([m_new, m_new], axis=1))
                l_new = jnp.sum(p, axis=-1, keepdims=True)
                if first:
                    l_new = jnp.broadcast_to(l_new, (BLOCK, LANES))
                else:
                    l_new = alpha * l_old + l_new
                ls.append(l_new)
                vh = jnp.where(keep_kv, vv, 0.0).astype(BF16)
                outs.append(lax.dot_general(p.astype(BF16), vh, NN, preferred_element_type=F32))
                if not last:
                    m_ref, l_ref = stats[half]
                    stores += [(m_ref, rows_q, m_new), (l_ref, rows_q, l_new)]
            yield None
            o_pair = outs[0] + outs[1]
            if not first:
                o_pair = o_pair + jnp.where(lo, alphas[0], alphas[1]) * acc_old
            if last:
                stores.append((o_ref, rows_q, o_pair * jnp.where(lo, 1.0 / ls[0], 1.0 / ls[1])))
            else:
                stores.append((acc, rows_q, o_pair))
            yield stores

        def body(jg, carry, bi=bi, dil=dil, shift=shift, first=first, last=last):
            groups = _lockstep([sub_block(jg * ATT_GROUP + g, bi, dil, shift, first, last)
                                for g in range(ATT_GROUP)])
            for stores in groups:
                for ref, rows, val in stores:
                    ref[rows, :] = val
            return carry

        lax.fori_loop(0, unit // BLOCK // ATT_GROUP, body, 0)


def _attn_prompt_bias(rel_bias):
    c = np.arange(2 * BLOCK)
    tabs = []
    for _, dil in DILATIONS:
        tabs.append(_bias_rows(rel_bias, np.maximum(BLOCK - c, 0) * dil).T)
    w = jnp.stack(tabs)
    return jnp.transpose(w.reshape(len(DILATIONS), N_PAIRS, 2, 2 * BLOCK), (1, 0, 2, 3))


def _attn_prompt(q, k, v, wtab, n, t):
    unit = ATT_UNIT
    assert t % unit == 0
    r4 = lambda x: x.reshape(N_PAIRS, n, t, LANES)
    cur = pl.BlockSpec((None, None, unit, LANES), lambda a, p, u: (p, a, u, 0))
    prv = pl.BlockSpec((None, None, unit, LANES), lambda a, p, u: (p, a, jnp.maximum(u - 1, 0), 0))
    stat = lambda: pltpu.VMEM((unit, LANES), F32)
    out = pl.pallas_call(
        _attn_prompt_kernel,
        out_shape=jax.ShapeDtypeStruct((N_PAIRS, n, t, LANES), F32),
        grid=(n, N_PAIRS, t // unit),
        in_specs=[cur, cur, prv, cur, prv,
                  pl.BlockSpec((None, len(DILATIONS), 2, 2 * BLOCK), lambda a, p, u: (p, 0, 0, 0))],
        out_specs=cur,
        scratch_shapes=[pltpu.VMEM((2 * unit, LANES), F32), pltpu.VMEM((2 * unit, LANES), F32),
                        stat(), stat(), stat(), stat(), stat(),
                        pltpu.VMEM((2 * len(DILATIONS), BLOCK, 2 * BLOCK), F32)],
        compiler_params=_cparams(("arbitrary", "arbitrary", "arbitrary")),
        name="attn_prompt",
    )(r4(q), r4(k), r4(k), r4(v), r4(v), wtab)
    return out.reshape(N_PAIRS, n * t, LANES)


def _attn_sample_kernel(q_ref, kn_ref, vn_ref, kc_ref, vc_ref, bt_ref, bn_ref, cnt_ref, cn_ref, hm_ref, o_ref,
                        kn_s, vn_s, *, s_new):
    scale = HEAD_DIM ** -0.5
    hm = hm_ref[...]
    qm = jnp.concatenate([(q_ref[s:s + 1, :] * scale) * hm for s in range(s_new)], axis=0).astype(BF16)
    kn_s[...] = jnp.zeros_like(kn_s)
    vn_s[...] = jnp.zeros_like(vn_s)
    kn_s[0:s_new, :] = kn_ref[...]
    vn_s[0:s_new, :] = vn_ref[...]
    sc = lax.dot_general(qm, kc_ref[...].astype(BF16), NN, preferred_element_type=F32) + bt_ref[...]
    sn = lax.dot_general(qm, kn_s[...].astype(BF16), NT, preferred_element_type=F32) + bn_ref[...]
    cnt = cnt_ref[...]
    cn = cn_ref[...]
    sc = jnp.where(cnt > 0.0, sc, NEG_INF)
    sn = jnp.where(cn > 0.0, sn, NEG_INF)
    m = jnp.maximum(jnp.max(sc, axis=-1, keepdims=True), jnp.max(sn, axis=-1, keepdims=True))
    p = cnt * jnp.exp(sc - m)
    pn = cn * jnp.exp(sn - m)
    l = jnp.sum(p, axis=-1, keepdims=True) + jnp.sum(pn, axis=-1, keepdims=True)
    o = (lax.dot_general(p.astype(BF16), vc_ref[...].astype(BF16), NT, preferred_element_type=F32)
         + lax.dot_general(pn.astype(BF16), vn_s[...].astype(BF16), NN, preferred_element_type=F32))
    o = o / l
    for s in range(s_new):
        o_ref[s:s + 1, :] = jnp.sum(o[s * N_ATT_HEADS:(s + 1) * N_ATT_HEADS] * hm, axis=0, keepdims=True)


def _attn_sample_tables(rel_bias, s_new, win):
    s = np.arange(s_new)[:, None]
    dist_c = win + s - np.arange(win)[None, :]
    dist_n = s - np.arange(LANES)[None, :]
    cnt_c = np.zeros(dist_c.shape, np.float32)
    cnt_n = np.zeros(dist_n.shape, np.float32)
    for window, dil in DILATIONS:
        cnt_c += ((dist_c % dil == 0) & (dist_c <= window)).astype(np.float32)
        cnt_n += ((dist_n >= 0) & (dist_n % dil == 0) & (dist_n <= window)).astype(np.float32)
    rep = lambda x: jnp.asarray(np.repeat(x, N_ATT_HEADS, axis=0))
    rows = lambda b: jnp.transpose(b, (0, 2, 1)).reshape(s_new * N_ATT_HEADS, -1)
    return (rows(_bias_rows(rel_bias, dist_c)), rows(_bias_rows(rel_bias, np.maximum(dist_n, 0))),
            rep(cnt_c), rep(cnt_n))


def _attn_sample(q, k, v, k_cache, v_cache, li, tables, n, s_new):
    win = k_cache.shape[-1]
    assert win == MAX_WINDOW
    lanes = np.arange(ATT_WIDTH)
    hm = jnp.asarray((np.arange(N_ATT_HEADS)[:, None] == lanes[None, :] // HEAD_DIM).astype(np.float32))
    new = pl.BlockSpec((None, s_new, ATT_WIDTH), lambda b: (b, 0, 0))
    buf = pl.BlockSpec((None, None, ATT_WIDTH, win), lambda b: (li, b, 0, 0))
    full = lambda arr: pl.BlockSpec(arr.shape, lambda b: (0, 0))
    r3 = lambda x: x.reshape(n, s_new, ATT_WIDTH)
    out = pl.pallas_call(
        functools.partial(_attn_sample_kernel, s_new=s_new),
        out_shape=jax.ShapeDtypeStruct((n, s_new, ATT_WIDTH), F32),
        grid=(n,),
        in_specs=[new, new, new, buf, buf] + [full(x) for x in tables] + [full(hm)],
        out_specs=new,
        scratch_shapes=[pltpu.VMEM((LANES, ATT_WIDTH), F32), pltpu.VMEM((LANES, ATT_WIDTH), F32)],
        compiler_params=_cparams(("arbitrary",)),
        name="attn_sample",
    )(r3(q), r3(k), r3(v), k_cache, v_cache, *tables, hm)
    return out.reshape(n * s_new, ATT_WIDTH)


def _out_proj_kernel(ya_ref, yb_ref, yc_ref, x_ref, gate_ref, g_ref, w_ref, o_ref):
    a = CONV_WIDTH
    b = CONV_WIDTH + RWKV_WIDTH
    yc = jnp.concatenate([yc_ref[p] for p in range(N_PAIRS)], axis=1)
    mix = _dot1(ya_ref[...], w_ref[0:a, :]) + _dot1(yb_ref[...], w_ref[a:b, :]) + _dot1(yc, w_ref[b:, :])
    ms = jnp.mean(mix * mix, axis=-1, keepdims=True)
    o_ref[...] = x_ref[...] + gate_ref[...] * (mix * lax.rsqrt(ms + RMS_EPS) * g_ref[...])


def _out_proj(ya, yb, yc, x2, gate, g, w_bf, tm, tiles_per_group):
    m = x2.shape[0]
    rows = lambda w: pl.BlockSpec((tm, w), lambda i: (i, 0))
    return pl.pallas_call(
        _out_proj_kernel,
        out_shape=jax.ShapeDtypeStruct((m, D_MODEL), F32),
        grid=(m // tm,),
        in_specs=[rows(CONV_WIDTH), rows(RWKV_WIDTH),
                  pl.BlockSpec((N_PAIRS, tm, LANES), lambda i: (0, i, 0)), rows(D_MODEL),
                  _mod_spec(gate, tiles_per_group),
                  pl.BlockSpec((1, D_MODEL), lambda i: (0, 0)),
                  pl.BlockSpec((D_MODEL, D_MODEL), lambda i: (0, 0))],
        out_specs=rows(D_MODEL),
        compiler_params=_cparams(("arbitrary",)),
        name="out_proj",
    )(ya, yb, yc, x2, gate, g.reshape(1, D_MODEL), w_bf)


def _ffn_kernel(x_ref, sc_ref, sh_ref, gate_ref, gpre_ref, gpost_ref, wg_ref, wu_ref, wo_ref, o_ref, h_ref, acc_ref):
    f = pl.program_id(1)

    @pl.when(f == 0)
    def _():
        x = x_ref[...]
        ms = jnp.mean(x * x, axis=-1, keepdims=True)
        h = x * lax.rsqrt(ms + RMS_EPS) * gpre_ref[...]
        h_ref[...] = (h * (1.0 + sc_ref[...]) + sh_ref[...]).astype(BF16)
        acc_ref[...] = jnp.zeros_like(acc_ref)

    hb = h_ref[...]
    gg = lax.dot_general(hb, wg_ref[...], NN, preferred_element_type=F32)
    uu = lax.dot_general(hb, wu_ref[...], NN, preferred_element_type=F32)
    act = (_silu(gg) * uu).astype(BF16)
    acc_ref[...] += lax.dot_general(act, wo_ref[...], NN, preferred_element_type=F32)

    @pl.when(f == pl.num_programs(1) - 1)
    def _():
        y = acc_ref[...]
        ms = jnp.mean(y * y, axis=-1, keepdims=True)
        o_ref[...] = x_ref[...] + gate_ref[...] * (y * lax.rsqrt(ms + RMS_EPS) * gpost_ref[...])


def _ffn(x2, scale, shift, gate, g_pre, g_post, wg, wu, wo, tm, tiles_per_group, tf=256):
    m = x2.shape[0]
    mod = lambda arr: pl.BlockSpec((None,) + arr.shape[1:], lambda i, f: (i // tiles_per_group, 0, 0))
    vec = pl.BlockSpec((1, D_MODEL), lambda i, f: (0, 0))
    return pl.pallas_call(
        _ffn_kernel,
        out_shape=jax.ShapeDtypeStruct((m, D_MODEL), F32),
        grid=(m // tm, D_FF // tf),
        in_specs=[pl.BlockSpec((tm, D_MODEL), lambda i, f: (i, 0)),
                  mod(scale), mod(shift), mod(gate), vec, vec,
                  pl.BlockSpec((D_MODEL, tf), lambda i, f: (0, f)),
                  pl.BlockSpec((D_MODEL, tf), lambda i, f: (0, f)),
                  pl.BlockSpec((tf, D_MODEL), lambda i, f: (f, 0))],
        out_specs=pl.BlockSpec((tm, D_MODEL), lambda i, f: (i, 0)),
        scratch_shapes=[pltpu.VMEM((tm, D_MODEL), BF16), pltpu.VMEM((tm, D_MODEL), F32)],
        compiler_params=_cparams(("arbitrary", "arbitrary")),
        name="ffn",
    )(x2, scale, shift, gate, g_pre.reshape(1, D_MODEL), g_post.reshape(1, D_MODEL), wg, wu, wo)


def _heads(x_pairs, n, t):
    return _pairs_to_rows(x_pairs).reshape(n, t, N_ATT_HEADS, HEAD_DIM)


def _layer(x2, n, t, mods, lp, att, state, li):
    shift1, scale1, gate1, shift2, scale2, gate2 = mods
    prompt = state is None
    tm = 512 if prompt else n * t
    tpg = t // tm if prompt else 1
    uc, ur, q, k, v = _in_proj(x2, scale1, shift1, lp['g_pre_mix'], lp['w_in_bf'], tm, tpg)
    conv_params = _conv_params(lp['conv_w'], lp['conv_b'], lp['conv_ln_g'], lp['conv_ln_b'])
    rwkv_params = _rwkv_params(lp)
    ur3 = ur.reshape(n, t, RWKV_COLS)
    new_shift = ur3[:, -1]
    if prompt:
        ya, conv_cache = _conv_prompt(uc, n, t, conv_params)
        shift0 = jnp.zeros((n, 1, RWKV_COLS), F32)
        s0 = jnp.zeros((n, HEAD_DIM, RWKV_WIDTH), F32)
        yb, s_out = _rwkv(ur3, shift0, s0, rwkv_params, tr=512, t_valid=512, nseq=4)
        yc = _attn_prompt(q, k, v, att['prompt'], n, t)
        win = min(MAX_WINDOW, t)
        r4 = lambda x: x.reshape(N_PAIRS, n, t, LANES)[:, :, t - win:].reshape(N_PAIRS, n * win, LANES)
        new_k = _heads(r4(k), n, win)
        new_v = _heads(r4(v), n, win)
    else:
        cache_conv, state_shift, state_wkv, k_cache, v_cache = state
        ya, conv_cache = _conv_sample(uc, cache_conv[li], n, t, conv_params)
        tr = RWKV_CHUNK
        ur_pad = jnp.pad(ur3, ((0, 0), (0, tr - t), (0, 0)))
        yb, s_out = _rwkv(ur_pad, state_shift[li][:, None, :], _state_to_lanes(state_wkv[li]), rwkv_params,
                          tr=tr, t_valid=t, nseq=4)
        yb = yb[:, :t]
        q_r, k_r, v_r = _pairs_to_rows(q), _pairs_to_rows(k), _pairs_to_rows(v)
        yc = _rows_to_pairs(_attn_sample(q_r, k_r, v_r, k_cache, v_cache, li, att['sample'], n, t))
        new_k = k_r.reshape(n, t, N_ATT_HEADS, HEAD_DIM)
        new_v = v_r.reshape(n, t, N_ATT_HEADS, HEAD_DIM)
    yb = yb.reshape(n * t, RWKV_WIDTH)
    x2 = _out_proj(ya, yb, yc, x2, gate1, lp['g_post_mix'], lp['w_out_bf'], tm, tpg)
    tmf = 1024 if prompt else n * t
    x2 = _ffn(x2, scale2, shift2, gate2, lp['g_pre_ffn'], lp['g_post_ffn'], lp['w_ffn_g_bf'], lp['w_ffn_u_bf'],
              lp['w_ffn_o_bf'], tmf, t // tmf if prompt else 1)
    return x2, (conv_cache, new_shift, _lanes_to_state(s_out), new_k, new_v)


def kernel(x_prompt, x_sample, c_prompt, c_sample, cache_conv, state_shift, state_wkv, cache_k_win, cache_v_win,
           w_ada, b_ada, g_pre_mix, g_post_mix, g_pre_ffn, g_post_ffn, w_in, w_out,
           conv_w, conv_b, conv_ln_g, conv_ln_b,
           rwkv_mu, rwkv_w0, rwkv_w2, rwkv_a0, rwkv_a2, rwkv_g2, rwkv_k_k, rwkv_k_a, rwkv_r_k,
           rwkv_ln_g, rwkv_ln_b, rel_bias, w_ffn_in, w_ffn_out):
    depth = w_in.shape[0]
    nb, tp, _ = x_prompt.shape
    ns, ts, _ = x_sample.shape
    rows = nb + ns
    rows_pad = -(-rows // SUBLANES) * SUBLANES
    c_all = jnp.pad(jnp.concatenate([c_prompt, c_sample], axis=0), ((0, rows_pad - rows), (0, 0)))
    mod = _ada_modulation(c_all, w_ada, b_ada)
    mod = mod.reshape(depth, rows_pad, 6, D_MODEL)

    win = cache_k_win.shape[2]
    att = {'prompt': _attn_prompt_bias(rel_bias), 'sample': _attn_sample_tables(rel_bias, ts, win)}
    to_pos_minor = lambda c: jnp.transpose(c, (0, 1, 3, 4, 2)).reshape(depth, ns, ATT_WIDTH, win)
    state = (cache_conv, state_shift, state_wkv, to_pos_minor(cache_k_win), to_pos_minor(cache_v_win))

    yp = x_prompt.reshape(nb * tp, D_MODEL)
    ys = x_sample.reshape(ns * ts, D_MODEL)
    outs_p, outs_s = [], []
    for li in range(depth):
        lp = dict(g_pre_mix=g_pre_mix[li], g_post_mix=g_post_mix[li], g_pre_ffn=g_pre_ffn[li],
                  g_post_ffn=g_post_ffn[li],
                  w_in_bf=w_in[li].astype(BF16), w_out_bf=w_out[li].astype(BF16),
                  w_ffn_g_bf=w_ffn_in[li, :, :D_FF].astype(BF16), w_ffn_u_bf=w_ffn_in[li, :, D_FF:].astype(BF16),
                  w_ffn_o_bf=w_ffn_out[li].astype(BF16),
                  conv_w=conv_w[li], conv_b=conv_b[li], conv_ln_g=conv_ln_g[li], conv_ln_b=conv_ln_b[li],
                  rwkv_mu=rwkv_mu[li], rwkv_w0=rwkv_w0[li], rwkv_w2=rwkv_w2[li], rwkv_a0=rwkv_a0[li],
                  rwkv_a2=rwkv_a2[li], rwkv_g2=rwkv_g2[li], rwkv_k_k=rwkv_k_k[li], rwkv_k_a=rwkv_k_a[li],
                  rwkv_r_k=rwkv_r_k[li], rwkv_ln_g=rwkv_ln_g[li], rwkv_ln_b=rwkv_ln_b[li])
        mods_p = tuple(mod[li, :nb, j][:, None, :] for j in range(6))
        mods_s = tuple(jnp.repeat(mod[li, nb:rows, j], ts, axis=0)[None] for j in range(6))
        yp, st_p = _layer(yp, nb, tp, mods_p, lp, att, None, li)
        ys, st_s = _layer(ys, ns, ts, mods_s, lp, att, state, li)
        outs_p.append(st_p)
        outs_s.append(st_s)
    stack = lambda outs, i: jnp.stack([o[i] for o in outs])
    return (yp.reshape(nb, tp, D_MODEL), ys.reshape(ns, ts, D_MODEL),
            stack(outs_p, 0), stack(outs_s, 0),
            stack(outs_p, 1), stack(outs_s, 1),
            stack(outs_p, 2), stack(outs_s, 2),
            stack(outs_p, 3), stack(outs_s, 3),
            stack(outs_p, 4), stack(outs_s, 4))
```

```python
import functools
import math

import numpy as np
import jax
import jax.numpy as jnp
from jax import lax
from jax.experimental import pallas as pl
from jax.experimental.pallas import tpu as pltpu

F32 = jnp.float32
BF16 = jnp.bfloat16

D_MODEL = 1024
HEAD_DIM = 64
CONV_WIDTH = 256
RWKV_WIDTH = 256
ATT_WIDTH = 512
N_RWKV_HEADS = RWKV_WIDTH // HEAD_DIM
N_ATT_HEADS = ATT_WIDTH // HEAD_DIM
CONV_K = 31
DECAY_LORA = 64
AAA_LORA = 64
GATE_LORA = 128
CONV_COLS = 2 * CONV_WIDTH
RWKV_COLS = 3 * RWKV_WIDTH + DECAY_LORA + AAA_LORA + GATE_LORA
ATT_COLS = 3 * ATT_WIDTH
IN_COLS = CONV_COLS + RWKV_COLS + ATT_COLS
DILATIONS = ((128, 1), (512, 4), (2048, 16))
MAX_WINDOW = 2048
BLOCK = 128
N_REL_BUCKETS = 32
REL_EXACT = N_REL_BUCKETS // 2
REL_MAX_DIST = MAX_WINDOW
D_FF = 2816
RMS_EPS = 1e-6
LN_EPS = 1e-5
RWKV_GN_EPS = HEAD_DIM * 1e-5
DECAY_SCALE = math.exp(-0.5)
NEG_INF = -1e30

VMEM_LIMIT_BYTES = 56 * 1024 * 1024
SUBLANES = 8
LANES = 128
CONV_HALO = 32
RWKV_CHUNK = HEAD_DIM
N_PAIRS = ATT_WIDTH // LANES
ATT_UNIT = BLOCK * max(d for _, d in DILATIONS)
ATT_GROUP = (8, 4, 4)

NN = (((1,), (0,)), ((), ()))
NT = (((1,), (1,)), ((), ()))


def _cparams(sem):
    return pltpu.CompilerParams(dimension_semantics=sem, vmem_limit_bytes=VMEM_LIMIT_BYTES)


def _split_bf16(x, n):
    if x.dtype == BF16:
        return [x]
    pieces = []
    r = x
    for i in range(n):
        p = r.astype(BF16)
        pieces.append(p)
        if i + 1 < n:
            r = r - p.astype(F32)
    return pieces


def _mmp(ap, bp, dims=NN):
    order = max(len(ap), len(bp))
    out = None
    for i, x in enumerate(ap):
        for j, y in enumerate(bp):
            if i + j < order:
                t = lax.dot_general(x, y, dims, preferred_element_type=F32)
                out = t if out is None else out + t
    return out


def _mm(a, b, dims=NN, pa=2, pb=2):
    return _mmp(_split_bf16(a, pa), _split_bf16(b, pb), dims)


def _dot1(a, b, dims=NN):
    return lax.dot_general(a.astype(BF16), b.astype(BF16), dims, preferred_element_type=F32)


def _lockstep(gens):
    results = [None] * len(gens)
    while any(r is None for r in results):
        results = [next(g) if r is None else r for g, r in zip(gens, results)]
    return results


def _sigmoid(x):
    return 1.0 / (1.0 + jnp.exp(-x))


def _silu(x):
    return x * _sigmoid(x)


def _rel_bucket_np(dist):
    d = np.maximum(dist, 1).astype(np.float32)
    large = REL_EXACT + (np.log(d / np.float32(REL_EXACT)) / np.float32(math.log(REL_MAX_DIST / REL_EXACT))
                         * np.float32(N_REL_BUCKETS - REL_EXACT)).astype(np.int32)
    large = np.minimum(large, N_REL_BUCKETS - 1)
    return np.where(dist < REL_EXACT, dist, large)


def _bias_rows(rel_bias, dist):
    idx = _rel_bucket_np(np.asarray(dist, np.int32))
    onehot = jnp.asarray(np.eye(N_REL_BUCKETS, dtype=np.float32)[idx.reshape(-1)])
    rows = jnp.dot(onehot, rel_bias, precision=lax.Precision.HIGHEST)
    return rows.reshape(idx.shape + (rel_bias.shape[1],))


def _ada_kernel(c_ref, w_ref, b_ref, o_ref):
    a = _silu(c_ref[...])
    o_ref[...] = _dot1(a, w_ref[...]) + b_ref[...]


def _ada_modulation(c_all, w_ada, b_ada):
    depth = w_ada.shape[0]
    rows = c_all.shape[0]
    tn = D_MODEL
    return pl.pallas_call(
        _ada_kernel,
        out_shape=jax.ShapeDtypeStruct((depth, rows, 6 * D_MODEL), F32),
        grid=(depth, 6 * D_MODEL // tn),
        in_specs=[
            pl.BlockSpec((rows, D_MODEL), lambda l, j: (0, 0)),
            pl.BlockSpec((None, D_MODEL, tn), lambda l, j: (l, 0, j)),
            pl.BlockSpec((None, 1, tn), lambda l, j: (l, 0, j)),
        ],
        out_specs=pl.BlockSpec((None, rows, tn), lambda l, j: (l, 0, j)),
        compiler_params=_cparams(("arbitrary", "arbitrary")),
        name="ada_modulation",
    )(c_all, w_ada, b_ada.reshape(depth, 1, 6 * D_MODEL))


def _in_proj_kernel(x_ref, sc_ref, sh_ref, g_ref, w_ref, uc_ref, ur_ref, q_ref, k_ref, v_ref):
    x = x_ref[...]
    ms = jnp.mean(x * x, axis=-1, keepdims=True)
    h = x * lax.rsqrt(ms + RMS_EPS) * g_ref[...]
    h = h * (1.0 + sc_ref[...]) + sh_ref[...]
    hb = h.astype(BF16)
    uc_ref[...] = lax.dot_general(hb, w_ref[:, 0:CONV_COLS], NN, preferred_element_type=F32)
    o = CONV_COLS
    ur_ref[...] = lax.dot_general(hb, w_ref[:, o:o + RWKV_COLS], NN, preferred_element_type=F32)
    o += RWKV_COLS
    for ref in (q_ref, k_ref, v_ref):
        res = lax.dot_general(hb, w_ref[:, o:o + ATT_WIDTH], NN, preferred_element_type=F32)
        for p in range(N_PAIRS):
            ref[p] = res[:, p * LANES:(p + 1) * LANES]
        o += ATT_WIDTH


def _mod_spec(mod, tiles_per_group):
    _, r, d = mod.shape
    return pl.BlockSpec((None, r, d), lambda i: (i // tiles_per_group, 0, 0))


def _in_proj(x2, scale, shift, g, w_bf, tm, tiles_per_group):
    m = x2.shape[0]
    flat = lambda w: jax.ShapeDtypeStruct((m, w), F32)
    pairs = jax.ShapeDtypeStruct((N_PAIRS, m, LANES), F32)
    pair_spec = pl.BlockSpec((N_PAIRS, tm, LANES), lambda i: (0, i, 0))
    return pl.pallas_call(
        _in_proj_kernel,
        out_shape=(flat(CONV_COLS), flat(RWKV_COLS), pairs, pairs, pairs),
        grid=(m // tm,),
        in_specs=[
            pl.BlockSpec((tm, D_MODEL), lambda i: (i, 0)),
            _mod_spec(scale, tiles_per_group),
            _mod_spec(shift, tiles_per_group),
            pl.BlockSpec((1, D_MODEL), lambda i: (0, 0)),
            pl.BlockSpec((D_MODEL, IN_COLS), lambda i: (0, 0)),
        ],
        out_specs=(pl.BlockSpec((tm, CONV_COLS), lambda i: (i, 0)), pl.BlockSpec((tm, RWKV_COLS), lambda i: (i, 0)),
                   pair_spec, pair_spec, pair_spec),
        compiler_params=_cparams(("arbitrary",)),
        name="in_proj",
    )(x2, scale, shift, g.reshape(1, D_MODEL), w_bf)


def _pairs_to_rows(x):
    return jnp.transpose(x, (1, 0, 2)).reshape(x.shape[1], ATT_WIDTH)


def _rows_to_pairs(x):
    return jnp.transpose(x.reshape(x.shape[0], N_PAIRS, LANES), (1, 0, 2))


def _conv_core(zs_ref, w_ref, b_ref, lg_ref, lb_ref, y_ref, tt):
    off = CONV_HALO - (CONV_K - 1)
    sub = min(tt, 64)
    for r0 in range(0, tt, sub):
        acc = jnp.zeros((sub, CONV_WIDTH), F32)
        for j in range(CONV_K):
            acc = acc + w_ref[j:j + 1, :] * zs_ref[pl.ds(r0 + off + j, sub), :]
        y = acc + b_ref[...]
        mu = jnp.mean(y, axis=-1, keepdims=True)
        yc = y - mu
        var = jnp.mean(yc * yc, axis=-1, keepdims=True)
        yn = yc * lax.rsqrt(var + LN_EPS) * lg_ref[...] + lb_ref[...]
        y_ref[r0:r0 + sub, :] = _silu(yn)


def _glu(u):
    return u[:, :CONV_WIDTH] * _sigmoid(u[:, CONV_WIDTH:])


def _conv_prompt_kernel(u_ref, uh_ref, w_ref, b_ref, lg_ref, lb_ref, y_ref, zt_ref, zs_ref, *, tt):
    j = pl.program_id(1)
    zh = _glu(uh_ref[...])
    zs_ref[0:CONV_HALO, :] = jnp.where(j > 0, zh, 0.0)
    zs_ref[CONV_HALO:CONV_HALO + tt, :] = _glu(u_ref[...])
    _conv_core(zs_ref, w_ref, b_ref, lg_ref, lb_ref, y_ref, tt)
    zt_ref[...] = zs_ref[tt:tt + CONV_HALO, :]


def _conv_sample_kernel(u_ref, zh_ref, w_ref, b_ref, lg_ref, lb_ref, y_ref, z_ref, zs_ref, *, tt):
    z = _glu(u_ref[...])
    zs_ref[0:CONV_HALO, :] = zh_ref[...]
    zs_ref[CONV_HALO:CONV_HALO + tt, :] = z
    _conv_core(zs_ref, w_ref, b_ref, lg_ref, lb_ref, y_ref, tt)
    z_ref[...] = z


def _conv_param_specs(nargs_grid):
    cmap = (lambda n, j: (0, 0)) if nargs_grid == 2 else (lambda n: (0, 0))
    return [
        pl.BlockSpec((CONV_HALO, CONV_WIDTH), cmap),
        pl.BlockSpec((1, CONV_WIDTH), cmap),
        pl.BlockSpec((1, CONV_WIDTH), cmap),
        pl.BlockSpec((1, CONV_WIDTH), cmap),
    ]


def _conv_params(conv_w, conv_b, ln_g, ln_b):
    w = jnp.pad(conv_w, ((0, CONV_HALO - CONV_K), (0, 0)))
    return w, conv_b.reshape(1, -1), ln_g.reshape(1, -1), ln_b.reshape(1, -1)


def _conv_prompt(uc, n, t, params, tt=512):
    u3 = uc.reshape(n, t, CONV_COLS)
    hb = tt // CONV_HALO
    y, zt = pl.pallas_call(
        functools.partial(_conv_prompt_kernel, tt=tt),
        out_shape=(jax.ShapeDtypeStruct((n, t, CONV_WIDTH), F32),
                   jax.ShapeDtypeStruct((n, CONV_HALO, CONV_WIDTH), F32)),
        grid=(n, t // tt),
        in_specs=[
            pl.BlockSpec((None, tt, CONV_COLS), lambda b, j: (b, j, 0)),
            pl.BlockSpec((None, CONV_HALO, CONV_COLS), lambda b, j: (b, jnp.maximum(j * hb - 1, 0), 0)),
        ] + _conv_param_specs(2),
        out_specs=(pl.BlockSpec((None, tt, CONV_WIDTH), lambda b, j: (b, j, 0)),
                   pl.BlockSpec((None, CONV_HALO, CONV_WIDTH), lambda b, j: (b, 0, 0))),
        scratch_shapes=[pltpu.VMEM((CONV_HALO + tt, CONV_WIDTH), F32)],
        compiler_params=_cparams(("arbitrary", "arbitrary")),
        name="conv_prompt",
    )(u3, u3, *params)
    return y.reshape(n * t, CONV_WIDTH), zt[:, CONV_HALO - (CONV_K - 1):]


def _conv_sample(uc, cache, n, t, params):
    tt = SUBLANES
    u3 = jnp.pad(uc.reshape(n, t, CONV_COLS), ((0, 0), (0, tt - t), (0, 0)))
    zh = jnp.pad(cache, ((0, 0), (CONV_HALO - (CONV_K - 1), 0), (0, 0)))
    y, z = pl.pallas_call(
        functools.partial(_conv_sample_kernel, tt=tt),
        out_shape=(jax.ShapeDtypeStruct((n, tt, CONV_WIDTH), F32),
                   jax.ShapeDtypeStruct((n, tt, CONV_WIDTH), F32)),
        grid=(n,),
        in_specs=[
            pl.BlockSpec((None, tt, CONV_COLS), lambda b: (b, 0, 0)),
            pl.BlockSpec((None, CONV_HALO, CONV_WIDTH), lambda b: (b, 0, 0)),
        ] + _conv_param_specs(1),
        out_specs=(pl.BlockSpec((None, tt, CONV_WIDTH), lambda b: (b, 0, 0)),
                   pl.BlockSpec((None, tt, CONV_WIDTH), lambda b: (b, 0, 0))),
        scratch_shapes=[pltpu.VMEM((CONV_HALO + tt, CONV_WIDTH), F32)],
        compiler_params=_cparams(("arbitrary",)),
        name="conv_sample",
    )(u3, zh, *params)
    new_cache = jnp.concatenate([cache[:, t:], z[:, :t]], axis=1)
    return y[:, :t].reshape(n * t, CONV_WIDTH), new_cache


def _rwkv_consts():
    c = RWKV_CHUNK
    rows = np.arange(c)
    tri = (rows[None, :] <= rows[:, None]).astype(np.float32)
    lanes = np.arange(RWKV_WIDTH)
    bd = (lanes[:, None] // HEAD_DIM == lanes[None, :] // HEAD_DIM).astype(np.float32)
    return jnp.asarray(tri, BF16), jnp.asarray(bd, BF16)


def _rwkv_kernel(u_ref, up_ref, sh0_ref, s0_ref, mu_ref, w0_ref, wwa_ref, a0_ref, g2_ref, kk_ref, ka_ref, rk_ref,
                 lg_ref, lb_ref, tri_ref, bd_ref,
                 y_ref, so_ref,
                 us_ref, lw_ref, kn_ref, kb_ref, k2_ref, rr_ref, vv_ref, bon_ref, gate_ref, yy_ref, s_ref,
                 *, tr, t_valid, nseq):
    j = pl.program_id(1)
    c = RWKV_CHUNK
    bd = bd_ref[...]
    o = 3 * RWKV_WIDTH

    @pl.when(j == 0)
    def _():
        s_ref[...] = s0_ref[...]

    for sq in range(nseq):
        u = u_ref[sq]
        prev = jnp.where(j > 0, up_ref[sq], jnp.broadcast_to(sh0_ref[sq], (SUBLANES, RWKV_COLS)))
        us_ref[sq, 0:SUBLANES, :] = prev
        us_ref[sq, SUBLANES:SUBLANES + tr, :] = u
        u_prev = us_ref[sq, pl.ds(SUBLANES - 1, tr), :]
        xs = u + (u_prev - u) * mu_ref[...]
        r = xs[:, :RWKV_WIDTH]
        k = xs[:, RWKV_WIDTH:2 * RWKV_WIDTH]
        v = xs[:, 2 * RWKV_WIDTH:o]
        lo = xs[:, o:o + DECAY_LORA + AAA_LORA]
        lane = lax.broadcasted_iota(jnp.int32, lo.shape, 1)
        lo = jnp.where(lane < DECAY_LORA, jnp.tanh(lo), lo)
        wa = _dot1(lo, wwa_ref[...])
        g = _dot1(_sigmoid(xs[:, o + DECAY_LORA + AAA_LORA:]), g2_ref[...])
        lw = -DECAY_SCALE * _sigmoid(w0_ref[...] + wa[:, :RWKV_WIDTH])
        a = _sigmoid(a0_ref[...] + wa[:, RWKV_WIDTH:])
        kkr = k * kk_ref[...]
        ss = _mm(kkr * kkr, bd, pa=2)
        kk = kkr / jnp.maximum(jnp.sqrt(ss), 1e-12)
        k2 = k * (1.0 + (a - 1.0) * ka_ref[...])
        if t_valid < tr:
            valid = lax.broadcasted_iota(jnp.int32, (tr, RWKV_WIDTH), 0) < t_valid
            zero = jnp.zeros_like(r)
            r, k2, v, kk, lw = (jnp.where(valid, z, zero) for z in (r, k2, v, kk, lw))
        lw_ref[sq] = lw
        kn_ref[sq] = -kk
        kb_ref[sq] = kk * a
        k2_ref[sq] = k2
        rr_ref[sq] = r
        vv_ref[sq] = v
        bon_ref[sq] = _mm(r * k2 * rk_ref[...], bd, pa=2) * v
        gate_ref[sq] = g

    row = lax.broadcasted_iota(jnp.int32, (c, LANES), 0)
    col = lax.broadcasted_iota(jnp.int32, (c, LANES), 1) & (HEAD_DIM - 1)
    strict = col < row
    incl = col <= row
    ident = jnp.where(col == row, 1.0, 0.0)
    bd2 = bd[0:LANES, 0:LANES]
    bd2_mask = bd2 > 0.5
    n_steps = int(round(math.log2(c)))

    def dot(a, b, dims=NN):
        return lax.dot_general(a.astype(BF16), b, dims, preferred_element_type=F32)

    def stack(x):
        xb = x.astype(BF16)
        return jnp.concatenate([xb, xb], axis=0) * bd2

    def rows2(a, b):
        return jnp.concatenate([a, b], axis=0)

    def chunk_pair(pt, bt, kt, rt, vc, gc, s0):
        zero = jnp.zeros((c, LANES), F32)
        pr = rows2(pt, rt)
        xb = dot(pr, stack(bt), NT)
        xk = dot(pr, stack(kt), NT)
        yield None
        lm = jnp.where(strict, xb[:c], zero)
        mm_ = jnp.where(strict, xk[:c], zero)
        qb = jnp.where(incl, xb[c:], zero)
        qk = jnp.where(incl, xk[c:], zero)
        tinv = ident + lm
        lk = dot(lm, stack(lm))
        mq = dot(rows2(mm_, qk), stack(vc))
        yield None
        for step in range(1, n_steps):
            if step + 1 < n_steps:
                both = dot(rows2(lk, tinv), stack(lk))
                lk = both[:c]
                tinv = tinv + both[c:]
            else:
                tinv = tinv + dot(tinv, stack(lk))
            yield None
        p2 = dot(tinv, stack(pt))
        u0 = dot(tinv, stack(mq[:c]))
        yield None
        xs_ = dot(rows2(p2, rt), stack(s0), NT)
        yield None
        uu = xs_[:c] + u0
        y = xs_[c:] + dot(qb, stack(uu)) + mq[c:]
        uvt = rows2(uu, vc).T
        bk = rows2(bt * gc, kt * gc).astype(BF16)
        z = jnp.where(bd2_mask, dot(uvt, bk), 0.0)
        yield y, s0 * gc + z[0:c] + z[c:2 * c]

    def chunk(ci, carry):
        sl = pl.ds(pl.multiple_of(ci * c, c), c)
        work = []
        for sq in range(nseq):
            lw_c = lw_ref[sq, sl, :]
            gi = _mmp([tri_ref[...]], _split_bf16(lw_c, 3))
            e_neg = jnp.exp(-gi)
            gc = jnp.exp(gi[c - 1:c, :])
            pt = kn_ref[sq, sl, :] * jnp.exp(gi - lw_c)
            bt = kb_ref[sq, sl, :] * e_neg
            kt = k2_ref[sq, sl, :] * e_neg
            rt = rr_ref[sq, sl, :] * jnp.exp(gi)
            vc = vv_ref[sq, sl, :]
            s0 = s_ref[sq]
            for p in range(RWKV_WIDTH // LANES):
                ln = slice(p * LANES, (p + 1) * LANES)
                work.append((sq, ln, tuple(x[:, ln] for x in (pt, bt, kt, rt, vc, gc, s0))))
        results = _lockstep([chunk_pair(*args) for _, _, args in work])
        done = [(sq, ln, res) for (sq, ln, _), res in zip(work, results)]
        for sq, ln, (y, s_new) in done:
            yy_ref[sq, sl, ln] = y
            s_ref[sq, :, ln] = s_new
        return carry

    lax.fori_loop(0, tr // c, chunk, 0)

    inv = 1.0 / HEAD_DIM
    for sq in range(nseq):
        y = yy_ref[sq]
        mu = _mm(y, bd, pa=2) * inv
        yc = y - mu
        var = _mm(yc * yc, bd, pa=2) * inv
        yn = yc * lax.rsqrt(var + RWKV_GN_EPS) * lg_ref[...] + lb_ref[...]
        y_ref[sq] = (yn + bon_ref[sq]) * gate_ref[sq]
    so_ref[...] = s_ref[...]


def _rwkv_params(p):
    z = jnp.zeros((DECAY_LORA, RWKV_WIDTH), F32)
    wwa = jnp.concatenate([jnp.concatenate([p['rwkv_w2'], z], axis=1),
                           jnp.concatenate([z, p['rwkv_a2']], axis=1)], axis=0)
    row = lambda x: x.reshape(1, -1)
    return (row(p['rwkv_mu']), row(p['rwkv_w0']), wwa, row(p['rwkv_a0']), p['rwkv_g2'], row(p['rwkv_k_k']),
            row(p['rwkv_k_a']), row(p['rwkv_r_k']), row(p['rwkv_ln_g']), row(p['rwkv_ln_b']))


def _rwkv(u3, shift0, s0, params, tr, t_valid, nseq):
    n, t, _ = u3.shape
    assert n % nseq == 0
    consts = _rwkv_consts()
    hb = tr // SUBLANES
    const2 = lambda b, j: (0, 0)
    full = lambda arr: pl.BlockSpec(arr.shape, const2)
    tile = lambda: pltpu.VMEM((nseq, tr, RWKV_WIDTH), F32)
    y, s_out = pl.pallas_call(
        functools.partial(_rwkv_kernel, tr=tr, t_valid=t_valid, nseq=nseq),
        out_shape=(jax.ShapeDtypeStruct((n, t, RWKV_WIDTH), F32),
                   jax.ShapeDtypeStruct((n, HEAD_DIM, RWKV_WIDTH), F32)),
        grid=(n // nseq, t // tr),
        in_specs=[
            pl.BlockSpec((nseq, tr, RWKV_COLS), lambda b, j: (b, j, 0)),
            pl.BlockSpec((nseq, SUBLANES, RWKV_COLS), lambda b, j: (b, jnp.maximum(j * hb - 1, 0), 0)),
            pl.BlockSpec((nseq, 1, RWKV_COLS), lambda b, j: (b, 0, 0)),
            pl.BlockSpec((nseq, HEAD_DIM, RWKV_WIDTH), lambda b, j: (b, 0, 0)),
        ] + [full(x) for x in params] + [full(x) for x in consts],
        out_specs=(pl.BlockSpec((nseq, tr, RWKV_WIDTH), lambda b, j: (b, j, 0)),
                   pl.BlockSpec((nseq, HEAD_DIM, RWKV_WIDTH), lambda b, j: (b, 0, 0))),
        scratch_shapes=[pltpu.VMEM((nseq, SUBLANES + tr, RWKV_COLS), F32)] + [tile() for _ in range(9)]
                       + [pltpu.VMEM((nseq, HEAD_DIM, RWKV_WIDTH), F32)],
        compiler_params=_cparams(("arbitrary", "arbitrary")),
        name="rwkv7",
    )(u3, u3, shift0, s0, *params, *consts)
    return y, s_out


def _state_to_lanes(s):
    n = s.shape[0]
    return jnp.transpose(s, (0, 2, 1, 3)).reshape(n, HEAD_DIM, RWKV_WIDTH)


def _lanes_to_state(s):
    n = s.shape[0]
    return jnp.transpose(s.reshape(n, HEAD_DIM, N_RWKV_HEADS, HEAD_DIM), (0, 2, 1, 3))


def _attn_prompt_kernel(q_ref, kc_ref, kp_ref, vc_ref, vp_ref, w_ref, o_ref,
                        kcat, vcat, acc, m0, m1, l0, l1, bias_s):
    u = pl.program_id(2)
    unit = ATT_UNIT
    scale = HEAD_DIM ** -0.5
    kcat[0:unit, :] = kp_ref[...]
    kcat[unit:2 * unit, :] = kc_ref[...]
    vcat[0:unit, :] = vp_ref[...]
    vcat[unit:2 * unit, :] = vc_ref[...]
    for bi in range(len(DILATIONS)):
        for half in range(2):
            wrow = jnp.broadcast_to(w_ref[bi, half:half + 1, :], (BLOCK, 2 * BLOCK))
            bias_s[bi, half * BLOCK:(half + 1) * BLOCK, :] = pltpu.roll(wrow, 0, 1, stride=1, stride_axis=0)
    row = lax.broadcasted_iota(jnp.int32, (2 * BLOCK, 2 * BLOCK), 0) & (BLOCK - 1)
    col = lax.broadcasted_iota(jnp.int32, (2 * BLOCK, 2 * BLOCK), 1)
    delta = row + BLOCK - col
    band = (delta >= 0) & (delta <= BLOCK)
    lo = lax.broadcasted_iota(jnp.int32, (BLOCK, LANES), 1) < HEAD_DIM
    n_br = len(DILATIONS)
    rows2 = lambda a, b: jnp.concatenate([a, b], axis=0)
    for bi, (window, dil) in enumerate(DILATIONS):
        first = bi == 0
        last = bi == n_br - 1
        shift = int(round(math.log2(dil)))

        def sub_block(j, bi, dil, shift, first, last):
            r = j & (dil - 1)
            blk = j >> shift
            start = blk * (BLOCK * dil) + r
            rows_q = pl.ds(start, BLOCK, stride=dil)
            rows_k = pl.ds(unit + start - BLOCK * dil, 2 * BLOCK, stride=dil)
            q = q_ref[rows_q, :] * scale
            kk = kcat[rows_k, :].astype(BF16)
            vv = vcat[rows_k, :].astype(BF16)
            valid = band & ((col >= BLOCK) | (u > 0) | (blk > 0))
            q2 = rows2(jnp.where(lo, q, 0.0), jnp.where(lo, 0.0, q)).astype(BF16)
            s = lax.dot_general(q2, kk, NT, preferred_element_type=F32)
            if not first:
                m_old = rows2(m0[rows_q, :], m1[rows_q, :])
                l_old = rows2(l0[rows_q, :], l1[rows_q, :])
                acc_old = acc[rows_q, :]
            yield None
            s = jnp.where(valid, s + bias_s[bi], NEG_INF)
            m_cur = jnp.max(s, axis=-1, keepdims=True)
            if first:
                m_new = jnp.broadcast_to(m_cur, (2 * BLOCK, LANES))
            else:
                m_new = jnp.maximum(m_old, m_cur)
                alpha = jnp.exp(m_old - m_new)
            p = jnp.exp(s - jnp.concatenate([m_new, m_new], axis=1))
            l_new = jnp.sum(p, axis=-1, keepdims=True)
            if first:
                l_new = jnp.broadcast_to(l_new, (2 * BLOCK, LANES))
            else:
                l_new = alpha * l_old + l_new
            o2 = lax.dot_general(p.astype(BF16), vv, NN, preferred_element_type=F32)
            yield None
            o_pair = jnp.where(lo, o2[:BLOCK], o2[BLOCK:])
            if not first:
                o_pair = o_pair + jnp.where(lo, alpha[:BLOCK], alpha[BLOCK:]) * acc_old
            if last:
                linv = 1.0 / l_new
                yield [(o_ref, rows_q, o_pair * jnp.where(lo, linv[:BLOCK], linv[BLOCK:]))]
            else:
                yield [(m0, rows_q, m_new[:BLOCK]), (m1, rows_q, m_new[BLOCK:]),
                       (l0, rows_q, l_new[:BLOCK]), (l1, rows_q, l_new[BLOCK:]), (acc, rows_q, o_pair)]

        group = ATT_GROUP[bi]

        def body(jg, carry, bi=bi, dil=dil, shift=shift, first=first, last=last, group=group):
            groups = _lockstep([sub_block(jg * group + g, bi, dil, shift, first, last) for g in range(group)])
            for stores in groups:
                for ref, rows, val in stores:
                    ref[rows, :] = val
            return carry

        lax.fori_loop(0, unit // BLOCK // group, body, 0)


def _attn_prompt_bias(rel_bias):
    c = np.arange(2 * BLOCK)
    tabs = []
    for _, dil in DILATIONS:
        tabs.append(_bias_rows(rel_bias, np.maximum(BLOCK - c, 0) * dil).T)
    w = jnp.stack(tabs)
    return jnp.transpose(w.reshape(len(DILATIONS), N_PAIRS, 2, 2 * BLOCK), (1, 0, 2, 3))


def _attn_prompt(q, k, v, wtab, n, t):
    unit = ATT_UNIT
    assert t % unit == 0
    r4 = lambda x: x.reshape(N_PAIRS, n, t, LANES)
    cur = pl.BlockSpec((None, None, unit, LANES), lambda a, p, u: (p, a, u, 0))
    prv = pl.BlockSpec((None, None, unit, LANES), lambda a, p, u: (p, a, jnp.maximum(u - 1, 0), 0))
    stat = lambda: pltpu.VMEM((unit, LANES), F32)
    out = pl.pallas_call(
        _attn_prompt_kernel,
        out_shape=jax.ShapeDtypeStruct((N_PAIRS, n, t, LANES), F32),
        grid=(n, N_PAIRS, t // unit),
        in_specs=[cur, cur, prv, cur, prv,
                  pl.BlockSpec((None, len(DILATIONS), 2, 2 * BLOCK), lambda a, p, u: (p, 0, 0, 0))],
        out_specs=cur,
        scratch_shapes=[pltpu.VMEM((2 * unit, LANES), F32), pltpu.VMEM((2 * unit, LANES), F32),
                        stat(), stat(), stat(), stat(), stat(),
                        pltpu.VMEM((len(DILATIONS), 2 * BLOCK, 2 * BLOCK), F32)],
        compiler_params=_cparams(("arbitrary", "arbitrary", "arbitrary")),
        name="attn_prompt",
    )(r4(q), r4(k), r4(k), r4(v), r4(v), wtab)
    return out.reshape(N_PAIRS, n * t, LANES)


def _attn_sample_kernel(q_ref, kn_ref, vn_ref, kc_ref, vc_ref, bt_ref, bn_ref, cnt_ref, cn_ref, hm_ref, o_ref,
                        kn_s, vn_s, *, s_new):
    scale = HEAD_DIM ** -0.5
    hm = hm_ref[...]
    qm = jnp.concatenate([(q_ref[s:s + 1, :] * scale) * hm for s in range(s_new)], axis=0).astype(BF16)
    kn_s[...] = jnp.zeros_like(kn_s)
    vn_s[...] = jnp.zeros_like(vn_s)
    kn_s[0:s_new, :] = kn_ref[...]
    vn_s[0:s_new, :] = vn_ref[...]
    sc = lax.dot_general(qm, kc_ref[...].astype(BF16), NN, preferred_element_type=F32) + bt_ref[...]
    sn = lax.dot_general(qm, kn_s[...].astype(BF16), NT, preferred_element_type=F32) + bn_ref[...]
    cnt = cnt_ref[...]
    cn = cn_ref[...]
    sc = jnp.where(cnt > 0.0, sc, NEG_INF)
    sn = jnp.where(cn > 0.0, sn, NEG_INF)
    m = jnp.maximum(jnp.max(sc, axis=-1, keepdims=True), jnp.max(sn, axis=-1, keepdims=True))
    p = cnt * jnp.exp(sc - m)
    pn = cn * jnp.exp(sn - m)
    l = jnp.sum(p, axis=-1, keepdims=True) + jnp.sum(pn, axis=-1, keepdims=True)
    o = (lax.dot_general(p.astype(BF16), vc_ref[...].astype(BF16), NT, preferred_element_type=F32)
         + lax.dot_general(pn.astype(BF16), vn_s[...].astype(BF16), NN, preferred_element_type=F32))
    o = o / l
    for s in range(s_new):
        o_ref[s:s + 1, :] = jnp.sum(o[s * N_ATT_HEADS:(s + 1) * N_ATT_HEADS] * hm, axis=0, keepdims=True)


def _attn_sample_tables(rel_bias, s_new, win):
    s = np.arange(s_new)[:, None]
    dist_c = win + s - np.arange(win)[None, :]
    dist_n = s - np.arange(LANES)[None, :]
    cnt_c = np.zeros(dist_c.shape, np.float32)
    cnt_n = np.zeros(dist_n.shape, np.float32)
    for window, dil in DILATIONS:
        cnt_c += ((dist_c % dil == 0) & (dist_c <= window)).astype(np.float32)
        cnt_n += ((dist_n >= 0) & (dist_n % dil == 0) & (dist_n <= window)).astype(np.float32)
    rep = lambda x: jnp.asarray(np.repeat(x, N_ATT_HEADS, axis=0))
    rows = lambda b: jnp.transpose(b, (0, 2, 1)).reshape(s_new * N_ATT_HEADS, -1)
    return (rows(_bias_rows(rel_bias, dist_c)), rows(_bias_rows(rel_bias, np.maximum(dist_n, 0))),
            rep(cnt_c), rep(cnt_n))


def _attn_sample(q, k, v, k_cache, v_cache, li, tables, n, s_new):
    win = k_cache.shape[-1]
    assert win == MAX_WINDOW
    lanes = np.arange(ATT_WIDTH)
    hm = jnp.asarray((np.arange(N_ATT_HEADS)[:, None] == lanes[None, :] // HEAD_DIM).astype(np.float32))
    new = pl.BlockSpec((None, s_new, ATT_WIDTH), lambda b: (b, 0, 0))
    buf = pl.BlockSpec((None, None, ATT_WIDTH, win), lambda b: (li, b, 0, 0))
    full = lambda arr: pl.BlockSpec(arr.shape, lambda b: (0, 0))
    r3 = lambda x: x.reshape(n, s_new, ATT_WIDTH)
    out = pl.pallas_call(
        functools.partial(_attn_sample_kernel, s_new=s_new),
        out_shape=jax.ShapeDtypeStruct((n, s_new, ATT_WIDTH), F32),
        grid=(n,),
        in_specs=[new, new, new, buf, buf] + [full(x) for x in tables] + [full(hm)],
        out_specs=new,
        scratch_shapes=[pltpu.VMEM((LANES, ATT_WIDTH), F32), pltpu.VMEM((LANES, ATT_WIDTH), F32)],
        compiler_params=_cparams(("arbitrary",)),
        name="attn_sample",
    )(r3(q), r3(k), r3(v), k_cache, v_cache, *tables, hm)
    return out.reshape(n * s_new, ATT_WIDTH)


def _out_proj_kernel(ya_ref, yb_ref, yc_ref, x_ref, gate_ref, g_ref, w_ref, o_ref):
    a = CONV_WIDTH
    b = CONV_WIDTH + RWKV_WIDTH
    yc = jnp.concatenate([yc_ref[p] for p in range(N_PAIRS)], axis=1)
    mix = _dot1(ya_ref[...], w_ref[0:a, :]) + _dot1(yb_ref[...], w_ref[a:b, :]) + _dot1(yc, w_ref[b:, :])
    ms = jnp.mean(mix * mix, axis=-1, keepdims=True)
    o_ref[...] = x_ref[...] + gate_ref[...] * (mix * lax.rsqrt(ms + RMS_EPS) * g_ref[...])


def _out_proj(ya, yb, yc, x2, gate, g, w_bf, tm, tiles_per_group):
    m = x2.shape[0]
    rows = lambda w: pl.BlockSpec((tm, w), lambda i: (i, 0))
    return pl.pallas_call(
        _out_proj_kernel,
        out_shape=jax.ShapeDtypeStruct((m, D_MODEL), F32),
        grid=(m // tm,),
        in_specs=[rows(CONV_WIDTH), rows(RWKV_WIDTH),
                  pl.BlockSpec((N_PAIRS, tm, LANES), lambda i: (0, i, 0)), rows(D_MODEL),
                  _mod_spec(gate, tiles_per_group),
                  pl.BlockSpec((1, D_MODEL), lambda i: (0, 0)),
                  pl.BlockSpec((D_MODEL, D_MODEL), lambda i: (0, 0))],
        out_specs=rows(D_MODEL),
        compiler_params=_cparams(("arbitrary",)),
        name="out_proj",
    )(ya, yb, yc, x2, gate, g.reshape(1, D_MODEL), w_bf)


def _ffn_kernel(x_ref, sc_ref, sh_ref, gate_ref, gpre_ref, gpost_ref, wg_ref, wu_ref, wo_ref, o_ref, h_ref, acc_ref):
    f = pl.program_id(1)

    @pl.when(f == 0)
    def _():
        x = x_ref[...]
        ms = jnp.mean(x * x, axis=-1, keepdims=True)
        h = x * lax.rsqrt(ms + RMS_EPS) * gpre_ref[...]
        h_ref[...] = (h * (1.0 + sc_ref[...]) + sh_ref[...]).astype(BF16)
        acc_ref[...] = jnp.zeros_like(acc_ref)

    hb = h_ref[...]
    gg = lax.dot_general(hb, wg_ref[...], NN, preferred_element_type=F32)
    uu = lax.dot_general(hb, wu_ref[...], NN, preferred_element_type=F32)
    act = (_silu(gg) * uu).astype(BF16)
    acc_ref[...] += lax.dot_general(act, wo_ref[...], NN, preferred_element_type=F32)

    @pl.when(f == pl.num_programs(1) - 1)
    def _():
        y = acc_ref[...]
        ms = jnp.mean(y * y, axis=-1, keepdims=True)
        o_ref[...] = x_ref[...] + gate_ref[...] * (y * lax.rsqrt(ms + RMS_EPS) * gpost_ref[...])


def _ffn(x2, scale, shift, gate, g_pre, g_post, wg, wu, wo, tm, tiles_per_group, tf=256):
    m = x2.shape[0]
    mod = lambda arr: pl.BlockSpec((None,) + arr.shape[1:], lambda i, f: (i // tiles_per_group, 0, 0))
    vec = pl.BlockSpec((1, D_MODEL), lambda i, f: (0, 0))
    return pl.pallas_call(
        _ffn_kernel,
        out_shape=jax.ShapeDtypeStruct((m, D_MODEL), F32),
        grid=(m // tm, D_FF // tf),
        in_specs=[pl.BlockSpec((tm, D_MODEL), lambda i, f: (i, 0)),
                  mod(scale), mod(shift), mod(gate), vec, vec,
                  pl.BlockSpec((D_MODEL, tf), lambda i, f: (0, f)),
                  pl.BlockSpec((D_MODEL, tf), lambda i, f: (0, f)),
                  pl.BlockSpec((tf, D_MODEL), lambda i, f: (f, 0))],
        out_specs=pl.BlockSpec((tm, D_MODEL), lambda i, f: (i, 0)),
        scratch_shapes=[pltpu.VMEM((tm, D_MODEL), BF16), pltpu.VMEM((tm, D_MODEL), F32)],
        compiler_params=_cparams(("arbitrary", "arbitrary")),
        name="ffn",
    )(x2, scale, shift, gate, g_pre.reshape(1, D_MODEL), g_post.reshape(1, D_MODEL), wg, wu, wo)


def _heads(x_pairs, n, t):
    return _pairs_to_rows(x_pairs).reshape(n, t, N_ATT_HEADS, HEAD_DIM)


def _layer(x2, n, t, mods, lp, att, state, li):
    shift1, scale1, gate1, shift2, scale2, gate2 = mods
    prompt = state is None
    tm = 512 if prompt else n * t
    tpg = t // tm if prompt else 1
    uc, ur, q, k, v = _in_proj(x2, scale1, shift1, lp['g_pre_mix'], lp['w_in_bf'], tm, tpg)
    conv_params = _conv_params(lp['conv_w'], lp['conv_b'], lp['conv_ln_g'], lp['conv_ln_b'])
    rwkv_params = _rwkv_params(lp)
    ur3 = ur.reshape(n, t, RWKV_COLS)
    new_shift = ur3[:, -1]
    if prompt:
        ya, conv_cache = _conv_prompt(uc, n, t, conv_params)
        shift0 = jnp.zeros((n, 1, RWKV_COLS), F32)
        s0 = jnp.zeros((n, HEAD_DIM, RWKV_WIDTH), F32)
        yb, s_out = _rwkv(ur3, shift0, s0, rwkv_params, tr=512, t_valid=512, nseq=4)
        yc = _attn_prompt(q, k, v, att['prompt'], n, t)
        win = min(MAX_WINDOW, t)
        r4 = lambda x: x.reshape(N_PAIRS, n, t, LANES)[:, :, t - win:].reshape(N_PAIRS, n * win, LANES)
        new_k = _heads(r4(k), n, win)
        new_v = _heads(r4(v), n, win)
    else:
        cache_conv, state_shift, state_wkv, k_cache, v_cache = state
        ya, conv_cache = _conv_sample(uc, cache_conv[li], n, t, conv_params)
        tr = RWKV_CHUNK
        ur_pad = jnp.pad(ur3, ((0, 0), (0, tr - t), (0, 0)))
        yb, s_out = _rwkv(ur_pad, state_shift[li][:, None, :], _state_to_lanes(state_wkv[li]), rwkv_params,
                          tr=tr, t_valid=t, nseq=4)
        yb = yb[:, :t]
        q_r, k_r, v_r = _pairs_to_rows(q), _pairs_to_rows(k), _pairs_to_rows(v)
        yc = _rows_to_pairs(_attn_sample(q_r, k_r, v_r, k_cache, v_cache, li, att['sample'], n, t))
        new_k = k_r.reshape(n, t, N_ATT_HEADS, HEAD_DIM)
        new_v = v_r.reshape(n, t, N_ATT_HEADS, HEAD_DIM)
    yb = yb.reshape(n * t, RWKV_WIDTH)
    x2 = _out_proj(ya, yb, yc, x2, gate1, lp['g_post_mix'], lp['w_out_bf'], tm, tpg)
    tmf = 1024 if prompt else n * t
    x2 = _ffn(x2, scale2, shift2, gate2, lp['g_pre_ffn'], lp['g_post_ffn'], lp['w_ffn_g_bf'], lp['w_ffn_u_bf'],
              lp['w_ffn_o_bf'], tmf, t // tmf if prompt else 1)
    return x2, (conv_cache, new_shift, _lanes_to_state(s_out), new_k, new_v)


def kernel(x_prompt, x_sample, c_prompt, c_sample, cache_conv, state_shift, state_wkv, cache_k_win, cache_v_win,
           w_ada, b_ada, g_pre_mix, g_post_mix, g_pre_ffn, g_post_ffn, w_in, w_out,
           conv_w, conv_b, conv_ln_g, conv_ln_b,
           rwkv_mu, rwkv_w0, rwkv_w2, rwkv_a0, rwkv_a2, rwkv_g2, rwkv_k_k, rwkv_k_a, rwkv_r_k,
           rwkv_ln_g, rwkv_ln_b, rel_bias, w_ffn_in, w_ffn_out):
    depth = w_in.shape[0]
    nb, tp, _ = x_prompt.shape
    ns, ts, _ = x_sample.shape
    rows = nb + ns
    rows_pad = -(-rows // SUBLANES) * SUBLANES
    c_all = jnp.pad(jnp.concatenate([c_prompt, c_sample], axis=0), ((0, rows_pad - rows), (0, 0)))
    mod = _ada_modulation(c_all, w_ada, b_ada)
    mod = mod.reshape(depth, rows_pad, 6, D_MODEL)

    win = cache_k_win.shape[2]
    att = {'prompt': _attn_prompt_bias(rel_bias), 'sample': _attn_sample_tables(rel_bias, ts, win)}
    to_pos_minor = lambda c: jnp.transpose(c, (0, 1, 3, 4, 2)).reshape(depth, ns, ATT_WIDTH, win)
    state = (cache_conv, state_shift, state_wkv, to_pos_minor(cache_k_win), to_pos_minor(cache_v_win))

    yp = x_prompt.reshape(nb * tp, D_MODEL)
    ys = x_sample.reshape(ns * ts, D_MODEL)
    outs_p, outs_s = [], []
    for li in range(depth):
        lp = dict(g_pre_mix=g_pre_mix[li], g_post_mix=g_post_mix[li], g_pre_ffn=g_pre_ffn[li],
                  g_post_ffn=g_post_ffn[li],
                  w_in_bf=w_in[li].astype(BF16), w_out_bf=w_out[li].astype(BF16),
                  w_ffn_g_bf=w_ffn_in[li, :, :D_FF].astype(BF16), w_ffn_u_bf=w_ffn_in[li, :, D_FF:].astype(BF16),
                  w_ffn_o_bf=w_ffn_out[li].astype(BF16),
                  conv_w=conv_w[li], conv_b=conv_b[li], conv_ln_g=conv_ln_g[li], conv_ln_b=conv_ln_b[li],
                  rwkv_mu=rwkv_mu[li], rwkv_w0=rwkv_w0[li], rwkv_w2=rwkv_w2[li], rwkv_a0=rwkv_a0[li],
                  rwkv_a2=rwkv_a2[li], rwkv_g2=rwkv_g2[li], rwkv_k_k=rwkv_k_k[li], rwkv_k_a=rwkv_k_a[li],
                  rwkv_r_k=rwkv_r_k[li], rwkv_ln_g=rwkv_ln_g[li], rwkv_ln_b=rwkv_ln_b[li])
        mods_p = tuple(mod[li, :nb, j][:, None, :] for j in range(6))
        mods_s = tuple(jnp.repeat(mod[li, nb:rows, j], ts, axis=0)[None] for j in range(6))
        yp, st_p = _layer(yp, nb, tp, mods_p, lp, att, None, li)
        ys, st_s = _layer(ys, ns, ts, mods_s, lp, att, state, li)
        outs_p.append(st_p)
        outs_s.append(st_s)
    stack = lambda outs, i: jnp.stack([o[i] for o in outs])
    return (yp.reshape(nb, tp, D_MODEL), ys.reshape(ns, ts, D_MODEL),
            stack(outs_p, 0), stack(outs_s, 0),
            stack(outs_p, 1), stack(outs_s, 1),
            stack(outs_p, 2), stack(outs_s, 2),
            stack(outs_p, 3), stack(outs_s, 3),
            stack(outs_p, 4), stack(outs_s, 4))
```

```python
import functools
import math

import numpy as np
import jax
import jax.numpy as jnp
from jax import lax
from jax.experimental import pallas as pl
from jax.experimental.pallas import tpu as pltpu

F32 = jnp.float32
BF16 = jnp.bfloat16

D_MODEL = 1024
HEAD_DIM = 64
CONV_WIDTH = 256
RWKV_WIDTH = 256
ATT_WIDTH = 512
N_RWKV_HEADS = RWKV_WIDTH // HEAD_DIM
N_ATT_HEADS = ATT_WIDTH // HEAD_DIM
CONV_K = 31
DECAY_LORA = 64
AAA_LORA = 64
GATE_LORA = 128
CONV_COLS = 2 * CONV_WIDTH
RWKV_COLS = 3 * RWKV_WIDTH + DECAY_LORA + AAA_LORA + GATE_LORA
ATT_COLS = 3 * ATT_WIDTH
IN_COLS = CONV_COLS + RWKV_COLS + ATT_COLS
DILATIONS = ((128, 1), (512, 4), (2048, 16))
MAX_WINDOW = 2048
BLOCK = 128
N_REL_BUCKETS = 32
REL_EXACT = N_REL_BUCKETS // 2
REL_MAX_DIST = MAX_WINDOW
D_FF = 2816
RMS_EPS = 1e-6
LN_EPS = 1e-5
RWKV_GN_EPS = HEAD_DIM * 1e-5
DECAY_SCALE = math.exp(-0.5)
NEG_INF = -1e30

VMEM_LIMIT_BYTES = 56 * 1024 * 1024
SUBLANES = 8
LANES = 128
CONV_HALO = 32
FFN_SPLIT = 2
RWKV_CHUNK = HEAD_DIM
N_PAIRS = ATT_WIDTH // LANES
ATT_UNIT = BLOCK * max(d for _, d in DILATIONS)
ATT_GROUP = (8, 4, 4)

NN = (((1,), (0,)), ((), ()))
NT = (((1,), (1,)), ((), ()))


def _cparams(sem):
    return pltpu.CompilerParams(dimension_semantics=sem, vmem_limit_bytes=VMEM_LIMIT_BYTES)


def _split_bf16(x, n):
    if x.dtype == BF16:
        return [x]
    pieces = []
    r = x
    for i in range(n):
        p = r.astype(BF16)
        pieces.append(p)
        if i + 1 < n:
            r = r - p.astype(F32)
    return pieces


def _mmp(ap, bp, dims=NN):
    order = max(len(ap), len(bp))
    out = None
    for i, x in enumerate(ap):
        for j, y in enumerate(bp):
            if i + j < order:
                t = lax.dot_general(x, y, dims, preferred_element_type=F32)
                out = t if out is None else out + t
    return out


def _mm(a, b, dims=NN, pa=2, pb=2):
    return _mmp(_split_bf16(a, pa), _split_bf16(b, pb), dims)


def _dot1(a, b, dims=NN):
    return lax.dot_general(a.astype(BF16), b.astype(BF16), dims, preferred_element_type=F32)


def _lockstep(gens):
    results = [None] * len(gens)
    while any(r is None for r in results):
        results = [next(g) if r is None else r for g, r in zip(gens, results)]
    return results


def _sigmoid(x):
    return 1.0 / (1.0 + jnp.exp(-x))


def _silu(x):
    return x * _sigmoid(x)


def _rel_bucket_np(dist):
    d = np.maximum(dist, 1).astype(np.float32)
    large = REL_EXACT + (np.log(d / np.float32(REL_EXACT)) / np.float32(math.log(REL_MAX_DIST / REL_EXACT))
                         * np.float32(N_REL_BUCKETS - REL_EXACT)).astype(np.int32)
    large = np.minimum(large, N_REL_BUCKETS - 1)
    return np.where(dist < REL_EXACT, dist, large)


def _bias_rows(rel_bias, dist):
    idx = _rel_bucket_np(np.asarray(dist, np.int32))
    onehot = jnp.asarray(np.eye(N_REL_BUCKETS, dtype=np.float32)[idx.reshape(-1)])
    rows = jnp.dot(onehot, rel_bias, precision=lax.Precision.HIGHEST)
    return rows.reshape(idx.shape + (rel_bias.shape[1],))


def _ada_kernel(c_ref, w_ref, b_ref, o_ref):
    a = _silu(c_ref[...])
    o_ref[...] = _dot1(a, w_ref[...]) + b_ref[...]


def _ada_modulation(c_all, w_ada, b_ada):
    depth = w_ada.shape[0]
    rows = c_all.shape[0]
    tn = D_MODEL
    return pl.pallas_call(
        _ada_kernel,
        out_shape=jax.ShapeDtypeStruct((depth, rows, 6 * D_MODEL), F32),
        grid=(depth, 6 * D_MODEL // tn),
        in_specs=[
            pl.BlockSpec((rows, D_MODEL), lambda l, j: (0, 0)),
            pl.BlockSpec((None, D_MODEL, tn), lambda l, j: (l, 0, j)),
            pl.BlockSpec((None, 1, tn), lambda l, j: (l, 0, j)),
        ],
        out_specs=pl.BlockSpec((None, rows, tn), lambda l, j: (l, 0, j)),
        compiler_params=_cparams(("arbitrary", "arbitrary")),
        name="ada_modulation",
    )(c_all, w_ada, b_ada.reshape(depth, 1, 6 * D_MODEL))


def _in_proj_kernel(x_ref, sc_ref, sh_ref, g_ref, w_ref, uc_ref, ur_ref, q_ref, k_ref, v_ref):
    x = x_ref[...]
    ms = jnp.mean(x * x, axis=-1, keepdims=True)
    h = x * lax.rsqrt(ms + RMS_EPS) * g_ref[...]
    h = h * (1.0 + sc_ref[...]) + sh_ref[...]
    hb = h.astype(BF16)
    uc_ref[...] = lax.dot_general(hb, w_ref[:, 0:CONV_COLS], NN, preferred_element_type=F32)
    o = CONV_COLS
    ur_ref[...] = lax.dot_general(hb, w_ref[:, o:o + RWKV_COLS], NN, preferred_element_type=F32)
    o += RWKV_COLS
    for ref in (q_ref, k_ref, v_ref):
        res = lax.dot_general(hb, w_ref[:, o:o + ATT_WIDTH], NN, preferred_element_type=F32)
        for p in range(N_PAIRS):
            ref[p] = res[:, p * LANES:(p + 1) * LANES]
        o += ATT_WIDTH


def _mod_spec(mod, tiles_per_group):
    _, r, d = mod.shape
    return pl.BlockSpec((None, r, d), lambda i: (i // tiles_per_group, 0, 0))


def _in_proj(x2, scale, shift, g, w_bf, tm, tiles_per_group):
    m = x2.shape[0]
    flat = lambda w: jax.ShapeDtypeStruct((m, w), F32)
    pairs = jax.ShapeDtypeStruct((N_PAIRS, m, LANES), F32)
    pair_spec = pl.BlockSpec((N_PAIRS, tm, LANES), lambda i: (0, i, 0))
    return pl.pallas_call(
        _in_proj_kernel,
        out_shape=(flat(CONV_COLS), flat(RWKV_COLS), pairs, pairs, pairs),
        grid=(m // tm,),
        in_specs=[
            pl.BlockSpec((tm, D_MODEL), lambda i: (i, 0)),
            _mod_spec(scale, tiles_per_group),
            _mod_spec(shift, tiles_per_group),
            pl.BlockSpec((1, D_MODEL), lambda i: (0, 0)),
            pl.BlockSpec((D_MODEL, IN_COLS), lambda i: (0, 0)),
        ],
        out_specs=(pl.BlockSpec((tm, CONV_COLS), lambda i: (i, 0)), pl.BlockSpec((tm, RWKV_COLS), lambda i: (i, 0)),
                   pair_spec, pair_spec, pair_spec),
        compiler_params=_cparams(("arbitrary",)),
        name="in_proj",
    )(x2, scale, shift, g.reshape(1, D_MODEL), w_bf)


def _pairs_to_rows(x):
    return jnp.transpose(x, (1, 0, 2)).reshape(x.shape[1], ATT_WIDTH)


def _rows_to_pairs(x):
    return jnp.transpose(x.reshape(x.shape[0], N_PAIRS, LANES), (1, 0, 2))


def _conv_core(zs_ref, zsh_ref, w_ref, b_ref, lg_ref, lb_ref, y_ref, tt):
    n_rows = CONV_HALO + tt
    zs_ref[n_rows:n_rows + SUBLANES, :] = jnp.zeros((SUBLANES, CONV_WIDTH), F32)
    for s in range(SUBLANES):
        zsh_ref[s] = zs_ref[pl.ds(s, n_rows), :]
    off = CONV_HALO - (CONV_K - 1)
    sub = min(tt, 64)
    for r0 in range(0, tt, sub):
        acc = jnp.zeros((sub, CONV_WIDTH), F32)
        for j in range(CONV_K):
            a, b = divmod(off + j, SUBLANES)
            acc = acc + w_ref[j:j + 1, :] * zsh_ref[b, pl.ds(r0 + SUBLANES * a, sub), :]
        y = acc + b_ref[...]
        mu = jnp.mean(y, axis=-1, keepdims=True)
        yc = y - mu
        var = jnp.mean(yc * yc, axis=-1, keepdims=True)
        yn = yc * lax.rsqrt(var + LN_EPS) * lg_ref[...] + lb_ref[...]
        y_ref[r0:r0 + sub, :] = _silu(yn)


def _glu(u):
    return u[:, :CONV_WIDTH] * _sigmoid(u[:, CONV_WIDTH:])


def _conv_prompt_kernel(u_ref, uh_ref, w_ref, b_ref, lg_ref, lb_ref, y_ref, zt_ref, zs_ref, zsh_ref, *, tt):
    j = pl.program_id(1)
    zh = _glu(uh_ref[...])
    zs_ref[0:CONV_HALO, :] = jnp.where(j > 0, zh, 0.0)
    zs_ref[CONV_HALO:CONV_HALO + tt, :] = _glu(u_ref[...])
    _conv_core(zs_ref, zsh_ref, w_ref, b_ref, lg_ref, lb_ref, y_ref, tt)
    zt_ref[...] = zs_ref[tt:tt + CONV_HALO, :]


def _conv_sample_kernel(u_ref, zh_ref, w_ref, b_ref, lg_ref, lb_ref, y_ref, z_ref, zs_ref, zsh_ref, *, tt):
    z = _glu(u_ref[...])
    zs_ref[0:CONV_HALO, :] = zh_ref[...]
    zs_ref[CONV_HALO:CONV_HALO + tt, :] = z
    _conv_core(zs_ref, zsh_ref, w_ref, b_ref, lg_ref, lb_ref, y_ref, tt)
    z_ref[...] = z


def _conv_param_specs(nargs_grid):
    cmap = (lambda n, j: (0, 0)) if nargs_grid == 2 else (lambda n: (0, 0))
    return [
        pl.BlockSpec((CONV_HALO, CONV_WIDTH), cmap),
        pl.BlockSpec((1, CONV_WIDTH), cmap),
        pl.BlockSpec((1, CONV_WIDTH), cmap),
        pl.BlockSpec((1, CONV_WIDTH), cmap),
    ]


def _conv_params(conv_w, conv_b, ln_g, ln_b):
    w = jnp.pad(conv_w, ((0, CONV_HALO - CONV_K), (0, 0)))
    return w, conv_b.reshape(1, -1), ln_g.reshape(1, -1), ln_b.reshape(1, -1)


def _conv_prompt(uc, n, t, params, tt=512):
    u3 = uc.reshape(n, t, CONV_COLS)
    hb = tt // CONV_HALO
    y, zt = pl.pallas_call(
        functools.partial(_conv_prompt_kernel, tt=tt),
        out_shape=(jax.ShapeDtypeStruct((n, t, CONV_WIDTH), F32),
                   jax.ShapeDtypeStruct((n, CONV_HALO, CONV_WIDTH), F32)),
        grid=(n, t // tt),
        in_specs=[
            pl.BlockSpec((None, tt, CONV_COLS), lambda b, j: (b, j, 0)),
            pl.BlockSpec((None, CONV_HALO, CONV_COLS), lambda b, j: (b, jnp.maximum(j * hb - 1, 0), 0)),
        ] + _conv_param_specs(2),
        out_specs=(pl.BlockSpec((None, tt, CONV_WIDTH), lambda b, j: (b, j, 0)),
                   pl.BlockSpec((None, CONV_HALO, CONV_WIDTH), lambda b, j: (b, 0, 0))),
        scratch_shapes=[pltpu.VMEM((CONV_HALO + tt + SUBLANES, CONV_WIDTH), F32),
                        pltpu.VMEM((SUBLANES, CONV_HALO + tt, CONV_WIDTH), F32)],
        compiler_params=_cparams(("arbitrary", "arbitrary")),
        name="conv_prompt",
    )(u3, u3, *params)
    return y.reshape(n * t, CONV_WIDTH), zt[:, CONV_HALO - (CONV_K - 1):]


def _conv_sample(uc, cache, n, t, params):
    tt = SUBLANES
    u3 = jnp.pad(uc.reshape(n, t, CONV_COLS), ((0, 0), (0, tt - t), (0, 0)))
    zh = jnp.pad(cache, ((0, 0), (CONV_HALO - (CONV_K - 1), 0), (0, 0)))
    y, z = pl.pallas_call(
        functools.partial(_conv_sample_kernel, tt=tt),
        out_shape=(jax.ShapeDtypeStruct((n, tt, CONV_WIDTH), F32),
                   jax.ShapeDtypeStruct((n, tt, CONV_WIDTH), F32)),
        grid=(n,),
        in_specs=[
            pl.BlockSpec((None, tt, CONV_COLS), lambda b: (b, 0, 0)),
            pl.BlockSpec((None, CONV_HALO, CONV_WIDTH), lambda b: (b, 0, 0)),
        ] + _conv_param_specs(1),
        out_specs=(pl.BlockSpec((None, tt, CONV_WIDTH), lambda b: (b, 0, 0)),
                   pl.BlockSpec((None, tt, CONV_WIDTH), lambda b: (b, 0, 0))),
        scratch_shapes=[pltpu.VMEM((CONV_HALO + tt + SUBLANES, CONV_WIDTH), F32),
                        pltpu.VMEM((SUBLANES, CONV_HALO + tt, CONV_WIDTH), F32)],
        compiler_params=_cparams(("arbitrary",)),
        name="conv_sample",
    )(u3, zh, *params)
    new_cache = jnp.concatenate([cache[:, t:], z[:, :t]], axis=1)
    return y[:, :t].reshape(n * t, CONV_WIDTH), new_cache


def _rwkv_consts():
    c = RWKV_CHUNK
    rows = np.arange(c)
    tri = (rows[None, :] <= rows[:, None]).astype(np.float32)
    lanes = np.arange(RWKV_WIDTH)
    bd = (lanes[:, None] // HEAD_DIM == lanes[None, :] // HEAD_DIM).astype(np.float32)
    return jnp.asarray(tri, BF16), jnp.asarray(bd, BF16)


def _rwkv_kernel(u_ref, up_ref, sh0_ref, s0_ref, mu_ref, w0_ref, wwa_ref, a0_ref, g2_ref, kk_ref, ka_ref, rk_ref,
                 lg_ref, lb_ref, tri_ref, bd_ref,
                 y_ref, so_ref,
                 us_ref, lw_ref, kn_ref, kb_ref, k2_ref, rr_ref, vv_ref, bon_ref, gate_ref, yy_ref, s_ref,
                 *, tr, t_valid, nseq):
    j = pl.program_id(1)
    c = RWKV_CHUNK
    bd = bd_ref[...]
    o = 3 * RWKV_WIDTH

    @pl.when(j == 0)
    def _():
        s_ref[...] = s0_ref[...]

    for sq in range(nseq):
        u = u_ref[sq]
        prev = jnp.where(j > 0, up_ref[sq], jnp.broadcast_to(sh0_ref[sq], (SUBLANES, RWKV_COLS)))
        us_ref[sq, 0:SUBLANES, :] = prev
        us_ref[sq, SUBLANES:SUBLANES + tr, :] = u
        u_prev = us_ref[sq, pl.ds(SUBLANES - 1, tr), :]
        xs = u + (u_prev - u) * mu_ref[...]
        r = xs[:, :RWKV_WIDTH]
        k = xs[:, RWKV_WIDTH:2 * RWKV_WIDTH]
        v = xs[:, 2 * RWKV_WIDTH:o]
        lo = xs[:, o:o + DECAY_LORA + AAA_LORA]
        lane = lax.broadcasted_iota(jnp.int32, lo.shape, 1)
        lo = jnp.where(lane < DECAY_LORA, jnp.tanh(lo), lo)
        wa = _dot1(lo, wwa_ref[...])
        g = _dot1(_sigmoid(xs[:, o + DECAY_LORA + AAA_LORA:]), g2_ref[...])
        lw = -DECAY_SCALE * _sigmoid(w0_ref[...] + wa[:, :RWKV_WIDTH])
        a = _sigmoid(a0_ref[...] + wa[:, RWKV_WIDTH:])
        kkr = k * kk_ref[...]
        ss = _mm(kkr * kkr, bd, pa=2)
        kk = kkr / jnp.maximum(jnp.sqrt(ss), 1e-12)
        k2 = k * (1.0 + (a - 1.0) * ka_ref[...])
        if t_valid < tr:
            valid = lax.broadcasted_iota(jnp.int32, (tr, RWKV_WIDTH), 0) < t_valid
            zero = jnp.zeros_like(r)
            r, k2, v, kk, lw = (jnp.where(valid, z, zero) for z in (r, k2, v, kk, lw))
        lw_ref[sq] = lw
        kn_ref[sq] = -kk
        kb_ref[sq] = kk * a
        k2_ref[sq] = k2
        rr_ref[sq] = r
        vv_ref[sq] = v
        bon_ref[sq] = _mm(r * k2 * rk_ref[...], bd, pa=2) * v
        gate_ref[sq] = g

    row = lax.broadcasted_iota(jnp.int32, (c, LANES), 0)
    col = lax.broadcasted_iota(jnp.int32, (c, LANES), 1) & (HEAD_DIM - 1)
    strict = col < row
    incl = col <= row
    ident = jnp.where(col == row, 1.0, 0.0)
    bd2 = bd[0:LANES, 0:LANES]
    bd2_mask = bd2 > 0.5
    n_steps = int(round(math.log2(c)))

    def dot(a, b, dims=NN):
        return lax.dot_general(a.astype(BF16), b, dims, preferred_element_type=F32)

    def stack(x):
        xb = x.astype(BF16)
        return jnp.concatenate([xb, xb], axis=0) * bd2

    def rows2(a, b):
        return jnp.concatenate([a, b], axis=0)

    def chunk_pair(pt, bt, kt, rt, vc, gc, s0):
        zero = jnp.zeros((c, LANES), F32)
        pr = rows2(pt, rt)
        xb = dot(pr, stack(bt), NT)
        xk = dot(pr, stack(kt), NT)
        yield None
        lm = jnp.where(strict, xb[:c], zero)
        mm_ = jnp.where(strict, xk[:c], zero)
        qb = jnp.where(incl, xb[c:], zero)
        qk = jnp.where(incl, xk[c:], zero)
        tinv = ident + lm
        lk = dot(lm, stack(lm))
        mq = dot(rows2(mm_, qk), stack(vc))
        yield None
        for step in range(1, n_steps):
            if step + 1 < n_steps:
                both = dot(rows2(lk, tinv), stack(lk))
                lk = both[:c]
                tinv = tinv + both[c:]
            else:
                tinv = tinv + dot(tinv, stack(lk))
            yield None
        p2 = dot(tinv, stack(pt))
        u0 = dot(tinv, stack(mq[:c]))
        yield None
        xs_ = dot(rows2(p2, rt), stack(s0), NT)
        yield None
        uu = xs_[:c] + u0
        y = xs_[c:] + dot(qb, stack(uu)) + mq[c:]
        uvt = rows2(uu, vc).T
        bk = rows2(bt * gc, kt * gc).astype(BF16)
        z = jnp.where(bd2_mask, dot(uvt, bk), 0.0)
        yield y, s0 * gc + z[0:c] + z[c:2 * c]

    def chunk(ci, carry):
        sl = pl.ds(pl.multiple_of(ci * c, c), c)
        work = []
        for sq in range(nseq):
            lw_c = lw_ref[sq, sl, :]
            gi = _mmp([tri_ref[...]], _split_bf16(lw_c, 3))
            e_neg = jnp.exp(-gi)
            gc = jnp.exp(gi[c - 1:c, :])
            pt = kn_ref[sq, sl, :] * jnp.exp(gi - lw_c)
            bt = kb_ref[sq, sl, :] * e_neg
            kt = k2_ref[sq, sl, :] * e_neg
            rt = rr_ref[sq, sl, :] * jnp.exp(gi)
            vc = vv_ref[sq, sl, :]
            s0 = s_ref[sq]
            for p in range(RWKV_WIDTH // LANES):
                ln = slice(p * LANES, (p + 1) * LANES)
                work.append((sq, ln, tuple(x[:, ln] for x in (pt, bt, kt, rt, vc, gc, s0))))
        results = _lockstep([chunk_pair(*args) for _, _, args in work])
        done = [(sq, ln, res) for (sq, ln, _), res in zip(work, results)]
        for sq, ln, (y, s_new) in done:
            yy_ref[sq, sl, ln] = y
            s_ref[sq, :, ln] = s_new
        return carry

    lax.fori_loop(0, tr // c, chunk, 0)

    inv = 1.0 / HEAD_DIM
    for sq in range(nseq):
        y = yy_ref[sq]
        mu = _mm(y, bd, pa=2) * inv
        yc = y - mu
        var = _mm(yc * yc, bd, pa=2) * inv
        yn = yc * lax.rsqrt(var + RWKV_GN_EPS) * lg_ref[...] + lb_ref[...]
        y_ref[sq] = (yn + bon_ref[sq]) * gate_ref[sq]
    so_ref[...] = s_ref[...]


def _rwkv_params(p):
    z = jnp.zeros((DECAY_LORA, RWKV_WIDTH), F32)
    wwa = jnp.concatenate([jnp.concatenate([p['rwkv_w2'], z], axis=1),
                           jnp.concatenate([z, p['rwkv_a2']], axis=1)], axis=0)
    row = lambda x: x.reshape(1, -1)
    return (row(p['rwkv_mu']), row(p['rwkv_w0']), wwa, row(p['rwkv_a0']), p['rwkv_g2'], row(p['rwkv_k_k']),
            row(p['rwkv_k_a']), row(p['rwkv_r_k']), row(p['rwkv_ln_g']), row(p['rwkv_ln_b']))


def _rwkv(u3, shift0, s0, params, tr, t_valid, nseq):
    n, t, _ = u3.shape
    assert n % nseq == 0
    consts = _rwkv_consts()
    hb = tr // SUBLANES
    const2 = lambda b, j: (0, 0)
    full = lambda arr: pl.BlockSpec(arr.shape, const2)
    tile = lambda: pltpu.VMEM((nseq, tr, RWKV_WIDTH), F32)
    y, s_out = pl.pallas_call(
        functools.partial(_rwkv_kernel, tr=tr, t_valid=t_valid, nseq=nseq),
        out_shape=(jax.ShapeDtypeStruct((n, t, RWKV_WIDTH), F32),
                   jax.ShapeDtypeStruct((n, HEAD_DIM, RWKV_WIDTH), F32)),
        grid=(n // nseq, t // tr),
        in_specs=[
            pl.BlockSpec((nseq, tr, RWKV_COLS), lambda b, j: (b, j, 0)),
            pl.BlockSpec((nseq, SUBLANES, RWKV_COLS), lambda b, j: (b, jnp.maximum(j * hb - 1, 0), 0)),
            pl.BlockSpec((nseq, 1, RWKV_COLS), lambda b, j: (b, 0, 0)),
            pl.BlockSpec((nseq, HEAD_DIM, RWKV_WIDTH), lambda b, j: (b, 0, 0)),
        ] + [full(x) for x in params] + [full(x) for x in consts],
        out_specs=(pl.BlockSpec((nseq, tr, RWKV_WIDTH), lambda b, j: (b, j, 0)),
                   pl.BlockSpec((nseq, HEAD_DIM, RWKV_WIDTH), lambda b, j: (b, 0, 0))),
        scratch_shapes=[pltpu.VMEM((nseq, SUBLANES + tr, RWKV_COLS), F32)] + [tile() for _ in range(9)]
                       + [pltpu.VMEM((nseq, HEAD_DIM, RWKV_WIDTH), F32)],
        compiler_params=_cparams(("arbitrary", "arbitrary")),
        name="rwkv7",
    )(u3, u3, shift0, s0, *params, *consts)
    return y, s_out


def _state_to_lanes(s):
    n = s.shape[0]
    return jnp.transpose(s, (0, 2, 1, 3)).reshape(n, HEAD_DIM, RWKV_WIDTH)


def _lanes_to_state(s):
    n = s.shape[0]
    return jnp.transpose(s.reshape(n, HEAD_DIM, N_RWKV_HEADS, HEAD_DIM), (0, 2, 1, 3))


def _attn_prompt_kernel(q_ref, kc_ref, kp_ref, vc_ref, vp_ref, w_ref, o_ref,
                        kcat, vcat, acc, m0, m1, l0, l1, bias_s):
    u = pl.program_id(2)
    unit = ATT_UNIT
    scale = HEAD_DIM ** -0.5
    kcat[0:unit, :] = kp_ref[...]
    kcat[unit:2 * unit, :] = kc_ref[...]
    vcat[0:unit, :] = vp_ref[...]
    vcat[unit:2 * unit, :] = vc_ref[...]
    for bi in range(len(DILATIONS)):
        for half in range(2):
            wrow = jnp.broadcast_to(w_ref[bi, half:half + 1, :], (BLOCK, 2 * BLOCK))
            bias_s[bi, half * BLOCK:(half + 1) * BLOCK, :] = pltpu.roll(wrow, 0, 1, stride=1, stride_axis=0)
    row = lax.broadcasted_iota(jnp.int32, (2 * BLOCK, 2 * BLOCK), 0) & (BLOCK - 1)
    col = lax.broadcasted_iota(jnp.int32, (2 * BLOCK, 2 * BLOCK), 1)
    delta = row + BLOCK - col
    band = (delta >= 0) & (delta <= BLOCK)
    lo = lax.broadcasted_iota(jnp.int32, (BLOCK, LANES), 1) < HEAD_DIM
    n_br = len(DILATIONS)
    rows2 = lambda a, b: jnp.concatenate([a, b], axis=0)
    for bi, (window, dil) in enumerate(DILATIONS):
        first = bi == 0
        last = bi == n_br - 1
        shift = int(round(math.log2(dil)))

        def sub_block(j, bi, dil, shift, first, last):
            r = j & (dil - 1)
            blk = j >> shift
            start = blk * (BLOCK * dil) + r
            rows_q = pl.ds(start, BLOCK, stride=dil)
            rows_k = pl.ds(unit + start - BLOCK * dil, 2 * BLOCK, stride=dil)
            q = q_ref[rows_q, :] * scale
            kk = kcat[rows_k, :].astype(BF16)
            vv = vcat[rows_k, :].astype(BF16)
            valid = band & ((col >= BLOCK) | (u > 0) | (blk > 0))
            q2 = rows2(jnp.where(lo, q, 0.0), jnp.where(lo, 0.0, q)).astype(BF16)
            s = lax.dot_general(q2, kk, NT, preferred_element_type=F32)
            if not first:
                m_old = rows2(m0[rows_q, :], m1[rows_q, :])
                l_old = rows2(l0[rows_q, :], l1[rows_q, :])
                acc_old = acc[rows_q, :]
            yield None
            s = jnp.where(valid, s + bias_s[bi], NEG_INF)
            m_cur = jnp.max(s, axis=-1, keepdims=True)
            if first:
                m_new = jnp.broadcast_to(m_cur, (2 * BLOCK, LANES))
            else:
                m_new = jnp.maximum(m_old, m_cur)
                alpha = jnp.exp(m_old - m_new)
            p = jnp.exp(s - jnp.concatenate([m_new, m_new], axis=1))
            l_new = jnp.sum(p, axis=-1, keepdims=True)
            if first:
                l_new = jnp.broadcast_to(l_new, (2 * BLOCK, LANES))
            else:
                l_new = alpha * l_old + l_new
            o2 = lax.dot_general(p.astype(BF16), vv, NN, preferred_element_type=F32)
            yield None
            o_pair = jnp.where(lo, o2[:BLOCK], o2[BLOCK:])
            if not first:
                o_pair = o_pair + jnp.where(lo, alpha[:BLOCK], alpha[BLOCK:]) * acc_old
            if last:
                linv = 1.0 / l_new
                yield [(o_ref, rows_q, o_pair * jnp.where(lo, linv[:BLOCK], linv[BLOCK:]))]
            else:
                yield [(m0, rows_q, m_new[:BLOCK]), (m1, rows_q, m_new[BLOCK:]),
                       (l0, rows_q, l_new[:BLOCK]), (l1, rows_q, l_new[BLOCK:]), (acc, rows_q, o_pair)]

        group = ATT_GROUP[bi]

        def body(jg, carry, bi=bi, dil=dil, shift=shift, first=first, last=last, group=group):
            groups = _lockstep([sub_block(jg * group + g, bi, dil, shift, first, last) for g in range(group)])
            for stores in groups:
                for ref, rows, val in stores:
                    ref[rows, :] = val
            return carry

        lax.fori_loop(0, unit // BLOCK // group, body, 0)


def _attn_prompt_bias(rel_bias):
    c = np.arange(2 * BLOCK)
    tabs = []
    for _, dil in DILATIONS:
        tabs.append(_bias_rows(rel_bias, np.maximum(BLOCK - c, 0) * dil).T)
    w = jnp.stack(tabs)
    return jnp.transpose(w.reshape(len(DILATIONS), N_PAIRS, 2, 2 * BLOCK), (1, 0, 2, 3))


def _attn_prompt(q, k, v, wtab, n, t):
    unit = ATT_UNIT
    assert t % unit == 0
    r4 = lambda x: x.reshape(N_PAIRS, n, t, LANES)
    cur = pl.BlockSpec((None, None, unit, LANES), lambda a, p, u: (p, a, u, 0))
    prv = pl.BlockSpec((None, None, unit, LANES), lambda a, p, u: (p, a, jnp.maximum(u - 1, 0), 0))
    stat = lambda: pltpu.VMEM((unit, LANES), F32)
    out = pl.pallas_call(
        _attn_prompt_kernel,
        out_shape=jax.ShapeDtypeStruct((N_PAIRS, n, t, LANES), F32),
        grid=(n, N_PAIRS, t // unit),
        in_specs=[cur, cur, prv, cur, prv,
                  pl.BlockSpec((None, len(DILATIONS), 2, 2 * BLOCK), lambda a, p, u: (p, 0, 0, 0))],
        out_specs=cur,
        scratch_shapes=[pltpu.VMEM((2 * unit, LANES), F32), pltpu.VMEM((2 * unit, LANES), F32),
                        stat(), stat(), stat(), stat(), stat(),
                        pltpu.VMEM((len(DILATIONS), 2 * BLOCK, 2 * BLOCK), F32)],
        compiler_params=_cparams(("arbitrary", "arbitrary", "arbitrary")),
        name="attn_prompt",
    )(r4(q), r4(k), r4(k), r4(v), r4(v), wtab)
    return out.reshape(N_PAIRS, n * t, LANES)


def _attn_sample_kernel(q_ref, kn_ref, vn_ref, kc_ref, vc_ref, bt_ref, bn_ref, cnt_ref, cn_ref, hm_ref, o_ref,
                        kn_s, vn_s, *, s_new):
    scale = HEAD_DIM ** -0.5
    hm = hm_ref[...]
    qm = jnp.concatenate([(q_ref[s:s + 1, :] * scale) * hm for s in range(s_new)], axis=0).astype(BF16)
    kn_s[...] = jnp.zeros_like(kn_s)
    vn_s[...] = jnp.zeros_like(vn_s)
    kn_s[0:s_new, :] = kn_ref[...]
    vn_s[0:s_new, :] = vn_ref[...]
    sc = lax.dot_general(qm, kc_ref[...].astype(BF16), NN, preferred_element_type=F32) + bt_ref[...]
    sn = lax.dot_general(qm, kn_s[...].astype(BF16), NT, preferred_element_type=F32) + bn_ref[...]
    cnt = cnt_ref[...]
    cn = cn_ref[...]
    sc = jnp.where(cnt > 0.0, sc, NEG_INF)
    sn = jnp.where(cn > 0.0, sn, NEG_INF)
    m = jnp.maximum(jnp.max(sc, axis=-1, keepdims=True), jnp.max(sn, axis=-1, keepdims=True))
    p = cnt * jnp.exp(sc - m)
    pn = cn * jnp.exp(sn - m)
    l = jnp.sum(p, axis=-1, keepdims=True) + jnp.sum(pn, axis=-1, keepdims=True)
    o = (lax.dot_general(p.astype(BF16), vc_ref[...].astype(BF16), NT, preferred_element_type=F32)
         + lax.dot_general(pn.astype(BF16), vn_s[...].astype(BF16), NN, preferred_element_type=F32))
    o = o / l
    for s in range(s_new):
        o_ref[s:s + 1, :] = jnp.sum(o[s * N_ATT_HEADS:(s + 1) * N_ATT_HEADS] * hm, axis=0, keepdims=True)


def _attn_sample_tables(rel_bias, s_new, win):
    s = np.arange(s_new)[:, None]
    dist_c = win + s - np.arange(win)[None, :]
    dist_n = s - np.arange(LANES)[None, :]
    cnt_c = np.zeros(dist_c.shape, np.float32)
    cnt_n = np.zeros(dist_n.shape, np.float32)
    for window, dil in DILATIONS:
        cnt_c += ((dist_c % dil == 0) & (dist_c <= window)).astype(np.float32)
        cnt_n += ((dist_n >= 0) & (dist_n % dil == 0) & (dist_n <= window)).astype(np.float32)
    rep = lambda x: jnp.asarray(np.repeat(x, N_ATT_HEADS, axis=0))
    rows = lambda b: jnp.transpose(b, (0, 2, 1)).reshape(s_new * N_ATT_HEADS, -1)
    return (rows(_bias_rows(rel_bias, dist_c)), rows(_bias_rows(rel_bias, np.maximum(dist_n, 0))),
            rep(cnt_c), rep(cnt_n))


def _attn_sample(q, k, v, k_cache, v_cache, li, tables, n, s_new):
    win = k_cache.shape[-1]
    assert win == MAX_WINDOW
    lanes = np.arange(ATT_WIDTH)
    hm = jnp.asarray((np.arange(N_ATT_HEADS)[:, None] == lanes[None, :] // HEAD_DIM).astype(np.float32))
    new = pl.BlockSpec((None, s_new, ATT_WIDTH), lambda b: (b, 0, 0))
    buf = pl.BlockSpec((None, None, ATT_WIDTH, win), lambda b: (li, b, 0, 0))
    full = lambda arr: pl.BlockSpec(arr.shape, lambda b: (0, 0))
    r3 = lambda x: x.reshape(n, s_new, ATT_WIDTH)
    out = pl.pallas_call(
        functools.partial(_attn_sample_kernel, s_new=s_new),
        out_shape=jax.ShapeDtypeStruct((n, s_new, ATT_WIDTH), F32),
        grid=(n,),
        in_specs=[new, new, new, buf, buf] + [full(x) for x in tables] + [full(hm)],
        out_specs=new,
        scratch_shapes=[pltpu.VMEM((LANES, ATT_WIDTH), F32), pltpu.VMEM((LANES, ATT_WIDTH), F32)],
        compiler_params=_cparams(("arbitrary",)),
        name="attn_sample",
    )(r3(q), r3(k), r3(v), k_cache, v_cache, *tables, hm)
    return out.reshape(n * s_new, ATT_WIDTH)


def _out_proj_kernel(ya_ref, yb_ref, yc_ref, x_ref, gate_ref, g_ref, w_ref, o_ref):
    a = CONV_WIDTH
    b = CONV_WIDTH + RWKV_WIDTH
    yc = jnp.concatenate([yc_ref[p] for p in range(N_PAIRS)], axis=1)
    mix = _dot1(ya_ref[...], w_ref[0:a, :]) + _dot1(yb_ref[...], w_ref[a:b, :]) + _dot1(yc, w_ref[b:, :])
    ms = jnp.mean(mix * mix, axis=-1, keepdims=True)
    o_ref[...] = x_ref[...] + gate_ref[...] * (mix * lax.rsqrt(ms + RMS_EPS) * g_ref[...])


def _out_proj(ya, yb, yc, x2, gate, g, w_bf, tm, tiles_per_group):
    m = x2.shape[0]
    rows = lambda w: pl.BlockSpec((tm, w), lambda i: (i, 0))
    return pl.pallas_call(
        _out_proj_kernel,
        out_shape=jax.ShapeDtypeStruct((m, D_MODEL), F32),
        grid=(m // tm,),
        in_specs=[rows(CONV_WIDTH), rows(RWKV_WIDTH),
                  pl.BlockSpec((N_PAIRS, tm, LANES), lambda i: (0, i, 0)), rows(D_MODEL),
                  _mod_spec(gate, tiles_per_group),
                  pl.BlockSpec((1, D_MODEL), lambda i: (0, 0)),
                  pl.BlockSpec((D_MODEL, D_MODEL), lambda i: (0, 0))],
        out_specs=rows(D_MODEL),
        compiler_params=_cparams(("arbitrary",)),
        name="out_proj",
    )(ya, yb, yc, x2, gate, g.reshape(1, D_MODEL), w_bf)


def _ffn_kernel(x_ref, sc_ref, sh_ref, gate_ref, gpre_ref, gpost_ref, wg_ref, wu_ref, wo_ref, o_ref):
    x = x_ref[...]
    ms = jnp.mean(x * x, axis=-1, keepdims=True)
    h = x * lax.rsqrt(ms + RMS_EPS) * gpre_ref[...]
    hb = (h * (1.0 + sc_ref[...]) + sh_ref[...]).astype(BF16)
    y = None
    for c in range(FFN_SPLIT):
        cols = slice(c * (D_FF // FFN_SPLIT), (c + 1) * (D_FF // FFN_SPLIT))
        gg = lax.dot_general(hb, wg_ref[:, cols], NN, preferred_element_type=F32)
        uu = lax.dot_general(hb, wu_ref[:, cols], NN, preferred_element_type=F32)
        act = (_silu(gg) * uu).astype(BF16)
        part = lax.dot_general(act, wo_ref[cols, :], NN, preferred_element_type=F32)
        y = part if y is None else y + part
    ms = jnp.mean(y * y, axis=-1, keepdims=True)
    o_ref[...] = x + gate_ref[...] * (y * lax.rsqrt(ms + RMS_EPS) * gpost_ref[...])


def _ffn(x2, scale, shift, gate, g_pre, g_post, wg, wu, wo, tm, tiles_per_group):
    m = x2.shape[0]
    mod = lambda arr: pl.BlockSpec((None,) + arr.shape[1:], lambda i: (i // tiles_per_group, 0, 0))
    vec = pl.BlockSpec((1, D_MODEL), lambda i: (0, 0))
    resident = lambda arr: pl.BlockSpec(arr.shape, lambda i: (0, 0), pipeline_mode=pl.Buffered(1))
    return pl.pallas_call(
        _ffn_kernel,
        out_shape=jax.ShapeDtypeStruct((m, D_MODEL), F32),
        grid=(m // tm,),
        in_specs=[pl.BlockSpec((tm, D_MODEL), lambda i: (i, 0)),
                  mod(scale), mod(shift), mod(gate), vec, vec,
                  resident(wg), resident(wu), resident(wo)],
        out_specs=pl.BlockSpec((tm, D_MODEL), lambda i: (i, 0)),
        compiler_params=_cparams(("arbitrary",)),
        name="ffn",
    )(x2, scale, shift, gate, g_pre.reshape(1, D_MODEL), g_post.reshape(1, D_MODEL), wg, wu, wo)


def _heads(x_pairs, n, t):
    return _pairs_to_rows(x_pairs).reshape(n, t, N_ATT_HEADS, HEAD_DIM)


def _layer(x2, n, t, mods, lp, att, state, li):
    shift1, scale1, gate1, shift2, scale2, gate2 = mods
    prompt = state is None
    tm = 512 if prompt else n * t
    tpg = t // tm if prompt else 1
    uc, ur, q, k, v = _in_proj(x2, scale1, shift1, lp['g_pre_mix'], lp['w_in_bf'], tm, tpg)
    conv_params = _conv_params(lp['conv_w'], lp['conv_b'], lp['conv_ln_g'], lp['conv_ln_b'])
    rwkv_params = _rwkv_params(lp)
    ur3 = ur.reshape(n, t, RWKV_COLS)
    new_shift = ur3[:, -1]
    if prompt:
        ya, conv_cache = _conv_prompt(uc, n, t, conv_params)
        shift0 = jnp.zeros((n, 1, RWKV_COLS), F32)
        s0 = jnp.zeros((n, HEAD_DIM, RWKV_WIDTH), F32)
        yb, s_out = _rwkv(ur3, shift0, s0, rwkv_params, tr=512, t_valid=512, nseq=4)
        yc = _attn_prompt(q, k, v, att['prompt'], n, t)
        win = min(MAX_WINDOW, t)
        r4 = lambda x: x.reshape(N_PAIRS, n, t, LANES)[:, :, t - win:].reshape(N_PAIRS, n * win, LANES)
        new_k = _heads(r4(k), n, win)
        new_v = _heads(r4(v), n, win)
    else:
        cache_conv, state_shift, state_wkv, k_cache, v_cache = state
        ya, conv_cache = _conv_sample(uc, cache_conv[li], n, t, conv_params)
        tr = RWKV_CHUNK
        ur_pad = jnp.pad(ur3, ((0, 0), (0, tr - t), (0, 0)))
        yb, s_out = _rwkv(ur_pad, state_shift[li][:, None, :], _state_to_lanes(state_wkv[li]), rwkv_params,
                          tr=tr, t_valid=t, nseq=4)
        yb = yb[:, :t]
        q_r, k_r, v_r = _pairs_to_rows(q), _pairs_to_rows(k), _pairs_to_rows(v)
        yc = _rows_to_pairs(_attn_sample(q_r, k_r, v_r, k_cache, v_cache, li, att['sample'], n, t))
        new_k = k_r.reshape(n, t, N_ATT_HEADS, HEAD_DIM)
        new_v = v_r.reshape(n, t, N_ATT_HEADS, HEAD_DIM)
    yb = yb.reshape(n * t, RWKV_WIDTH)
    x2 = _out_proj(ya, yb, yc, x2, gate1, lp['g_post_mix'], lp['w_out_bf'], tm, tpg)
    tmf = 512 if prompt else n * t
    x2 = _ffn(x2, scale2, shift2, gate2, lp['g_pre_ffn'], lp['g_post_ffn'], lp['w_ffn_g_bf'], lp['w_ffn_u_bf'],
              lp['w_ffn_o_bf'], tmf, t // tmf if prompt else 1)
    return x2, (conv_cache, new_shift, _lanes_to_state(s_out), new_k, new_v)


def kernel(x_prompt, x_sample, c_prompt, c_sample, cache_conv, state_shift, state_wkv, cache_k_win, cache_v_win,
           w_ada, b_ada, g_pre_mix, g_post_mix, g_pre_ffn, g_post_ffn, w_in, w_out,
           conv_w, conv_b, conv_ln_g, conv_ln_b,
           rwkv_mu, rwkv_w0, rwkv_w2, rwkv_a0, rwkv_a2, rwkv_g2, rwkv_k_k, rwkv_k_a, rwkv_r_k,
           rwkv_ln_g, rwkv_ln_b, rel_bias, w_ffn_in, w_ffn_out):
    depth = w_in.shape[0]
    nb, tp, _ = x_prompt.shape
    ns, ts, _ = x_sample.shape
    rows = nb + ns
    rows_pad = -(-rows // SUBLANES) * SUBLANES
    c_all = jnp.pad(jnp.concatenate([c_prompt, c_sample], axis=0), ((0, rows_pad - rows), (0, 0)))
    mod = _ada_modulation(c_all, w_ada, b_ada)
    mod = mod.reshape(depth, rows_pad, 6, D_MODEL)

    win = cache_k_win.shape[2]
    att = {'prompt': _attn_prompt_bias(rel_bias), 'sample': _attn_sample_tables(rel_bias, ts, win)}
    to_pos_minor = lambda c: jnp.transpose(c, (0, 1, 3, 4, 2)).reshape(depth, ns, ATT_WIDTH, win)
    state = (cache_conv, state_shift, state_wkv, to_pos_minor(cache_k_win), to_pos_minor(cache_v_win))

    yp = x_prompt.reshape(nb * tp, D_MODEL)
    ys = x_sample.reshape(ns * ts, D_MODEL)
    outs_p, outs_s = [], []
    for li in range(depth):
        lp = dict(g_pre_mix=g_pre_mix[li], g_post_mix=g_post_mix[li], g_pre_ffn=g_pre_ffn[li],
                  g_post_ffn=g_post_ffn[li],
                  w_in_bf=w_in[li].astype(BF16), w_out_bf=w_out[li].astype(BF16),
                  w_ffn_g_bf=w_ffn_in[li, :, :D_FF].astype(BF16), w_ffn_u_bf=w_ffn_in[li, :, D_FF:].astype(BF16),
                  w_ffn_o_bf=w_ffn_out[li].astype(BF16),
                  conv_w=conv_w[li], conv_b=conv_b[li], conv_ln_g=conv_ln_g[li], conv_ln_b=conv_ln_b[li],
                  rwkv_mu=rwkv_mu[li], rwkv_w0=rwkv_w0[li], rwkv_w2=rwkv_w2[li], rwkv_a0=rwkv_a0[li],
                  rwkv_a2=rwkv_a2[li], rwkv_g2=rwkv_g2[li], rwkv_k_k=rwkv_k_k[li], rwkv_k_a=rwkv_k_a[li],
                  rwkv_r_k=rwkv_r_k[li], rwkv_ln_g=rwkv_ln_g[li], rwkv_ln_b=rwkv_ln_b[li])
        mods_p = tuple(mod[li, :nb, j][:, None, :] for j in range(6))
        mods_s = tuple(jnp.repeat(mod[li, nb:rows, j], ts, axis=0)[None] for j in range(6))
        yp, st_p = _layer(yp, nb, tp, mods_p, lp, att, None, li)
        ys, st_s = _layer(ys, ns, ts, mods_s, lp, att, state, li)
        outs_p.append(st_p)
        outs_s.append(st_s)
    stack = lambda outs, i: jnp.stack([o[i] for o in outs])
    return (yp.reshape(nb, tp, D_MODEL), ys.reshape(ns, ts, D_MODEL),
            stack(outs_p, 0), stack(outs_s, 0),
            stack(outs_p, 1), stack(outs_s, 1),
            stack(outs_p, 2), stack(outs_s, 2),
            stack(outs_p, 3), stack(outs_s, 3),
            stack(outs_p, 4), stack(outs_s, 4))
```

```python
import functools
import math

import numpy as np
import jax
import jax.numpy as jnp
from jax import lax
from jax.experimental import pallas as pl
from jax.experimental.pallas import tpu as pltpu

F32 = jnp.float32
BF16 = jnp.bfloat16

D_MODEL = 1024
HEAD_DIM = 64
CONV_WIDTH = 256
RWKV_WIDTH = 256
ATT_WIDTH = 512
N_RWKV_HEADS = RWKV_WIDTH // HEAD_DIM
N_ATT_HEADS = ATT_WIDTH // HEAD_DIM
CONV_K = 31
DECAY_LORA = 64
AAA_LORA = 64
GATE_LORA = 128
CONV_COLS = 2 * CONV_WIDTH
RWKV_COLS = 3 * RWKV_WIDTH + DECAY_LORA + AAA_LORA + GATE_LORA
ATT_COLS = 3 * ATT_WIDTH
IN_COLS = CONV_COLS + RWKV_COLS + ATT_COLS
DILATIONS = ((128, 1), (512, 4), (2048, 16))
MAX_WINDOW = 2048
BLOCK = 128
N_REL_BUCKETS = 32
REL_EXACT = N_REL_BUCKETS // 2
REL_MAX_DIST = MAX_WINDOW
D_FF = 2816
RMS_EPS = 1e-6
LN_EPS = 1e-5
RWKV_GN_EPS = HEAD_DIM * 1e-5
DECAY_SCALE = math.exp(-0.5)
NEG_INF = -1e30

VMEM_LIMIT_BYTES = 56 * 1024 * 1024
SUBLANES = 8
LANES = 128
CONV_HALO = 32
FFN_SPLIT = 2
RWKV_CHUNK = HEAD_DIM
N_PAIRS = ATT_WIDTH // LANES
ATT_RES = max(d for _, d in DILATIONS)
ATT_UNIT = BLOCK * ATT_RES
ATT_GROUP = (8, 8, 8)

NN = (((1,), (0,)), ((), ()))
NT = (((1,), (1,)), ((), ()))


def _cparams(sem):
    return pltpu.CompilerParams(dimension_semantics=sem, vmem_limit_bytes=VMEM_LIMIT_BYTES)


def _split_bf16(x, n):
    if x.dtype == BF16:
        return [x]
    pieces = []
    r = x
    for i in range(n):
        p = r.astype(BF16)
        pieces.append(p)
        if i + 1 < n:
            r = r - p.astype(F32)
    return pieces


def _mmp(ap, bp, dims=NN):
    order = max(len(ap), len(bp))
    out = None
    for i, x in enumerate(ap):
        for j, y in enumerate(bp):
            if i + j < order:
                t = lax.dot_general(x, y, dims, preferred_element_type=F32)
                out = t if out is None else out + t
    return out


def _mm(a, b, dims=NN, pa=2, pb=2):
    return _mmp(_split_bf16(a, pa), _split_bf16(b, pb), dims)


def _dot1(a, b, dims=NN):
    return lax.dot_general(a.astype(BF16), b.astype(BF16), dims, preferred_element_type=F32)


def _lockstep(gens):
    results = [None] * len(gens)
    while any(r is None for r in results):
        results = [next(g) if r is None else r for g, r in zip(gens, results)]
    return results


def _sigmoid(x):
    return 1.0 / (1.0 + jnp.exp(-x))


def _silu(x):
    return x * _sigmoid(x)


def _rel_bucket_np(dist):
    d = np.maximum(dist, 1).astype(np.float32)
    large = REL_EXACT + (np.log(d / np.float32(REL_EXACT)) / np.float32(math.log(REL_MAX_DIST / REL_EXACT))
                         * np.float32(N_REL_BUCKETS - REL_EXACT)).astype(np.int32)
    large = np.minimum(large, N_REL_BUCKETS - 1)
    return np.where(dist < REL_EXACT, dist, large)


def _bias_rows(rel_bias, dist):
    idx = _rel_bucket_np(np.asarray(dist, np.int32))
    onehot = jnp.asarray(np.eye(N_REL_BUCKETS, dtype=np.float32)[idx.reshape(-1)])
    rows = jnp.dot(onehot, rel_bias, precision=lax.Precision.HIGHEST)
    return rows.reshape(idx.shape + (rel_bias.shape[1],))


def _ada_kernel(c_ref, w_ref, b_ref, o_ref):
    a = _silu(c_ref[...])
    o_ref[...] = _dot1(a, w_ref[...]) + b_ref[...]


def _ada_modulation(c_all, w_ada, b_ada):
    depth = w_ada.shape[0]
    rows = c_all.shape[0]
    tn = D_MODEL
    return pl.pallas_call(
        _ada_kernel,
        out_shape=jax.ShapeDtypeStruct((depth, rows, 6 * D_MODEL), F32),
        grid=(depth, 6 * D_MODEL // tn),
        in_specs=[
            pl.BlockSpec((rows, D_MODEL), lambda l, j: (0, 0)),
            pl.BlockSpec((None, D_MODEL, tn), lambda l, j: (l, 0, j)),
            pl.BlockSpec((None, 1, tn), lambda l, j: (l, 0, j)),
        ],
        out_specs=pl.BlockSpec((None, rows, tn), lambda l, j: (l, 0, j)),
        compiler_params=_cparams(("arbitrary", "arbitrary")),
        name="ada_modulation",
    )(c_all, w_ada, b_ada.reshape(depth, 1, 6 * D_MODEL))


def _in_proj_kernel(x_ref, sc_ref, sh_ref, g_ref, w_ref, *refs, residue_major):
    if residue_major:
        perm_ref, uc_ref, ur_ref, q_ref, k_ref, v_ref = refs
    else:
        uc_ref, ur_ref, q_ref, k_ref, v_ref = refs
    x = x_ref[...]
    ms = jnp.mean(x * x, axis=-1, keepdims=True)
    h = x * lax.rsqrt(ms + RMS_EPS) * g_ref[...]
    h = h * (1.0 + sc_ref[...]) + sh_ref[...]
    hb = h.astype(BF16)
    uc_ref[...] = lax.dot_general(hb, w_ref[:, 0:CONV_COLS], NN, preferred_element_type=F32)
    o = CONV_COLS
    ur_ref[...] = lax.dot_general(hb, w_ref[:, o:o + RWKV_COLS], NN, preferred_element_type=F32)
    o += RWKV_COLS
    if residue_major:
        hb = lax.dot_general(perm_ref[...], hb, NN, preferred_element_type=F32).astype(BF16)
        run = hb.shape[0] // ATT_RES
    for ref in (q_ref, k_ref, v_ref):
        res = lax.dot_general(hb, w_ref[:, o:o + ATT_WIDTH], NN, preferred_element_type=F32)
        for p in range(N_PAIRS):
            cols = res[:, p * LANES:(p + 1) * LANES]
            if residue_major:
                for r in range(ATT_RES):
                    ref[p, r] = cols[r * run:(r + 1) * run]
            else:
                ref[p] = cols
        o += ATT_WIDTH


def _tile_perm(tm):
    run = tm // ATT_RES
    t = np.arange(tm)
    perm = np.zeros((tm, tm), np.float32)
    perm[(t % ATT_RES) * run + t // ATT_RES, t] = 1.0
    return jnp.asarray(perm, BF16), jnp.asarray(perm.T, BF16)


def _mod_spec(mod, tiles_per_group):
    _, r, d = mod.shape
    return pl.BlockSpec((None, r, d), lambda i: (i // tiles_per_group, 0, 0))


def _in_proj(x2, scale, shift, g, w_bf, tm, tiles_per_group, seq_shape=None):
    m = x2.shape[0]
    flat = lambda w: jax.ShapeDtypeStruct((m, w), F32)
    residue_major = seq_shape is not None
    ins = [x2, scale, shift, g.reshape(1, D_MODEL), w_bf]
    in_specs = [
        pl.BlockSpec((tm, D_MODEL), lambda i: (i, 0)),
        _mod_spec(scale, tiles_per_group),
        _mod_spec(shift, tiles_per_group),
        pl.BlockSpec((1, D_MODEL), lambda i: (0, 0)),
        pl.BlockSpec((D_MODEL, IN_COLS), lambda i: (0, 0)),
    ]
    if residue_major:
        n, t = seq_shape
        per_unit = ATT_UNIT // tm
        per_seq = t // tm
        rows_res = ATT_UNIT // ATT_RES
        pairs = jax.ShapeDtypeStruct((N_PAIRS, n, t // ATT_UNIT, ATT_RES, rows_res, LANES), F32)
        pair_spec = pl.BlockSpec((N_PAIRS, None, None, ATT_RES, tm // ATT_RES, LANES),
                                 lambda i: (0, i // per_seq, (i % per_seq) // per_unit, 0, i % per_unit, 0))
        perm, _ = _tile_perm(tm)
        ins.append(perm)
        in_specs.append(pl.BlockSpec((tm, tm), lambda i: (0, 0)))
    else:
        pairs = jax.ShapeDtypeStruct((N_PAIRS, m, LANES), F32)
        pair_spec = pl.BlockSpec((N_PAIRS, tm, LANES), lambda i: (0, i, 0))
    return pl.pallas_call(
        functools.partial(_in_proj_kernel, residue_major=residue_major),
        out_shape=(flat(CONV_COLS), flat(RWKV_COLS), pairs, pairs, pairs),
        grid=(m // tm,),
        in_specs=in_specs,
        out_specs=(pl.BlockSpec((tm, CONV_COLS), lambda i: (i, 0)), pl.BlockSpec((tm, RWKV_COLS), lambda i: (i, 0)),
                   pair_spec, pair_spec, pair_spec),
        compiler_params=_cparams(("arbitrary",)),
        name="in_proj",
    )(*ins)


def _pairs_to_rows(x):
    return jnp.transpose(x, (1, 0, 2)).reshape(x.shape[1], ATT_WIDTH)


def _rows_to_pairs(x):
    return jnp.transpose(x.reshape(x.shape[0], N_PAIRS, LANES), (1, 0, 2))


def _conv_core(zs_ref, zsh_ref, w_ref, b_ref, lg_ref, lb_ref, y_ref, tt):
    n_rows = CONV_HALO + tt
    zs_ref[n_rows:n_rows + SUBLANES, :] = jnp.zeros((SUBLANES, CONV_WIDTH), F32)
    for s in range(SUBLANES):
        zsh_ref[s] = zs_ref[pl.ds(s, n_rows), :]
    off = CONV_HALO - (CONV_K - 1)
    sub = min(tt, 64)
    for r0 in range(0, tt, sub):
        acc = jnp.zeros((sub, CONV_WIDTH), F32)
        for j in range(CONV_K):
            a, b = divmod(off + j, SUBLANES)
            acc = acc + w_ref[j:j + 1, :] * zsh_ref[b, pl.ds(r0 + SUBLANES * a, sub), :]
        y = acc + b_ref[...]
        mu = jnp.mean(y, axis=-1, keepdims=True)
        yc = y - mu
        var = jnp.mean(yc * yc, axis=-1, keepdims=True)
        yn = yc * lax.rsqrt(var + LN_EPS) * lg_ref[...] + lb_ref[...]
        y_ref[r0:r0 + sub, :] = _silu(yn)


def _glu(u):
    return u[:, :CONV_WIDTH] * _sigmoid(u[:, CONV_WIDTH:])


def _conv_prompt_kernel(u_ref, uh_ref, w_ref, b_ref, lg_ref, lb_ref, y_ref, zt_ref, zs_ref, zsh_ref, *, tt):
    j = pl.program_id(1)
    zh = _glu(uh_ref[...])
    zs_ref[0:CONV_HALO, :] = jnp.where(j > 0, zh, 0.0)
    zs_ref[CONV_HALO:CONV_HALO + tt, :] = _glu(u_ref[...])
    _conv_core(zs_ref, zsh_ref, w_ref, b_ref, lg_ref, lb_ref, y_ref, tt)
    zt_ref[...] = zs_ref[tt:tt + CONV_HALO, :]


def _conv_sample_kernel(u_ref, zh_ref, w_ref, b_ref, lg_ref, lb_ref, y_ref, z_ref, zs_ref, zsh_ref, *, tt):
    z = _glu(u_ref[...])
    zs_ref[0:CONV_HALO, :] = zh_ref[...]
    zs_ref[CONV_HALO:CONV_HALO + tt, :] = z
    _conv_core(zs_ref, zsh_ref, w_ref, b_ref, lg_ref, lb_ref, y_ref, tt)
    z_ref[...] = z


def _conv_param_specs(nargs_grid):
    cmap = (lambda n, j: (0, 0)) if nargs_grid == 2 else (lambda n: (0, 0))
    return [
        pl.BlockSpec((CONV_HALO, CONV_WIDTH), cmap),
        pl.BlockSpec((1, CONV_WIDTH), cmap),
        pl.BlockSpec((1, CONV_WIDTH), cmap),
        pl.BlockSpec((1, CONV_WIDTH), cmap),
    ]


def _conv_params(conv_w, conv_b, ln_g, ln_b):
    w = jnp.pad(conv_w, ((0, CONV_HALO - CONV_K), (0, 0)))
    return w, conv_b.reshape(1, -1), ln_g.reshape(1, -1), ln_b.reshape(1, -1)


def _conv_prompt(uc, n, t, params, tt=512):
    u3 = uc.reshape(n, t, CONV_COLS)
    hb = tt // CONV_HALO
    y, zt = pl.pallas_call(
        functools.partial(_conv_prompt_kernel, tt=tt),
        out_shape=(jax.ShapeDtypeStruct((n, t, CONV_WIDTH), F32),
                   jax.ShapeDtypeStruct((n, CONV_HALO, CONV_WIDTH), F32)),
        grid=(n, t // tt),
        in_specs=[
            pl.BlockSpec((None, tt, CONV_COLS), lambda b, j: (b, j, 0)),
            pl.BlockSpec((None, CONV_HALO, CONV_COLS), lambda b, j: (b, jnp.maximum(j * hb - 1, 0), 0)),
        ] + _conv_param_specs(2),
        out_specs=(pl.BlockSpec((None, tt, CONV_WIDTH), lambda b, j: (b, j, 0)),
                   pl.BlockSpec((None, CONV_HALO, CONV_WIDTH), lambda b, j: (b, 0, 0))),
        scratch_shapes=[pltpu.VMEM((CONV_HALO + tt + SUBLANES, CONV_WIDTH), F32),
                        pltpu.VMEM((SUBLANES, CONV_HALO + tt, CONV_WIDTH), F32)],
        compiler_params=_cparams(("arbitrary", "arbitrary")),
        name="conv_prompt",
    )(u3, u3, *params)
    return y.reshape(n * t, CONV_WIDTH), zt[:, CONV_HALO - (CONV_K - 1):]


def _conv_sample(uc, cache, n, t, params):
    tt = SUBLANES
    u3 = jnp.pad(uc.reshape(n, t, CONV_COLS), ((0, 0), (0, tt - t), (0, 0)))
    zh = jnp.pad(cache, ((0, 0), (CONV_HALO - (CONV_K - 1), 0), (0, 0)))
    y, z = pl.pallas_call(
        functools.partial(_conv_sample_kernel, tt=tt),
        out_shape=(jax.ShapeDtypeStruct((n, tt, CONV_WIDTH), F32),
                   jax.ShapeDtypeStruct((n, tt, CONV_WIDTH), F32)),
        grid=(n,),
        in_specs=[
            pl.BlockSpec((None, tt, CONV_COLS), lambda b: (b, 0, 0)),
            pl.BlockSpec((None, CONV_HALO, CONV_WIDTH), lambda b: (b, 0, 0)),
        ] + _conv_param_specs(1),
        out_specs=(pl.BlockSpec((None, tt, CONV_WIDTH), lambda b: (b, 0, 0)),
                   pl.BlockSpec((None, tt, CONV_WIDTH), lambda b: (b, 0, 0))),
        scratch_shapes=[pltpu.VMEM((CONV_HALO + tt + SUBLANES, CONV_WIDTH), F32),
                        pltpu.VMEM((SUBLANES, CONV_HALO + tt, CONV_WIDTH), F32)],
        compiler_params=_cparams(("arbitrary",)),
        name="conv_sample",
    )(u3, zh, *params)
    new_cache = jnp.concatenate([cache[:, t:], z[:, :t]], axis=1)
    return y[:, :t].reshape(n * t, CONV_WIDTH), new_cache


def _rwkv_consts():
    c = RWKV_CHUNK
    rows = np.arange(c)
    tri = (rows[None, :] <= rows[:, None]).astype(np.float32)
    lanes = np.arange(RWKV_WIDTH)
    bd = (lanes[:, None] // HEAD_DIM == lanes[None, :] // HEAD_DIM).astype(np.float32)
    return jnp.asarray(tri, BF16), jnp.asarray(bd, BF16)


def _rwkv_kernel(u_ref, up_ref, sh0_ref, s0_ref, mu_ref, w0_ref, wwa_ref, a0_ref, g2_ref, kk_ref, ka_ref, rk_ref,
                 lg_ref, lb_ref, tri_ref, bd_ref,
                 y_ref, so_ref,
                 us_ref, lw_ref, kn_ref, kb_ref, k2_ref, rr_ref, vv_ref, bon_ref, gate_ref, yy_ref, s_ref,
                 *, tr, t_valid, nseq):
    j = pl.program_id(1)
    c = RWKV_CHUNK
    bd = bd_ref[...]
    o = 3 * RWKV_WIDTH

    @pl.when(j == 0)
    def _():
        s_ref[...] = s0_ref[...]

    for sq in range(nseq):
        u = u_ref[sq]
        prev = jnp.where(j > 0, up_ref[sq], jnp.broadcast_to(sh0_ref[sq], (SUBLANES, RWKV_COLS)))
        us_ref[sq, 0:SUBLANES, :] = prev
        us_ref[sq, SUBLANES:SUBLANES + tr, :] = u
        u_prev = us_ref[sq, pl.ds(SUBLANES - 1, tr), :]
        xs = u + (u_prev - u) * mu_ref[...]
        r = xs[:, :RWKV_WIDTH]
        k = xs[:, RWKV_WIDTH:2 * RWKV_WIDTH]
        v = xs[:, 2 * RWKV_WIDTH:o]
        lo = xs[:, o:o + DECAY_LORA + AAA_LORA]
        lane = lax.broadcasted_iota(jnp.int32, lo.shape, 1)
        lo = jnp.where(lane < DECAY_LORA, jnp.tanh(lo), lo)
        wa = _dot1(lo, wwa_ref[...])
        g = _dot1(_sigmoid(xs[:, o + DECAY_LORA + AAA_LORA:]), g2_ref[...])
        lw = -DECAY_SCALE * _sigmoid(w0_ref[...] + wa[:, :RWKV_WIDTH])
        a = _sigmoid(a0_ref[...] + wa[:, RWKV_WIDTH:])
        kkr = k * kk_ref[...]
        ss = _mm(kkr * kkr, bd, pa=2)
        kk = kkr / jnp.maximum(jnp.sqrt(ss), 1e-12)
        k2 = k * (1.0 + (a - 1.0) * ka_ref[...])
        if t_valid < tr:
            valid = lax.broadcasted_iota(jnp.int32, (tr, RWKV_WIDTH), 0) < t_valid
            zero = jnp.zeros_like(r)
            r, k2, v, kk, lw = (jnp.where(valid, z, zero) for z in (r, k2, v, kk, lw))
        lw_ref[sq] = lw
        kn_ref[sq] = -kk
        kb_ref[sq] = kk * a
        k2_ref[sq] = k2
        rr_ref[sq] = r
        vv_ref[sq] = v
        bon_ref[sq] = _mm(r * k2 * rk_ref[...], bd, pa=2) * v
        gate_ref[sq] = g

    row = lax.broadcasted_iota(jnp.int32, (c, LANES), 0)
    col = lax.broadcasted_iota(jnp.int32, (c, LANES), 1) & (HEAD_DIM - 1)
    strict = col < row
    incl = col <= row
    ident = jnp.where(col == row, 1.0, 0.0)
    bd2 = bd[0:LANES, 0:LANES]
    bd2_mask = bd2 > 0.5
    n_steps = int(round(math.log2(c)))

    def dot(a, b, dims=NN):
        return lax.dot_general(a.astype(BF16), b, dims, preferred_element_type=F32)

    def stack(x):
        xb = x.astype(BF16)
        return jnp.concatenate([xb, xb], axis=0) * bd2

    def rows2(a, b):
        return jnp.concatenate([a, b], axis=0)

    def chunk_pair(pt, bt, kt, rt, vc, gc, s0):
        zero = jnp.zeros((c, LANES), F32)
        pr = rows2(pt, rt)
        xb = dot(pr, stack(bt), NT)
        xk = dot(pr, stack(kt), NT)
        yield None
        lm = jnp.where(strict, xb[:c], zero)
        mm_ = jnp.where(strict, xk[:c], zero)
        qb = jnp.where(incl, xb[c:], zero)
        qk = jnp.where(incl, xk[c:], zero)
        tinv = ident + lm
        lk = dot(lm, stack(lm))
        mq = dot(rows2(mm_, qk), stack(vc))
        yield None
        for step in range(1, n_steps):
            if step + 1 < n_steps:
                both = dot(rows2(lk, tinv), stack(lk))
                lk = both[:c]
                tinv = tinv + both[c:]
            else:
                tinv = tinv + dot(tinv, stack(lk))
            yield None
        p2 = dot(tinv, stack(pt))
        u0 = dot(tinv, stack(mq[:c]))
        yield None
        xs_ = dot(rows2(p2, rt), stack(s0), NT)
        yield None
        uu = xs_[:c] + u0
        y = xs_[c:] + dot(qb, stack(uu)) + mq[c:]
        uvt = rows2(uu, vc).T
        bk = rows2(bt * gc, kt * gc).astype(BF16)
        z = jnp.where(bd2_mask, dot(uvt, bk), 0.0)
        yield y, s0 * gc + z[0:c] + z[c:2 * c]

    def chunk(ci, carry):
        sl = pl.ds(pl.multiple_of(ci * c, c), c)
        work = []
        for sq in range(nseq):
            lw_c = lw_ref[sq, sl, :]
            gi = _mmp([tri_ref[...]], _split_bf16(lw_c, 3))
            e_neg = jnp.exp(-gi)
            gc = jnp.exp(gi[c - 1:c, :])
            pt = kn_ref[sq, sl, :] * jnp.exp(gi - lw_c)
            bt = kb_ref[sq, sl, :] * e_neg
            kt = k2_ref[sq, sl, :] * e_neg
            rt = rr_ref[sq, sl, :] * jnp.exp(gi)
            vc = vv_ref[sq, sl, :]
            s0 = s_ref[sq]
            for p in range(RWKV_WIDTH // LANES):
                ln = slice(p * LANES, (p + 1) * LANES)
                work.append((sq, ln, tuple(x[:, ln] for x in (pt, bt, kt, rt, vc, gc, s0))))
        results = _lockstep([chunk_pair(*args) for _, _, args in work])
        done = [(sq, ln, res) for (sq, ln, _), res in zip(work, results)]
        for sq, ln, (y, s_new) in done:
            yy_ref[sq, sl, ln] = y
            s_ref[sq, :, ln] = s_new
        return carry

    lax.fori_loop(0, tr // c, chunk, 0)

    inv = 1.0 / HEAD_DIM
    for sq in range(nseq):
        y = yy_ref[sq]
        mu = _mm(y, bd, pa=2) * inv
        yc = y - mu
        var = _mm(yc * yc, bd, pa=2) * inv
        yn = yc * lax.rsqrt(var + RWKV_GN_EPS) * lg_ref[...] + lb_ref[...]
        y_ref[sq] = (yn + bon_ref[sq]) * gate_ref[sq]
    so_ref[...] = s_ref[...]


def _rwkv_params(p):
    z = jnp.zeros((DECAY_LORA, RWKV_WIDTH), F32)
    wwa = jnp.concatenate([jnp.concatenate([p['rwkv_w2'], z], axis=1),
                           jnp.concatenate([z, p['rwkv_a2']], axis=1)], axis=0)
    row = lambda x: x.reshape(1, -1)
    return (row(p['rwkv_mu']), row(p['rwkv_w0']), wwa, row(p['rwkv_a0']), p['rwkv_g2'], row(p['rwkv_k_k']),
            row(p['rwkv_k_a']), row(p['rwkv_r_k']), row(p['rwkv_ln_g']), row(p['rwkv_ln_b']))


def _rwkv(u3, shift0, s0, params, tr, t_valid, nseq):
    n, t, _ = u3.shape
    assert n % nseq == 0
    consts = _rwkv_consts()
    hb = tr // SUBLANES
    const2 = lambda b, j: (0, 0)
    full = lambda arr: pl.BlockSpec(arr.shape, const2)
    tile = lambda: pltpu.VMEM((nseq, tr, RWKV_WIDTH), F32)
    y, s_out = pl.pallas_call(
        functools.partial(_rwkv_kernel, tr=tr, t_valid=t_valid, nseq=nseq),
        out_shape=(jax.ShapeDtypeStruct((n, t, RWKV_WIDTH), F32),
                   jax.ShapeDtypeStruct((n, HEAD_DIM, RWKV_WIDTH), F32)),
        grid=(n // nseq, t // tr),
        in_specs=[
            pl.BlockSpec((nseq, tr, RWKV_COLS), lambda b, j: (b, j, 0)),
            pl.BlockSpec((nseq, SUBLANES, RWKV_COLS), lambda b, j: (b, jnp.maximum(j * hb - 1, 0), 0)),
            pl.BlockSpec((nseq, 1, RWKV_COLS), lambda b, j: (b, 0, 0)),
            pl.BlockSpec((nseq, HEAD_DIM, RWKV_WIDTH), lambda b, j: (b, 0, 0)),
        ] + [full(x) for x in params] + [full(x) for x in consts],
        out_specs=(pl.BlockSpec((nseq, tr, RWKV_WIDTH), lambda b, j: (b, j, 0)),
                   pl.BlockSpec((nseq, HEAD_DIM, RWKV_WIDTH), lambda b, j: (b, 0, 0))),
        scratch_shapes=[pltpu.VMEM((nseq, SUBLANES + tr, RWKV_COLS), F32)] + [tile() for _ in range(9)]
                       + [pltpu.VMEM((nseq, HEAD_DIM, RWKV_WIDTH), F32)],
        compiler_params=_cparams(("arbitrary", "arbitrary")),
        name="rwkv7",
    )(u3, u3, shift0, s0, *params, *consts)
    return y, s_out


def _state_to_lanes(s):
    n = s.shape[0]
    return jnp.transpose(s, (0, 2, 1, 3)).reshape(n, HEAD_DIM, RWKV_WIDTH)


def _lanes_to_state(s):
    n = s.shape[0]
    return jnp.transpose(s.reshape(n, HEAD_DIM, N_RWKV_HEADS, HEAD_DIM), (0, 2, 1, 3))


def _att_natural_index(dil):
    runs = ATT_RES // dil
    p = np.arange(BLOCK)
    return runs * (p % (BLOCK // runs)) + p // (BLOCK // runs)


def _attn_prompt_kernel(q_ref, kc_ref, kp_ref, vc_ref, vp_ref, bias_ref, o_ref, kcat, vcat, acc, m0, m1, l0, l1):
    u = pl.program_id(2)
    unit = ATT_UNIT
    scale = HEAD_DIM ** -0.5
    kcat[0:unit, :] = kp_ref[...]
    kcat[unit:2 * unit, :] = kc_ref[...]
    vcat[0:unit, :] = vp_ref[...]
    vcat[unit:2 * unit, :] = vc_ref[...]
    col = lax.broadcasted_iota(jnp.int32, (2 * BLOCK, 2 * BLOCK), 1)
    lo = lax.broadcasted_iota(jnp.int32, (BLOCK, LANES), 1) < HEAD_DIM
    n_br = len(DILATIONS)
    rows2 = lambda a, b: jnp.concatenate([a, b], axis=0)
    for bi, (window, dil) in enumerate(DILATIONS):
        first = bi == 0
        last = bi == n_br - 1
        shift = int(round(math.log2(dil)))
        n_runs = ATT_RES // dil
        run_len = BLOCK // n_runs
        n_blk = unit // (BLOCK * dil)

        def runs_of(r, g):
            base = (g // n_blk) * unit + (g % n_blk) * run_len
            return [base + (dil * m + r) * BLOCK for m in range(n_runs)]

        def gather(ref, starts):
            return jnp.concatenate([ref[pl.ds(pl.multiple_of(s0, SUBLANES), run_len), :] for s0 in starts], axis=0)

        def sub_block(j, bi, dil, shift, first, last):
            r = j & (dil - 1)
            blk = j >> shift
            q_rows = runs_of(r, blk)
            k_rows = runs_of(r, n_blk + blk - 1) + runs_of(r, n_blk + blk)
            q = gather(q_ref, q_rows) * scale
            kk = gather(kcat, k_rows).astype(BF16)
            vv = gather(vcat, k_rows).astype(BF16)
            q2 = rows2(jnp.where(lo, q, 0.0), jnp.where(lo, 0.0, q)).astype(BF16)
            s = lax.dot_general(q2, kk, NT, preferred_element_type=F32)
            if not first:
                m_old = rows2(gather(m0, q_rows), gather(m1, q_rows))
                l_old = rows2(gather(l0, q_rows), gather(l1, q_rows))
                acc_old = gather(acc, q_rows)
            yield None
            s = s + bias_ref[bi]
            s = jnp.where((col < BLOCK) & (u == 0) & (blk == 0), NEG_INF, s)
            m_cur = jnp.max(s, axis=-1, keepdims=True)
            if first:
                m_new = jnp.broadcast_to(m_cur, (2 * BLOCK, LANES))
            else:
                m_new = jnp.maximum(m_old, m_cur)
                alpha = jnp.exp(m_old - m_new)
            p = jnp.exp(s - jnp.concatenate([m_new, m_new], axis=1))
            l_new = jnp.sum(p, axis=-1, keepdims=True)
            if first:
                l_new = jnp.broadcast_to(l_new, (2 * BLOCK, LANES))
            else:
                l_new = alpha * l_old + l_new
            o2 = lax.dot_general(p.astype(BF16), vv, NN, preferred_element_type=F32)
            yield None
            o_pair = jnp.where(lo, o2[:BLOCK], o2[BLOCK:])
            if not first:
                o_pair = o_pair + jnp.where(lo, alpha[:BLOCK], alpha[BLOCK:]) * acc_old
            if last:
                linv = 1.0 / l_new
                yield [(o_ref, q_rows, o_pair * jnp.where(lo, linv[:BLOCK], linv[BLOCK:]))]
            else:
                yield [(m0, q_rows, m_new[:BLOCK]), (m1, q_rows, m_new[BLOCK:]),
                       (l0, q_rows, l_new[:BLOCK]), (l1, q_rows, l_new[BLOCK:]), (acc, q_rows, o_pair)]

        group = ATT_GROUP[bi]

        def body(jg, carry, bi=bi, dil=dil, shift=shift, first=first, last=last, group=group, run_len=run_len):
            groups = _lockstep([sub_block(jg * group + g, bi, dil, shift, first, last) for g in range(group)])
            for stores in groups:
                for ref, starts, val in stores:
                    for m, s0 in enumerate(starts):
                        ref[pl.ds(pl.multiple_of(s0, SUBLANES), run_len), :] = val[m * run_len:(m + 1) * run_len]
            return carry

        lax.fori_loop(0, unit // BLOCK // group, body, 0)


def _attn_prompt_bias(rel_bias):
    tabs = []
    for _, dil in DILATIONS:
        a = _att_natural_index(dil)
        c = np.concatenate([a, BLOCK + a])
        delta = a[:, None] + BLOCK - c[None, :]
        band = (delta >= 0) & (delta <= BLOCK)
        b = _bias_rows(rel_bias, np.clip(delta, 0, BLOCK) * dil)
        b = jnp.where(jnp.asarray(band)[:, :, None], b, NEG_INF)
        tabs.append(jnp.transpose(b, (2, 0, 1)))
    w = jnp.stack(tabs)
    w = w.reshape(len(DILATIONS), N_PAIRS, 2 * BLOCK, 2 * BLOCK)
    return jnp.transpose(w, (1, 0, 2, 3))


def _attn_prompt(q, k, v, btab):
    unit = ATT_UNIT
    n_pairs, n, units = q.shape[:3]
    r5 = lambda x: x.reshape(n_pairs, n, units, unit, LANES)
    cur = pl.BlockSpec((None, None, None, unit, LANES), lambda a, p, u: (p, a, u, 0, 0))
    prv = pl.BlockSpec((None, None, None, unit, LANES), lambda a, p, u: (p, a, jnp.maximum(u - 1, 0), 0, 0))
    stat = lambda: pltpu.VMEM((unit, LANES), F32)
    out = pl.pallas_call(
        _attn_prompt_kernel,
        out_shape=jax.ShapeDtypeStruct((n_pairs, n, units, unit, LANES), F32),
        grid=(n, n_pairs, units),
        in_specs=[cur, cur, prv, cur, prv,
                  pl.BlockSpec((None, len(DILATIONS), 2 * BLOCK, 2 * BLOCK), lambda a, p, u: (p, 0, 0, 0))],
        out_specs=cur,
        scratch_shapes=[pltpu.VMEM((2 * unit, LANES), F32), pltpu.VMEM((2 * unit, LANES), F32),
                        stat(), stat(), stat(), stat(), stat()],
        compiler_params=_cparams(("arbitrary", "arbitrary", "arbitrary")),
        name="attn_prompt",
    )(r5(q), r5(k), r5(k), r5(v), r5(v), btab)
    return out.reshape(q.shape)


def _attn_sample_kernel(q_ref, kn_ref, vn_ref, kc_ref, vc_ref, bt_ref, bn_ref, cnt_ref, cn_ref, hm_ref, o_ref,
                        kn_s, vn_s, *, s_new):
    scale = HEAD_DIM ** -0.5
    hm = hm_ref[...]
    qm = jnp.concatenate([(q_ref[s:s + 1, :] * scale) * hm for s in range(s_new)], axis=0).astype(BF16)
    kn_s[...] = jnp.zeros_like(kn_s)
    vn_s[...] = jnp.zeros_like(vn_s)
    kn_s[0:s_new, :] = kn_ref[...]
    vn_s[0:s_new, :] = vn_ref[...]
    sc = lax.dot_general(qm, kc_ref[...].astype(BF16), NN, preferred_element_type=F32) + bt_ref[...]
    sn = lax.dot_general(qm, kn_s[...].astype(BF16), NT, preferred_element_type=F32) + bn_ref[...]
    cnt = cnt_ref[...]
    cn = cn_ref[...]
    sc = jnp.where(cnt > 0.0, sc, NEG_INF)
    sn = jnp.where(cn > 0.0, sn, NEG_INF)
    m = jnp.maximum(jnp.max(sc, axis=-1, keepdims=True), jnp.max(sn, axis=-1, keepdims=True))
    p = cnt * jnp.exp(sc - m)
    pn = cn * jnp.exp(sn - m)
    l = jnp.sum(p, axis=-1, keepdims=True) + jnp.sum(pn, axis=-1, keepdims=True)
    o = (lax.dot_general(p.astype(BF16), vc_ref[...].astype(BF16), NT, preferred_element_type=F32)
         + lax.dot_general(pn.astype(BF16), vn_s[...].astype(BF16), NN, preferred_element_type=F32))
    o = o / l
    for s in range(s_new):
        o_ref[s:s + 1, :] = jnp.sum(o[s * N_ATT_HEADS:(s + 1) * N_ATT_HEADS] * hm, axis=0, keepdims=True)


def _attn_sample_tables(rel_bias, s_new, win):
    s = np.arange(s_new)[:, None]
    dist_c = win + s - np.arange(win)[None, :]
    dist_n = s - np.arange(LANES)[None, :]
    cnt_c = np.zeros(dist_c.shape, np.float32)
    cnt_n = np.zeros(dist_n.shape, np.float32)
    for window, dil in DILATIONS:
        cnt_c += ((dist_c % dil == 0) & (dist_c <= window)).astype(np.float32)
        cnt_n += ((dist_n >= 0) & (dist_n % dil == 0) & (dist_n <= window)).astype(np.float32)
    rep = lambda x: jnp.asarray(np.repeat(x, N_ATT_HEADS, axis=0))
    rows = lambda b: jnp.transpose(b, (0, 2, 1)).reshape(s_new * N_ATT_HEADS, -1)
    return (rows(_bias_rows(rel_bias, dist_c)), rows(_bias_rows(rel_bias, np.maximum(dist_n, 0))),
            rep(cnt_c), rep(cnt_n))


def _attn_sample(q, k, v, k_cache, v_cache, li, tables, n, s_new):
    win = k_cache.shape[-1]
    assert win == MAX_WINDOW
    lanes = np.arange(ATT_WIDTH)
    hm = jnp.asarray((np.arange(N_ATT_HEADS)[:, None] == lanes[None, :] // HEAD_DIM).astype(np.float32))
    new = pl.BlockSpec((None, s_new, ATT_WIDTH), lambda b: (b, 0, 0))
    buf = pl.BlockSpec((None, None, ATT_WIDTH, win), lambda b: (li, b, 0, 0))
    full = lambda arr: pl.BlockSpec(arr.shape, lambda b: (0, 0))
    r3 = lambda x: x.reshape(n, s_new, ATT_WIDTH)
    out = pl.pallas_call(
        functools.partial(_attn_sample_kernel, s_new=s_new),
        out_shape=jax.ShapeDtypeStruct((n, s_new, ATT_WIDTH), F32),
        grid=(n,),
        in_specs=[new, new, new, buf, buf] + [full(x) for x in tables] + [full(hm)],
        out_specs=new,
        scratch_shapes=[pltpu.VMEM((LANES, ATT_WIDTH), F32), pltpu.VMEM((LANES, ATT_WIDTH), F32)],
        compiler_params=_cparams(("arbitrary",)),
        name="attn_sample",
    )(r3(q), r3(k), r3(v), k_cache, v_cache, *tables, hm)
    return out.reshape(n * s_new, ATT_WIDTH)


def _out_proj_kernel(ya_ref, yb_ref, yc_ref, x_ref, gate_ref, g_ref, w_ref, *refs, residue_major):
    a = CONV_WIDTH
    b = CONV_WIDTH + RWKV_WIDTH
    if residue_major:
        permt_ref, o_ref = refs
        yc = jnp.concatenate([jnp.concatenate([yc_ref[p, r] for r in range(ATT_RES)], axis=0)
                              for p in range(N_PAIRS)], axis=1).astype(BF16)
        yc = lax.dot_general(permt_ref[...], yc, NN, preferred_element_type=F32)
    else:
        o_ref, = refs
        yc = jnp.concatenate([yc_ref[p] for p in range(N_PAIRS)], axis=1)
    mix = _dot1(ya_ref[...], w_ref[0:a, :]) + _dot1(yb_ref[...], w_ref[a:b, :]) + _dot1(yc, w_ref[b:, :])
    ms = jnp.mean(mix * mix, axis=-1, keepdims=True)
    o_ref[...] = x_ref[...] + gate_ref[...] * (mix * lax.rsqrt(ms + RMS_EPS) * g_ref[...])


def _out_proj(ya, yb, yc, x2, gate, g, w_bf, tm, tiles_per_group):
    m = x2.shape[0]
    rows = lambda w: pl.BlockSpec((tm, w), lambda i: (i, 0))
    residue_major = yc.ndim == 6
    extra, extra_specs = [], []
    if residue_major:
        t = yc.shape[2] * ATT_UNIT
        per_unit = ATT_UNIT // tm
        per_seq = t // tm
        yc_spec = pl.BlockSpec((N_PAIRS, None, None, ATT_RES, tm // ATT_RES, LANES),
                               lambda i: (0, i // per_seq, (i % per_seq) // per_unit, 0, i % per_unit, 0))
        _, permt = _tile_perm(tm)
        extra, extra_specs = [permt], [pl.BlockSpec((tm, tm), lambda i: (0, 0))]
    else:
        yc_spec = pl.BlockSpec((N_PAIRS, tm, LANES), lambda i: (0, i, 0))
    return pl.pallas_call(
        functools.partial(_out_proj_kernel, residue_major=residue_major),
        out_shape=jax.ShapeDtypeStruct((m, D_MODEL), F32),
        grid=(m // tm,),
        in_specs=[rows(CONV_WIDTH), rows(RWKV_WIDTH), yc_spec, rows(D_MODEL),
                  _mod_spec(gate, tiles_per_group),
                  pl.BlockSpec((1, D_MODEL), lambda i: (0, 0)),
                  pl.BlockSpec((D_MODEL, D_MODEL), lambda i: (0, 0))] + extra_specs,
        out_specs=rows(D_MODEL),
        compiler_params=_cparams(("arbitrary",)),
        name="out_proj",
    )(ya, yb, yc, x2, gate, g.reshape(1, D_MODEL), w_bf, *extra)


def _ffn_kernel(x_ref, sc_ref, sh_ref, gate_ref, gpre_ref, gpost_ref, wg_ref, wu_ref, wo_ref, o_ref):
    x = x_ref[...]
    ms = jnp.mean(x * x, axis=-1, keepdims=True)
    h = x * lax.rsqrt(ms + RMS_EPS) * gpre_ref[...]
    hb = (h * (1.0 + sc_ref[...]) + sh_ref[...]).astype(BF16)
    y = None
    for c in range(FFN_SPLIT):
        cols = slice(c * (D_FF // FFN_SPLIT), (c + 1) * (D_FF // FFN_SPLIT))
        gg = lax.dot_general(hb, wg_ref[:, cols], NN, preferred_element_type=F32)
        uu = lax.dot_general(hb, wu_ref[:, cols], NN, preferred_element_type=F32)
        act = (_silu(gg) * uu).astype(BF16)
        part = lax.dot_general(act, wo_ref[cols, :], NN, preferred_element_type=F32)
        y = part if y is None else y + part
    ms = jnp.mean(y * y, axis=-1, keepdims=True)
    o_ref[...] = x + gate_ref[...] * (y * lax.rsqrt(ms + RMS_EPS) * gpost_ref[...])


def _ffn(x2, scale, shift, gate, g_pre, g_post, wg, wu, wo, tm, tiles_per_group):
    m = x2.shape[0]
    mod = lambda arr: pl.BlockSpec((None,) + arr.shape[1:], lambda i: (i // tiles_per_group, 0, 0))
    vec = pl.BlockSpec((1, D_MODEL), lambda i: (0, 0))
    resident = lambda arr: pl.BlockSpec(arr.shape, lambda i: (0, 0), pipeline_mode=pl.Buffered(1))
    return pl.pallas_call(
        _ffn_kernel,
        out_shape=jax.ShapeDtypeStruct((m, D_MODEL), F32),
        grid=(m // tm,),
        in_specs=[pl.BlockSpec((tm, D_MODEL), lambda i: (i, 0)),
                  mod(scale), mod(shift), mod(gate), vec, vec,
                  resident(wg), resident(wu), resident(wo)],
        out_specs=pl.BlockSpec((tm, D_MODEL), lambda i: (i, 0)),
        compiler_params=_cparams(("arbitrary",)),
        name="ffn",
    )(x2, scale, shift, gate, g_pre.reshape(1, D_MODEL), g_post.reshape(1, D_MODEL), wg, wu, wo)


def _unit_to_heads(x):
    n = x.shape[1]
    return jnp.transpose(x, (1, 3, 2, 0, 4)).reshape(n, ATT_UNIT, N_ATT_HEADS, HEAD_DIM)


def _layer(x2, n, t, mods, lp, att, state, li):
    shift1, scale1, gate1, shift2, scale2, gate2 = mods
    prompt = state is None
    tm = 512 if prompt else n * t
    tpg = t // tm if prompt else 1
    uc, ur, q, k, v = _in_proj(x2, scale1, shift1, lp['g_pre_mix'], lp['w_in_bf'], tm, tpg,
                               seq_shape=(n, t) if prompt else None)
    conv_params = _conv_params(lp['conv_w'], lp['conv_b'], lp['conv_ln_g'], lp['conv_ln_b'])
    rwkv_params = _rwkv_params(lp)
    ur3 = ur.reshape(n, t, RWKV_COLS)
    new_shift = ur3[:, -1]
    if prompt:
        ya, conv_cache = _conv_prompt(uc, n, t, conv_params)
        shift0 = jnp.zeros((n, 1, RWKV_COLS), F32)
        s0 = jnp.zeros((n, HEAD_DIM, RWKV_WIDTH), F32)
        yb, s_out = _rwkv(ur3, shift0, s0, rwkv_params, tr=512, t_valid=512, nseq=4)
        yc = _attn_prompt(q, k, v, att['prompt'])
        assert min(MAX_WINDOW, t) == ATT_UNIT
        new_k = _unit_to_heads(k[:, :, -1])
        new_v = _unit_to_heads(v[:, :, -1])
    else:
        cache_conv, state_shift, state_wkv, k_cache, v_cache = state
        ya, conv_cache = _conv_sample(uc, cache_conv[li], n, t, conv_params)
        tr = RWKV_CHUNK
        ur_pad = jnp.pad(ur3, ((0, 0), (0, tr - t), (0, 0)))
        yb, s_out = _rwkv(ur_pad, state_shift[li][:, None, :], _state_to_lanes(state_wkv[li]), rwkv_params,
                          tr=tr, t_valid=t, nseq=4)
        yb = yb[:, :t]
        q_r, k_r, v_r = _pairs_to_rows(q), _pairs_to_rows(k), _pairs_to_rows(v)
        yc = _rows_to_pairs(_attn_sample(q_r, k_r, v_r, k_cache, v_cache, li, att['sample'], n, t))
        new_k = k_r.reshape(n, t, N_ATT_HEADS, HEAD_DIM)
        new_v = v_r.reshape(n, t, N_ATT_HEADS, HEAD_DIM)
    yb = yb.reshape(n * t, RWKV_WIDTH)
    x2 = _out_proj(ya, yb, yc, x2, gate1, lp['g_post_mix'], lp['w_out_bf'], tm, tpg)
    tmf = 512 if prompt else n * t
    x2 = _ffn(x2, scale2, shift2, gate2, lp['g_pre_ffn'], lp['g_post_ffn'], lp['w_ffn_g_bf'], lp['w_ffn_u_bf'],
              lp['w_ffn_o_bf'], tmf, t // tmf if prompt else 1)
    return x2, (conv_cache, new_shift, _lanes_to_state(s_out), new_k, new_v)


def kernel(x_prompt, x_sample, c_prompt, c_sample, cache_conv, state_shift, state_wkv, cache_k_win, cache_v_win,
           w_ada, b_ada, g_pre_mix, g_post_mix, g_pre_ffn, g_post_ffn, w_in, w_out,
           conv_w, conv_b, conv_ln_g, conv_ln_b,
           rwkv_mu, rwkv_w0, rwkv_w2, rwkv_a0, rwkv_a2, rwkv_g2, rwkv_k_k, rwkv_k_a, rwkv_r_k,
           rwkv_ln_g, rwkv_ln_b, rel_bias, w_ffn_in, w_ffn_out):
    depth = w_in.shape[0]
    nb, tp, _ = x_prompt.shape
    ns, ts, _ = x_sample.shape
    rows = nb + ns
    rows_pad = -(-rows // SUBLANES) * SUBLANES
    c_all = jnp.pad(jnp.concatenate([c_prompt, c_sample], axis=0), ((0, rows_pad - rows), (0, 0)))
    mod = _ada_modulation(c_all, w_ada, b_ada)
    mod = mod.reshape(depth, rows_pad, 6, D_MODEL)

    win = cache_k_win.shape[2]
    att = {'prompt': _attn_prompt_bias(rel_bias), 'sample': _attn_sample_tables(rel_bias, ts, win)}
    to_pos_minor = lambda c: jnp.transpose(c, (0, 1, 3, 4, 2)).reshape(depth, ns, ATT_WIDTH, win)
    state = (cache_conv, state_shift, state_wkv, to_pos_minor(cache_k_win), to_pos_minor(cache_v_win))

    yp = x_prompt.reshape(nb * tp, D_MODEL)
    ys = x_sample.reshape(ns * ts, D_MODEL)
    outs_p, outs_s = [], []
    for li in range(depth):
        lp = dict(g_pre_mix=g_pre_mix[li], g_post_mix=g_post_mix[li], g_pre_ffn=g_pre_ffn[li],
                  g_post_ffn=g_post_ffn[li],
                  w_in_bf=w_in[li].astype(BF16), w_out_bf=w_out[li].astype(BF16),
                  w_ffn_g_bf=w_ffn_in[li, :, :D_FF].astype(BF16), w_ffn_u_bf=w_ffn_in[li, :, D_FF:].astype(BF16),
                  w_ffn_o_bf=w_ffn_out[li].astype(BF16),
                  conv_w=conv_w[li], conv_b=conv_b[li], conv_ln_g=conv_ln_g[li], conv_ln_b=conv_ln_b[li],
                  rwkv_mu=rwkv_mu[li], rwkv_w0=rwkv_w0[li], rwkv_w2=rwkv_w2[li], rwkv_a0=rwkv_a0[li],
                  rwkv_a2=rwkv_a2[li], rwkv_g2=rwkv_g2[li], rwkv_k_k=rwkv_k_k[li], rwkv_k_a=rwkv_k_a[li],
                  rwkv_r_k=rwkv_r_k[li], rwkv_ln_g=rwkv_ln_g[li], rwkv_ln_b=rwkv_ln_b[li])
        mods_p = tuple(mod[li, :nb, j][:, None, :] for j in range(6))
        mods_s = tuple(jnp.repeat(mod[li, nb:rows, j], ts, axis=0)[None] for j in range(6))
        yp, st_p = _layer(yp, nb, tp, mods_p, lp, att, None, li)
        ys, st_s = _layer(ys, ns, ts, mods_s, lp, att, state, li)
        outs_p.append(st_p)
        outs_s.append(st_s)
    stack = lambda outs, i: jnp.stack([o[i] for o in outs])
    return (yp.reshape(nb, tp, D_MODEL), ys.reshape(ns, ts, D_MODEL),
            stack(outs_p, 0), stack(outs_s, 0),
            stack(outs_p, 1), stack(outs_s, 1),
            stack(outs_p, 2), stack(outs_s, 2),
            stack(outs_p, 3), stack(outs_s, 3),
            stack(outs_p, 4), stack(outs_s, 4))
```

```python
import functools
import math

import numpy as np
import jax
import jax.numpy as jnp
from jax import lax
from jax.experimental import pallas as pl
from jax.experimental.pallas import tpu as pltpu

F32 = jnp.float32
BF16 = jnp.bfloat16

D_MODEL = 1024
HEAD_DIM = 64
CONV_WIDTH = 256
RWKV_WIDTH = 256
ATT_WIDTH = 512
N_RWKV_HEADS = RWKV_WIDTH // HEAD_DIM
N_ATT_HEADS = ATT_WIDTH // HEAD_DIM
CONV_K = 31
DECAY_LORA = 64
AAA_LORA = 64
GATE_LORA = 128
CONV_COLS = 2 * CONV_WIDTH
RWKV_COLS = 3 * RWKV_WIDTH + DECAY_LORA + AAA_LORA + GATE_LORA
ATT_COLS = 3 * ATT_WIDTH
IN_COLS = CONV_COLS + RWKV_COLS + ATT_COLS
DILATIONS = ((128, 1), (512, 4), (2048, 16))
MAX_WINDOW = 2048
BLOCK = 128
N_REL_BUCKETS = 32
REL_EXACT = N_REL_BUCKETS // 2
REL_MAX_DIST = MAX_WINDOW
D_FF = 2816
RMS_EPS = 1e-6
LN_EPS = 1e-5
RWKV_GN_EPS = HEAD_DIM * 1e-5
DECAY_SCALE = math.exp(-0.5)
NEG_INF = -1e30

VMEM_LIMIT_BYTES = 56 * 1024 * 1024
SUBLANES = 8
LANES = 128
CONV_HALO = 32
FFN_SPLIT = 2
RWKV_CHUNK = HEAD_DIM
N_PAIRS = ATT_WIDTH // LANES
ATT_RES = max(d for _, d in DILATIONS)
ATT_UNIT = BLOCK * ATT_RES
ATT_GROUP = (8, 8, 8)

NN = (((1,), (0,)), ((), ()))
NT = (((1,), (1,)), ((), ()))


def _cparams(sem):
    return pltpu.CompilerParams(dimension_semantics=sem, vmem_limit_bytes=VMEM_LIMIT_BYTES)


def _split_bf16(x, n):
    if x.dtype == BF16:
        return [x]
    pieces = []
    r = x
    for i in range(n):
        p = r.astype(BF16)
        pieces.append(p)
        if i + 1 < n:
            r = r - p.astype(F32)
    return pieces


def _mmp(ap, bp, dims=NN):
    order = max(len(ap), len(bp))
    out = None
    for i, x in enumerate(ap):
        for j, y in enumerate(bp):
            if i + j < order:
                t = lax.dot_general(x, y, dims, preferred_element_type=F32)
                out = t if out is None else out + t
    return out


def _mm(a, b, dims=NN, pa=2, pb=2):
    return _mmp(_split_bf16(a, pa), _split_bf16(b, pb), dims)


def _dot1(a, b, dims=NN):
    return lax.dot_general(a.astype(BF16), b.astype(BF16), dims, preferred_element_type=F32)


def _lockstep(gens):
    results = [None] * len(gens)
    while any(r is None for r in results):
        results = [next(g) if r is None else r for g, r in zip(gens, results)]
    return results


def _sigmoid(x):
    return 1.0 / (1.0 + jnp.exp(-x))


def _silu(x):
    return x * _sigmoid(x)


def _rel_bucket_np(dist):
    d = np.maximum(dist, 1).astype(np.float32)
    large = REL_EXACT + (np.log(d / np.float32(REL_EXACT)) / np.float32(math.log(REL_MAX_DIST / REL_EXACT))
                         * np.float32(N_REL_BUCKETS - REL_EXACT)).astype(np.int32)
    large = np.minimum(large, N_REL_BUCKETS - 1)
    return np.where(dist < REL_EXACT, dist, large)


def _bias_rows(rel_bias, dist):
    idx = _rel_bucket_np(np.asarray(dist, np.int32))
    onehot = jnp.asarray(np.eye(N_REL_BUCKETS, dtype=np.float32)[idx.reshape(-1)])
    rows = jnp.dot(onehot, rel_bias, precision=lax.Precision.HIGHEST)
    return rows.reshape(idx.shape + (rel_bias.shape[1],))


def _ada_kernel(c_ref, w_ref, b_ref, o_ref):
    a = _silu(c_ref[...])
    o_ref[...] = _dot1(a, w_ref[...]) + b_ref[...]


def _ada_modulation(c_all, w_ada, b_ada):
    depth = w_ada.shape[0]
    rows = c_all.shape[0]
    tn = D_MODEL
    return pl.pallas_call(
        _ada_kernel,
        out_shape=jax.ShapeDtypeStruct((depth, rows, 6 * D_MODEL), F32),
        grid=(depth, 6 * D_MODEL // tn),
        in_specs=[
            pl.BlockSpec((rows, D_MODEL), lambda l, j: (0, 0)),
            pl.BlockSpec((None, D_MODEL, tn), lambda l, j: (l, 0, j)),
            pl.BlockSpec((None, 1, tn), lambda l, j: (l, 0, j)),
        ],
        out_specs=pl.BlockSpec((None, rows, tn), lambda l, j: (l, 0, j)),
        compiler_params=_cparams(("arbitrary", "arbitrary")),
        name="ada_modulation",
    )(c_all, w_ada, b_ada.reshape(depth, 1, 6 * D_MODEL))


def _in_proj_kernel(x_ref, sc_ref, sh_ref, g_ref, w_ref, *refs, residue_major, tiles_per_seq=None):
    if residue_major:
        perm_ref, wkt_ref, wvt_ref, uc_ref, ur_ref, q_ref, k_ref, v_ref, kt_ref, vt_ref = refs
    else:
        uc_ref, ur_ref, q_ref, k_ref, v_ref = refs
    x = x_ref[...]
    ms = jnp.mean(x * x, axis=-1, keepdims=True)
    h = x * lax.rsqrt(ms + RMS_EPS) * g_ref[...]
    h = h * (1.0 + sc_ref[...]) + sh_ref[...]
    hb = h.astype(BF16)
    uc_ref[...] = lax.dot_general(hb, w_ref[:, 0:CONV_COLS], NN, preferred_element_type=F32)
    o = CONV_COLS
    ur_ref[...] = lax.dot_general(hb, w_ref[:, o:o + RWKV_COLS], NN, preferred_element_type=F32)
    o += RWKV_COLS
    if residue_major:
        @pl.when(pl.program_id(0) % tiles_per_seq >= tiles_per_seq - ATT_UNIT // hb.shape[0])
        def _():
            kt_ref[...] = lax.dot_general(wkt_ref[...], hb, NT, preferred_element_type=F32)
            vt_ref[...] = lax.dot_general(wvt_ref[...], hb, NT, preferred_element_type=F32)

        hb = lax.dot_general(perm_ref[...], hb, NN, preferred_element_type=F32).astype(BF16)
        run = hb.shape[0] // ATT_RES
    for ref in (q_ref, k_ref, v_ref):
        res = lax.dot_general(hb, w_ref[:, o:o + ATT_WIDTH], NN, preferred_element_type=F32)
        for p in range(N_PAIRS):
            cols = res[:, p * LANES:(p + 1) * LANES]
            if residue_major:
                for r in range(ATT_RES):
                    ref[p, r] = cols[r * run:(r + 1) * run]
            else:
                ref[p] = cols
        o += ATT_WIDTH


def _tile_perm(tm):
    run = tm // ATT_RES
    t = np.arange(tm)
    perm = np.zeros((tm, tm), np.float32)
    perm[(t % ATT_RES) * run + t // ATT_RES, t] = 1.0
    return jnp.asarray(perm, BF16), jnp.asarray(perm.T, BF16)


def _mod_spec(mod, tiles_per_group):
    _, r, d = mod.shape
    return pl.BlockSpec((None, r, d), lambda i: (i // tiles_per_group, 0, 0))


def _in_proj(x2, scale, shift, g, w_bf, tm, tiles_per_group, seq_shape=None):
    m = x2.shape[0]
    flat = lambda w: jax.ShapeDtypeStruct((m, w), F32)
    residue_major = seq_shape is not None
    ins = [x2, scale, shift, g.reshape(1, D_MODEL), w_bf]
    in_specs = [
        pl.BlockSpec((tm, D_MODEL), lambda i: (i, 0)),
        _mod_spec(scale, tiles_per_group),
        _mod_spec(shift, tiles_per_group),
        pl.BlockSpec((1, D_MODEL), lambda i: (0, 0)),
        pl.BlockSpec((D_MODEL, IN_COLS), lambda i: (0, 0)),
    ]
    if residue_major:
        n, t = seq_shape
        per_unit = ATT_UNIT // tm
        per_seq = t // tm
        rows_res = ATT_UNIT // ATT_RES
        pairs = jax.ShapeDtypeStruct((N_PAIRS, n, t // ATT_UNIT, ATT_RES, rows_res, LANES), F32)
        pair_spec = pl.BlockSpec((N_PAIRS, None, None, ATT_RES, tm // ATT_RES, LANES),
                                 lambda i: (0, i // per_seq, (i % per_seq) // per_unit, 0, i % per_unit, 0))
        perm, _ = _tile_perm(tm)
        k0 = CONV_COLS + RWKV_COLS + ATT_WIDTH
        wkt = w_bf[:, k0:k0 + ATT_WIDTH].T
        wvt = w_bf[:, k0 + ATT_WIDTH:k0 + 2 * ATT_WIDTH].T
        ins += [perm, wkt, wvt]
        in_specs += [pl.BlockSpec((tm, tm), lambda i: (0, 0)),
                     pl.BlockSpec((ATT_WIDTH, D_MODEL), lambda i: (0, 0)),
                     pl.BlockSpec((ATT_WIDTH, D_MODEL), lambda i: (0, 0))]
        win_t = jax.ShapeDtypeStruct((n, ATT_WIDTH, ATT_UNIT), F32)
        win_spec = pl.BlockSpec((None, ATT_WIDTH, tm),
                                lambda i: (i // per_seq, 0, jnp.maximum(i % per_seq - (per_seq - per_unit), 0)))
        extra_out, extra_specs = (win_t, win_t), (win_spec, win_spec)
        kernel_fn = functools.partial(_in_proj_kernel, residue_major=True, tiles_per_seq=per_seq)
    else:
        pairs = jax.ShapeDtypeStruct((N_PAIRS, m, LANES), F32)
        pair_spec = pl.BlockSpec((N_PAIRS, tm, LANES), lambda i: (0, i, 0))
        extra_out, extra_specs = (), ()
        kernel_fn = functools.partial(_in_proj_kernel, residue_major=False)
    return pl.pallas_call(
        kernel_fn,
        out_shape=(flat(CONV_COLS), flat(RWKV_COLS), pairs, pairs, pairs) + extra_out,
        grid=(m // tm,),
        in_specs=in_specs,
        out_specs=(pl.BlockSpec((tm, CONV_COLS), lambda i: (i, 0)), pl.BlockSpec((tm, RWKV_COLS), lambda i: (i, 0)),
                   pair_spec, pair_spec, pair_spec) + extra_specs,
        compiler_params=_cparams(("arbitrary",)),
        name="in_proj",
    )(*ins)


def _pairs_to_rows(x):
    return jnp.transpose(x, (1, 0, 2)).reshape(x.shape[1], ATT_WIDTH)


def _rows_to_pairs(x):
    return jnp.transpose(x.reshape(x.shape[0], N_PAIRS, LANES), (1, 0, 2))


def _conv_core(zs_ref, zsh_ref, w_ref, b_ref, lg_ref, lb_ref, y_ref, tt):
    n_rows = CONV_HALO + tt
    zs_ref[n_rows:n_rows + SUBLANES, :] = jnp.zeros((SUBLANES, CONV_WIDTH), F32)
    for s in range(SUBLANES):
        zsh_ref[s] = zs_ref[pl.ds(s, n_rows), :]
    off = CONV_HALO - (CONV_K - 1)
    sub = min(tt, 64)
    for r0 in range(0, tt, sub):
        acc = jnp.zeros((sub, CONV_WIDTH), F32)
        for j in range(CONV_K):
            a, b = divmod(off + j, SUBLANES)
            acc = acc + w_ref[j:j + 1, :] * zsh_ref[b, pl.ds(r0 + SUBLANES * a, sub), :]
        y = acc + b_ref[...]
        mu = jnp.mean(y, axis=-1, keepdims=True)
        yc = y - mu
        var = jnp.mean(yc * yc, axis=-1, keepdims=True)
        yn = yc * lax.rsqrt(var + LN_EPS) * lg_ref[...] + lb_ref[...]
        y_ref[r0:r0 + sub, :] = _silu(yn)


def _glu(u):
    return u[:, :CONV_WIDTH] * _sigmoid(u[:, CONV_WIDTH:])


def _conv_prompt_kernel(u_ref, uh_ref, w_ref, b_ref, lg_ref, lb_ref, y_ref, zt_ref, zs_ref, zsh_ref, *, tt):
    j = pl.program_id(1)
    zh = _glu(uh_ref[...])
    zs_ref[0:CONV_HALO, :] = jnp.where(j > 0, zh, 0.0)
    zs_ref[CONV_HALO:CONV_HALO + tt, :] = _glu(u_ref[...])
    _conv_core(zs_ref, zsh_ref, w_ref, b_ref, lg_ref, lb_ref, y_ref, tt)
    zt_ref[...] = zs_ref[tt:tt + CONV_HALO, :]


def _conv_sample_kernel(u_ref, zh_ref, w_ref, b_ref, lg_ref, lb_ref, y_ref, z_ref, zs_ref, zsh_ref, *, tt):
    z = _glu(u_ref[...])
    zs_ref[0:CONV_HALO, :] = zh_ref[...]
    zs_ref[CONV_HALO:CONV_HALO + tt, :] = z
    _conv_core(zs_ref, zsh_ref, w_ref, b_ref, lg_ref, lb_ref, y_ref, tt)
    z_ref[...] = z


def _conv_param_specs(nargs_grid):
    cmap = (lambda n, j: (0, 0)) if nargs_grid == 2 else (lambda n: (0, 0))
    return [
        pl.BlockSpec((CONV_HALO, CONV_WIDTH), cmap),
        pl.BlockSpec((1, CONV_WIDTH), cmap),
        pl.BlockSpec((1, CONV_WIDTH), cmap),
        pl.BlockSpec((1, CONV_WIDTH), cmap),
    ]


def _conv_params(conv_w, conv_b, ln_g, ln_b):
    w = jnp.pad(conv_w, ((0, CONV_HALO - CONV_K), (0, 0)))
    return w, conv_b.reshape(1, -1), ln_g.reshape(1, -1), ln_b.reshape(1, -1)


def _conv_prompt(uc, n, t, params, tt=512):
    u3 = uc.reshape(n, t, CONV_COLS)
    hb = tt // CONV_HALO
    y, zt = pl.pallas_call(
        functools.partial(_conv_prompt_kernel, tt=tt),
        out_shape=(jax.ShapeDtypeStruct((n, t, CONV_WIDTH), F32),
                   jax.ShapeDtypeStruct((n, CONV_HALO, CONV_WIDTH), F32)),
        grid=(n, t // tt),
        in_specs=[
            pl.BlockSpec((None, tt, CONV_COLS), lambda b, j: (b, j, 0)),
            pl.BlockSpec((None, CONV_HALO, CONV_COLS), lambda b, j: (b, jnp.maximum(j * hb - 1, 0), 0)),
        ] + _conv_param_specs(2),
        out_specs=(pl.BlockSpec((None, tt, CONV_WIDTH), lambda b, j: (b, j, 0)),
                   pl.BlockSpec((None, CONV_HALO, CONV_WIDTH), lambda b, j: (b, 0, 0))),
        scratch_shapes=[pltpu.VMEM((CONV_HALO + tt + SUBLANES, CONV_WIDTH), F32),
                        pltpu.VMEM((SUBLANES, CONV_HALO + tt, CONV_WIDTH), F32)],
        compiler_params=_cparams(("arbitrary", "arbitrary")),
        name="conv_prompt",
    )(u3, u3, *params)
    return y.reshape(n * t, CONV_WIDTH), zt[:, CONV_HALO - (CONV_K - 1):]


def _conv_sample(uc, cache, n, t, params):
    tt = SUBLANES
    u3 = jnp.pad(uc.reshape(n, t, CONV_COLS), ((0, 0), (0, tt - t), (0, 0)))
    zh = jnp.pad(cache, ((0, 0), (CONV_HALO - (CONV_K - 1), 0), (0, 0)))
    y, z = pl.pallas_call(
        functools.partial(_conv_sample_kernel, tt=tt),
        out_shape=(jax.ShapeDtypeStruct((n, tt, CONV_WIDTH), F32),
                   jax.ShapeDtypeStruct((n, tt, CONV_WIDTH), F32)),
        grid=(n,),
        in_specs=[
            pl.BlockSpec((None, tt, CONV_COLS), lambda b: (b, 0, 0)),
            pl.BlockSpec((None, CONV_HALO, CONV_WIDTH), lambda b: (b, 0, 0)),
        ] + _conv_param_specs(1),
        out_specs=(pl.BlockSpec((None, tt, CONV_WIDTH), lambda b: (b, 0, 0)),
                   pl.BlockSpec((None, tt, CONV_WIDTH), lambda b: (b, 0, 0))),
        scratch_shapes=[pltpu.VMEM((CONV_HALO + tt + SUBLANES, CONV_WIDTH), F32),
                        pltpu.VMEM((SUBLANES, CONV_HALO + tt, CONV_WIDTH), F32)],
        compiler_params=_cparams(("arbitrary",)),
        name="conv_sample",
    )(u3, zh, *params)
    new_cache = jnp.concatenate([cache[:, t:], z[:, :t]], axis=1)
    return y[:, :t].reshape(n * t, CONV_WIDTH), new_cache


def _rwkv_consts():
    c = RWKV_CHUNK
    rows = np.arange(c)
    tri = (rows[None, :] <= rows[:, None]).astype(np.float32)
    lanes = np.arange(RWKV_WIDTH)
    bd = (lanes[:, None] // HEAD_DIM == lanes[None, :] // HEAD_DIM).astype(np.float32)
    return jnp.asarray(tri, BF16), jnp.asarray(bd, BF16)


def _rwkv_kernel(u_ref, up_ref, sh0_ref, s0_ref, mu_ref, w0_ref, wwa_ref, a0_ref, g2_ref, kk_ref, ka_ref, rk_ref,
                 lg_ref, lb_ref, tri_ref, bd_ref,
                 y_ref, so_ref,
                 us_ref, lw_ref, kn_ref, kb_ref, k2_ref, rr_ref, vv_ref, bon_ref, gate_ref, yy_ref, s_ref,
                 *, tr, t_valid, nseq):
    j = pl.program_id(1)
    c = RWKV_CHUNK
    bd = bd_ref[...]
    o = 3 * RWKV_WIDTH

    @pl.when(j == 0)
    def _():
        s_ref[...] = s0_ref[...]

    for sq in range(nseq):
        u = u_ref[sq]
        prev = jnp.where(j > 0, up_ref[sq], jnp.broadcast_to(sh0_ref[sq], (SUBLANES, RWKV_COLS)))
        us_ref[sq, 0:SUBLANES, :] = prev
        us_ref[sq, SUBLANES:SUBLANES + tr, :] = u
        u_prev = us_ref[sq, pl.ds(SUBLANES - 1, tr), :]
        xs = u + (u_prev - u) * mu_ref[...]
        r = xs[:, :RWKV_WIDTH]
        k = xs[:, RWKV_WIDTH:2 * RWKV_WIDTH]
        v = xs[:, 2 * RWKV_WIDTH:o]
        lo = xs[:, o:o + DECAY_LORA + AAA_LORA]
        lane = lax.broadcasted_iota(jnp.int32, lo.shape, 1)
        lo = jnp.where(lane < DECAY_LORA, jnp.tanh(lo), lo)
        wa = _dot1(lo, wwa_ref[...])
        g = _dot1(_sigmoid(xs[:, o + DECAY_LORA + AAA_LORA:]), g2_ref[...])
        lw = -DECAY_SCALE * _sigmoid(w0_ref[...] + wa[:, :RWKV_WIDTH])
        a = _sigmoid(a0_ref[...] + wa[:, RWKV_WIDTH:])
        kkr = k * kk_ref[...]
        ss = _mm(kkr * kkr, bd, pa=2)
        kk = kkr / jnp.maximum(jnp.sqrt(ss), 1e-12)
        k2 = k * (1.0 + (a - 1.0) * ka_ref[...])
        if t_valid < tr:
            valid = lax.broadcasted_iota(jnp.int32, (tr, RWKV_WIDTH), 0) < t_valid
            zero = jnp.zeros_like(r)
            r, k2, v, kk, lw = (jnp.where(valid, z, zero) for z in (r, k2, v, kk, lw))
        lw_ref[sq] = lw
        kn_ref[sq] = -kk
        kb_ref[sq] = kk * a
        k2_ref[sq] = k2
        rr_ref[sq] = r
        vv_ref[sq] = v
        bon_ref[sq] = _mm(r * k2 * rk_ref[...], bd, pa=2) * v
        gate_ref[sq] = g

    row = lax.broadcasted_iota(jnp.int32, (c, LANES), 0)
    col = lax.broadcasted_iota(jnp.int32, (c, LANES), 1) & (HEAD_DIM - 1)
    strict = col < row
    incl = col <= row
    ident = jnp.where(col == row, 1.0, 0.0)
    bd2 = bd[0:LANES, 0:LANES]
    bd2_mask = bd2 > 0.5
    n_steps = int(round(math.log2(c)))

    def dot(a, b, dims=NN):
        return lax.dot_general(a.astype(BF16), b, dims, preferred_element_type=F32)

    def stack(x):
        xb = x.astype(BF16)
        return jnp.concatenate([xb, xb], axis=0) * bd2

    def rows2(a, b):
        return jnp.concatenate([a, b], axis=0)

    def chunk_pair(pt, bt, kt, rt, vc, gc, s0):
        zero = jnp.zeros((c, LANES), F32)
        pr = rows2(pt, rt)
        xb = dot(pr, stack(bt), NT)
        xk = dot(pr, stack(kt), NT)
        yield None
        lm = jnp.where(strict, xb[:c], zero)
        mm_ = jnp.where(strict, xk[:c], zero)
        qb = jnp.where(incl, xb[c:], zero)
        qk = jnp.where(incl, xk[c:], zero)
        tinv = ident + lm
        lk = dot(lm, stack(lm))
        mq = dot(rows2(mm_, qk), stack(vc))
        yield None
        for step in range(1, n_steps):
            if step + 1 < n_steps:
                both = dot(rows2(lk, tinv), stack(lk))
                lk = both[:c]
                tinv = tinv + both[c:]
            else:
                tinv = tinv + dot(tinv, stack(lk))
            yield None
        p2 = dot(tinv, stack(pt))
        u0 = dot(tinv, stack(mq[:c]))
        yield None
        xs_ = dot(rows2(p2, rt), stack(s0), NT)
        yield None
        uu = xs_[:c] + u0
        y = xs_[c:] + dot(qb, stack(uu)) + mq[c:]
        uvt = rows2(uu, vc).T
        bk = rows2(bt * gc, kt * gc).astype(BF16)
        z = jnp.where(bd2_mask, dot(uvt, bk), 0.0)
        yield y, s0 * gc + z[0:c] + z[c:2 * c]

    def chunk(ci, carry):
        sl = pl.ds(pl.multiple_of(ci * c, c), c)
        work = []
        for sq in range(nseq):
            lw_c = lw_ref[sq, sl, :]
            gi = _mmp([tri_ref[...]], _split_bf16(lw_c, 3))
            e_neg = jnp.exp(-gi)
            gc = jnp.exp(gi[c - 1:c, :])
            pt = kn_ref[sq, sl, :] * jnp.exp(gi - lw_c)
            bt = kb_ref[sq, sl, :] * e_neg
            kt = k2_ref[sq, sl, :] * e_neg
            rt = rr_ref[sq, sl, :] * jnp.exp(gi)
            vc = vv_ref[sq, sl, :]
            s0 = s_ref[sq]
            for p in range(RWKV_WIDTH // LANES):
                ln = slice(p * LANES, (p + 1) * LANES)
                work.append((sq, ln, tuple(x[:, ln] for x in (pt, bt, kt, rt, vc, gc, s0))))
        results = _lockstep([chunk_pair(*args) for _, _, args in work])
        done = [(sq, ln, res) for (sq, ln, _), res in zip(work, results)]
        for sq, ln, (y, s_new) in done:
            yy_ref[sq, sl, ln] = y
            s_ref[sq, :, ln] = s_new
        return carry

    lax.fori_loop(0, tr // c, chunk, 0)

    inv = 1.0 / HEAD_DIM
    for sq in range(nseq):
        y = yy_ref[sq]
        mu = _mm(y, bd, pa=2) * inv
        yc = y - mu
        var = _mm(yc * yc, bd, pa=2) * inv
        yn = yc * lax.rsqrt(var + RWKV_GN_EPS) * lg_ref[...] + lb_ref[...]
        y_ref[sq] = (yn + bon_ref[sq]) * gate_ref[sq]
    so_ref[...] = s_ref[...]


def _rwkv_params(p):
    z = jnp.zeros((DECAY_LORA, RWKV_WIDTH), F32)
    wwa = jnp.concatenate([jnp.concatenate([p['rwkv_w2'], z], axis=1),
                           jnp.concatenate([z, p['rwkv_a2']], axis=1)], axis=0)
    row = lambda x: x.reshape(1, -1)
    return (row(p['rwkv_mu']), row(p['rwkv_w0']), wwa, row(p['rwkv_a0']), p['rwkv_g2'], row(p['rwkv_k_k']),
            row(p['rwkv_k_a']), row(p['rwkv_r_k']), row(p['rwkv_ln_g']), row(p['rwkv_ln_b']))


def _rwkv(u3, shift0, s0, params, tr, t_valid, nseq):
    n, t, _ = u3.shape
    assert n % nseq == 0
    consts = _rwkv_consts()
    hb = tr // SUBLANES
    const2 = lambda b, j: (0, 0)
    full = lambda arr: pl.BlockSpec(arr.shape, const2)
    tile = lambda: pltpu.VMEM((nseq, tr, RWKV_WIDTH), F32)
    y, s_out = pl.pallas_call(
        functools.partial(_rwkv_kernel, tr=tr, t_valid=t_valid, nseq=nseq),
        out_shape=(jax.ShapeDtypeStruct((n, t, RWKV_WIDTH), F32),
                   jax.ShapeDtypeStruct((n, HEAD_DIM, RWKV_WIDTH), F32)),
        grid=(n // nseq, t // tr),
        in_specs=[
            pl.BlockSpec((nseq, tr, RWKV_COLS), lambda b, j: (b, j, 0)),
            pl.BlockSpec((nseq, SUBLANES, RWKV_COLS), lambda b, j: (b, jnp.maximum(j * hb - 1, 0), 0)),
            pl.BlockSpec((nseq, 1, RWKV_COLS), lambda b, j: (b, 0, 0)),
            pl.BlockSpec((nseq, HEAD_DIM, RWKV_WIDTH), lambda b, j: (b, 0, 0)),
        ] + [full(x) for x in params] + [full(x) for x in consts],
        out_specs=(pl.BlockSpec((nseq, tr, RWKV_WIDTH), lambda b, j: (b, j, 0)),
                   pl.BlockSpec((nseq, HEAD_DIM, RWKV_WIDTH), lambda b, j: (b, 0, 0))),
        scratch_shapes=[pltpu.VMEM((nseq, SUBLANES + tr, RWKV_COLS), F32)] + [tile() for _ in range(9)]
                       + [pltpu.VMEM((nseq, HEAD_DIM, RWKV_WIDTH), F32)],
        compiler_params=_cparams(("arbitrary", "arbitrary")),
        name="rwkv7",
    )(u3, u3, shift0, s0, *params, *consts)
    return y, s_out


def _state_to_lanes(s):
    n = s.shape[0]
    return jnp.transpose(s, (0, 2, 1, 3)).reshape(n, HEAD_DIM, RWKV_WIDTH)


def _lanes_to_state(s):
    n = s.shape[0]
    return jnp.transpose(s.reshape(n, HEAD_DIM, N_RWKV_HEADS, HEAD_DIM), (0, 2, 1, 3))


def _att_natural_index(dil):
    runs = ATT_RES // dil
    p = np.arange(BLOCK)
    return runs * (p % (BLOCK // runs)) + p // (BLOCK // runs)


def _attn_prompt_kernel(q_ref, kc_ref, kp_ref, vc_ref, vp_ref, bias_ref, o_ref, kcat, vcat, acc, m0, m1, l0, l1):
    u = pl.program_id(2)
    unit = ATT_UNIT
    scale = HEAD_DIM ** -0.5
    kcat[0:unit, :] = kp_ref[...]
    kcat[unit:2 * unit, :] = kc_ref[...]
    vcat[0:unit, :] = vp_ref[...]
    vcat[unit:2 * unit, :] = vc_ref[...]
    col = lax.broadcasted_iota(jnp.int32, (2 * BLOCK, 2 * BLOCK), 1)
    lo = lax.broadcasted_iota(jnp.int32, (BLOCK, LANES), 1) < HEAD_DIM
    n_br = len(DILATIONS)
    rows2 = lambda a, b: jnp.concatenate([a, b], axis=0)
    for bi, (window, dil) in enumerate(DILATIONS):
        first = bi == 0
        last = bi == n_br - 1
        shift = int(round(math.log2(dil)))
        n_runs = ATT_RES // dil
        run_len = BLOCK // n_runs
        n_blk = unit // (BLOCK * dil)

        def runs_of(r, g):
            base = (g // n_blk) * unit + (g % n_blk) * run_len
            return [base + (dil * m + r) * BLOCK for m in range(n_runs)]

        def gather(ref, starts):
            return jnp.concatenate([ref[pl.ds(pl.multiple_of(s0, SUBLANES), run_len), :] for s0 in starts], axis=0)

        def sub_block(j, bi, dil, shift, first, last):
            r = j & (dil - 1)
            blk = j >> shift
            q_rows = runs_of(r, blk)
            k_rows = runs_of(r, n_blk + blk - 1) + runs_of(r, n_blk + blk)
            q = gather(q_ref, q_rows) * scale
            kk = gather(kcat, k_rows).astype(BF16)
            vv = gather(vcat, k_rows).astype(BF16)
            q2 = rows2(jnp.where(lo, q, 0.0), jnp.where(lo, 0.0, q)).astype(BF16)
            s = lax.dot_general(q2, kk, NT, preferred_element_type=F32)
            if not first:
                m_old = rows2(gather(m0, q_rows), gather(m1, q_rows))
                l_old = rows2(gather(l0, q_rows), gather(l1, q_rows))
                acc_old = gather(acc, q_rows)
            yield None
            s = s + bias_ref[bi]
            s = jnp.where((col < BLOCK) & (u == 0) & (blk == 0), NEG_INF, s)
            m_cur = jnp.max(s, axis=-1, keepdims=True)
            if first:
                m_new = jnp.broadcast_to(m_cur, (2 * BLOCK, LANES))
            else:
                m_new = jnp.maximum(m_old, m_cur)
                alpha = jnp.exp(m_old - m_new)
            p = jnp.exp(s - jnp.concatenate([m_new, m_new], axis=1))
            l_new = jnp.sum(p, axis=-1, keepdims=True)
            if first:
                l_new = jnp.broadcast_to(l_new, (2 * BLOCK, LANES))
            else:
                l_new = alpha * l_old + l_new
            o2 = lax.dot_general(p.astype(BF16), vv, NN, preferred_element_type=F32)
            yield None
            o_pair = jnp.where(lo, o2[:BLOCK], o2[BLOCK:])
            if not first:
                o_pair = o_pair + jnp.where(lo, alpha[:BLOCK], alpha[BLOCK:]) * acc_old
            if last:
                linv = 1.0 / l_new
                yield [(o_ref, q_rows, o_pair * jnp.where(lo, linv[:BLOCK], linv[BLOCK:]))]
            else:
                yield [(m0, q_rows, m_new[:BLOCK]), (m1, q_rows, m_new[BLOCK:]),
                       (l0, q_rows, l_new[:BLOCK]), (l1, q_rows, l_new[BLOCK:]), (acc, q_rows, o_pair)]

        group = ATT_GROUP[bi]

        def body(jg, carry, bi=bi, dil=dil, shift=shift, first=first, last=last, group=group, run_len=run_len):
            groups = _lockstep([sub_block(jg * group + g, bi, dil, shift, first, last) for g in range(group)])
            for stores in groups:
                for ref, starts, val in stores:
                    for m, s0 in enumerate(starts):
                        ref[pl.ds(pl.multiple_of(s0, SUBLANES), run_len), :] = val[m * run_len:(m + 1) * run_len]
            return carry

        lax.fori_loop(0, unit // BLOCK // group, body, 0)


def _attn_prompt_bias(rel_bias):
    tabs = []
    for _, dil in DILATIONS:
        a = _att_natural_index(dil)
        c = np.concatenate([a, BLOCK + a])
        delta = a[:, None] + BLOCK - c[None, :]
        band = (delta >= 0) & (delta <= BLOCK)
        b = _bias_rows(rel_bias, np.clip(delta, 0, BLOCK) * dil)
        b = jnp.where(jnp.asarray(band)[:, :, None], b, NEG_INF)
        tabs.append(jnp.transpose(b, (2, 0, 1)))
    w = jnp.stack(tabs)
    w = w.reshape(len(DILATIONS), N_PAIRS, 2 * BLOCK, 2 * BLOCK)
    return jnp.transpose(w, (1, 0, 2, 3))


def _attn_prompt(q, k, v, btab):
    unit = ATT_UNIT
    n_pairs, n, units = q.shape[:3]
    r5 = lambda x: x.reshape(n_pairs, n, units, unit, LANES)
    cur = pl.BlockSpec((None, None, None, unit, LANES), lambda a, p, u: (p, a, u, 0, 0))
    prv = pl.BlockSpec((None, None, None, unit, LANES), lambda a, p, u: (p, a, jnp.maximum(u - 1, 0), 0, 0))
    stat = lambda: pltpu.VMEM((unit, LANES), F32)
    out = pl.pallas_call(
        _attn_prompt_kernel,
        out_shape=jax.ShapeDtypeStruct((n_pairs, n, units, unit, LANES), F32),
        grid=(n, n_pairs, units),
        in_specs=[cur, cur, prv, cur, prv,
                  pl.BlockSpec((None, len(DILATIONS), 2 * BLOCK, 2 * BLOCK), lambda a, p, u: (p, 0, 0, 0))],
        out_specs=cur,
        scratch_shapes=[pltpu.VMEM((2 * unit, LANES), F32), pltpu.VMEM((2 * unit, LANES), F32),
                        stat(), stat(), stat(), stat(), stat()],
        compiler_params=_cparams(("arbitrary", "arbitrary", "arbitrary")),
        name="attn_prompt",
    )(r5(q), r5(k), r5(k), r5(v), r5(v), btab)
    return out.reshape(q.shape)


def _attn_sample_kernel(q_ref, kn_ref, vn_ref, kc_ref, vc_ref, bt_ref, bn_ref, cnt_ref, cn_ref, hm_ref, o_ref,
                        kn_s, vn_s, *, s_new):
    scale = HEAD_DIM ** -0.5
    hm = hm_ref[...]
    qm = jnp.concatenate([(q_ref[s:s + 1, :] * scale) * hm for s in range(s_new)], axis=0).astype(BF16)
    kn_s[...] = jnp.zeros_like(kn_s)
    vn_s[...] = jnp.zeros_like(vn_s)
    kn_s[0:s_new, :] = kn_ref[...]
    vn_s[0:s_new, :] = vn_ref[...]
    sc = lax.dot_general(qm, kc_ref[...].astype(BF16), NN, preferred_element_type=F32) + bt_ref[...]
    sn = lax.dot_general(qm, kn_s[...].astype(BF16), NT, preferred_element_type=F32) + bn_ref[...]
    cnt = cnt_ref[...]
    cn = cn_ref[...]
    sc = jnp.where(cnt > 0.0, sc, NEG_INF)
    sn = jnp.where(cn > 0.0, sn, NEG_INF)
    m = jnp.maximum(jnp.max(sc, axis=-1, keepdims=True), jnp.max(sn, axis=-1, keepdims=True))
    p = cnt * jnp.exp(sc - m)
    pn = cn * jnp.exp(sn - m)
    l = jnp.sum(p, axis=-1, keepdims=True) + jnp.sum(pn, axis=-1, keepdims=True)
    o = (lax.dot_general(p.astype(BF16), vc_ref[...].astype(BF16), NT, preferred_element_type=F32)
         + lax.dot_general(pn.astype(BF16), vn_s[...].astype(BF16), NN, preferred_element_type=F32))
    o = o / l
    for s in range(s_new):
        o_ref[s:s + 1, :] = jnp.sum(o[s * N_ATT_HEADS:(s + 1) * N_ATT_HEADS] * hm, axis=0, keepdims=True)


def _attn_sample_tables(rel_bias, s_new, win):
    s = np.arange(s_new)[:, None]
    dist_c = win + s - np.arange(win)[None, :]
    dist_n = s - np.arange(LANES)[None, :]
    cnt_c = np.zeros(dist_c.shape, np.float32)
    cnt_n = np.zeros(dist_n.shape, np.float32)
    for window, dil in DILATIONS:
        cnt_c += ((dist_c % dil == 0) & (dist_c <= window)).astype(np.float32)
        cnt_n += ((dist_n >= 0) & (dist_n % dil == 0) & (dist_n <= window)).astype(np.float32)
    rep = lambda x: jnp.asarray(np.repeat(x, N_ATT_HEADS, axis=0))
    rows = lambda b: jnp.transpose(b, (0, 2, 1)).reshape(s_new * N_ATT_HEADS, -1)
    return (rows(_bias_rows(rel_bias, dist_c)), rows(_bias_rows(rel_bias, np.maximum(dist_n, 0))),
            rep(cnt_c), rep(cnt_n))


def _attn_sample(q, k, v, k_cache, v_cache, li, tables, n, s_new):
    win = k_cache.shape[-1]
    assert win == MAX_WINDOW
    lanes = np.arange(ATT_WIDTH)
    hm = jnp.asarray((np.arange(N_ATT_HEADS)[:, None] == lanes[None, :] // HEAD_DIM).astype(np.float32))
    new = pl.BlockSpec((None, s_new, ATT_WIDTH), lambda b: (b, 0, 0))
    buf = pl.BlockSpec((None, None, ATT_WIDTH, win), lambda b: (li, b, 0, 0))
    full = lambda arr: pl.BlockSpec(arr.shape, lambda b: (0, 0))
    r3 = lambda x: x.reshape(n, s_new, ATT_WIDTH)
    out = pl.pallas_call(
        functools.partial(_attn_sample_kernel, s_new=s_new),
        out_shape=jax.ShapeDtypeStruct((n, s_new, ATT_WIDTH), F32),
        grid=(n,),
        in_specs=[new, new, new, buf, buf] + [full(x) for x in tables] + [full(hm)],
        out_specs=new,
        scratch_shapes=[pltpu.VMEM((LANES, ATT_WIDTH), F32), pltpu.VMEM((LANES, ATT_WIDTH), F32)],
        compiler_params=_cparams(("arbitrary",)),
        name="attn_sample",
    )(r3(q), r3(k), r3(v), k_cache, v_cache, *tables, hm)
    return out.reshape(n * s_new, ATT_WIDTH)


def _mix_ffn_kernel(ya_ref, yb_ref, yc_ref, x_ref, gate1_ref, gmix_ref, wout_ref,
                    sc_ref, sh_ref, gate2_ref, gpre_ref, gpost_ref, wg_ref, wu_ref, wo_ref, *refs, residue_major):
    a = CONV_WIDTH
    b = CONV_WIDTH + RWKV_WIDTH
    if residue_major:
        permt_ref, o_ref = refs
        yc = jnp.concatenate([jnp.concatenate([yc_ref[p, r] for r in range(ATT_RES)], axis=0)
                              for p in range(N_PAIRS)], axis=1).astype(BF16)
        yc = lax.dot_general(permt_ref[...], yc, NN, preferred_element_type=F32)
    else:
        o_ref, = refs
        yc = jnp.concatenate([yc_ref[p] for p in range(N_PAIRS)], axis=1)
    mix = (_dot1(ya_ref[...], wout_ref[0:a, :]) + _dot1(yb_ref[...], wout_ref[a:b, :])
           + _dot1(yc, wout_ref[b:, :]))
    ms = jnp.mean(mix * mix, axis=-1, keepdims=True)
    x = x_ref[...] + gate1_ref[...] * (mix * lax.rsqrt(ms + RMS_EPS) * gmix_ref[...])

    ms = jnp.mean(x * x, axis=-1, keepdims=True)
    h = x * lax.rsqrt(ms + RMS_EPS) * gpre_ref[...]
    hb = (h * (1.0 + sc_ref[...]) + sh_ref[...]).astype(BF16)
    y = None
    for c in range(FFN_SPLIT):
        cols = slice(c * (D_FF // FFN_SPLIT), (c + 1) * (D_FF // FFN_SPLIT))
        gg = lax.dot_general(hb, wg_ref[:, cols], NN, preferred_element_type=F32)
        uu = lax.dot_general(hb, wu_ref[:, cols], NN, preferred_element_type=F32)
        act = (_silu(gg) * uu).astype(BF16)
        part = lax.dot_general(act, wo_ref[cols, :], NN, preferred_element_type=F32)
        y = part if y is None else y + part
    ms = jnp.mean(y * y, axis=-1, keepdims=True)
    o_ref[...] = x + gate2_ref[...] * (y * lax.rsqrt(ms + RMS_EPS) * gpost_ref[...])


def _mix_ffn(ya, yb, yc, x2, gate1, g_mix, w_out, scale2, shift2, gate2, g_pre, g_post, wg, wu, wo, tm,
             tiles_per_group):
    m = x2.shape[0]
    rows = lambda w: pl.BlockSpec((tm, w), lambda i: (i, 0))
    vec = pl.BlockSpec((1, D_MODEL), lambda i: (0, 0))
    resident = lambda arr: pl.BlockSpec(arr.shape, lambda i: (0, 0), pipeline_mode=pl.Buffered(1))
    residue_major = yc.ndim == 6
    extra, extra_specs = [], []
    if residue_major:
        t = yc.shape[2] * ATT_UNIT
        per_unit = ATT_UNIT // tm
        per_seq = t // tm
        yc_spec = pl.BlockSpec((N_PAIRS, None, None, ATT_RES, tm // ATT_RES, LANES),
                               lambda i: (0, i // per_seq, (i % per_seq) // per_unit, 0, i % per_unit, 0))
        _, permt = _tile_perm(tm)
        extra, extra_specs = [permt], [resident(permt)]
    else:
        yc_spec = pl.BlockSpec((N_PAIRS, tm, LANES), lambda i: (0, i, 0))
    mod = lambda arr: _mod_spec(arr, tiles_per_group)
    return pl.pallas_call(
        functools.partial(_mix_ffn_kernel, residue_major=residue_major),
        out_shape=jax.ShapeDtypeStruct((m, D_MODEL), F32),
        grid=(m // tm,),
        in_specs=[rows(CONV_WIDTH), rows(RWKV_WIDTH), yc_spec, rows(D_MODEL), mod(gate1), vec, resident(w_out),
                  mod(scale2), mod(shift2), mod(gate2), vec, vec, resident(wg), resident(wu), resident(wo)]
                 + extra_specs,
        out_specs=rows(D_MODEL),
        compiler_params=_cparams(("arbitrary",)),
        name="mix_ffn",
    )(ya, yb, yc, x2, gate1, g_mix.reshape(1, D_MODEL), w_out, scale2, shift2, gate2,
      g_pre.reshape(1, D_MODEL), g_post.reshape(1, D_MODEL), wg, wu, wo, *extra)


def _window_to_heads(xt):
    n, _, win = xt.shape
    return jnp.transpose(xt.reshape(n, N_ATT_HEADS, HEAD_DIM, win), (0, 3, 1, 2))


def _layer(x2, n, t, mods, lp, att, state, li):
    shift1, scale1, gate1, shift2, scale2, gate2 = mods
    prompt = state is None
    tm = 512 if prompt else n * t
    tpg = t // tm if prompt else 1
    uc, ur, q, k, v, *windows = _in_proj(x2, scale1, shift1, lp['g_pre_mix'], lp['w_in_bf'], tm, tpg,
                                         seq_shape=(n, t) if prompt else None)
    conv_params = _conv_params(lp['conv_w'], lp['conv_b'], lp['conv_ln_g'], lp['conv_ln_b'])
    rwkv_params = _rwkv_params(lp)
    ur3 = ur.reshape(n, t, RWKV_COLS)
    new_shift = ur3[:, -1]
    if prompt:
        ya, conv_cache = _conv_prompt(uc, n, t, conv_params)
        shift0 = jnp.zeros((n, 1, RWKV_COLS), F32)
        s0 = jnp.zeros((n, HEAD_DIM, RWKV_WIDTH), F32)
        yb, s_out = _rwkv(ur3, shift0, s0, rwkv_params, tr=512, t_valid=512, nseq=4)
        yc = _attn_prompt(q, k, v, att['prompt'])
        assert min(MAX_WINDOW, t) == ATT_UNIT
        new_k, new_v = (_window_to_heads(w) for w in windows)
    else:
        cache_conv, state_shift, state_wkv, k_cache, v_cache = state
        ya, conv_cache = _conv_sample(uc, cache_conv[li], n, t, conv_params)
        tr = RWKV_CHUNK
        ur_pad = jnp.pad(ur3, ((0, 0), (0, tr - t), (0, 0)))
        yb, s_out = _rwkv(ur_pad, state_shift[li][:, None, :], _state_to_lanes(state_wkv[li]), rwkv_params,
                          tr=tr, t_valid=t, nseq=8)
        yb = yb[:, :t]
        q_r, k_r, v_r = _pairs_to_rows(q), _pairs_to_rows(k), _pairs_to_rows(v)
        yc = _rows_to_pairs(_attn_sample(q_r, k_r, v_r, k_cache, v_cache, li, att['sample'], n, t))
        new_k = k_r.reshape(n, t, N_ATT_HEADS, HEAD_DIM)
        new_v = v_r.reshape(n, t, N_ATT_HEADS, HEAD_DIM)
    yb = yb.reshape(n * t, RWKV_WIDTH)
    x2 = _mix_ffn(ya, yb, yc, x2, gate1, lp['g_post_mix'], lp['w_out_bf'], scale2, shift2, gate2,
                  lp['g_pre_ffn'], lp['g_post_ffn'], lp['w_ffn_g_bf'], lp['w_ffn_u_bf'], lp['w_ffn_o_bf'], tm, tpg)
    return x2, (conv_cache, new_shift, _lanes_to_state(s_out), new_k, new_v)


def kernel(x_prompt, x_sample, c_prompt, c_sample, cache_conv, state_shift, state_wkv, cache_k_win, cache_v_win,
           w_ada, b_ada, g_pre_mix, g_post_mix, g_pre_ffn, g_post_ffn, w_in, w_out,
           conv_w, conv_b, conv_ln_g, conv_ln_b,
           rwkv_mu, rwkv_w0, rwkv_w2, rwkv_a0, rwkv_a2, rwkv_g2, rwkv_k_k, rwkv_k_a, rwkv_r_k,
           rwkv_ln_g, rwkv_ln_b, rel_bias, w_ffn_in, w_ffn_out):
    depth = w_in.shape[0]
    nb, tp, _ = x_prompt.shape
    ns, ts, _ = x_sample.shape
    rows = nb + ns
    rows_pad = -(-rows // SUBLANES) * SUBLANES
    c_all = jnp.pad(jnp.concatenate([c_prompt, c_sample], axis=0), ((0, rows_pad - rows), (0, 0)))
    mod = _ada_modulation(c_all, w_ada, b_ada)
    mod = mod.reshape(depth, rows_pad, 6, D_MODEL)

    win = cache_k_win.shape[2]
    att = {'prompt': _attn_prompt_bias(rel_bias), 'sample': _attn_sample_tables(rel_bias, ts, win)}
    to_pos_minor = lambda c: jnp.transpose(c, (0, 1, 3, 4, 2)).reshape(depth, ns, ATT_WIDTH, win)
    state = (cache_conv, state_shift, state_wkv, to_pos_minor(cache_k_win), to_pos_minor(cache_v_win))

    yp = x_prompt.reshape(nb * tp, D_MODEL)
    ys = x_sample.reshape(ns * ts, D_MODEL)
    outs_p, outs_s = [], []
    for li in range(depth):
        lp = dict(g_pre_mix=g_pre_mix[li], g_post_mix=g_post_mix[li], g_pre_ffn=g_pre_ffn[li],
                  g_post_ffn=g_post_ffn[li],
                  w_in_bf=w_in[li].astype(BF16), w_out_bf=w_out[li].astype(BF16),
                  w_ffn_g_bf=w_ffn_in[li, :, :D_FF].astype(BF16), w_ffn_u_bf=w_ffn_in[li, :, D_FF:].astype(BF16),
                  w_ffn_o_bf=w_ffn_out[li].astype(BF16),
                  conv_w=conv_w[li], conv_b=conv_b[li], conv_ln_g=conv_ln_g[li], conv_ln_b=conv_ln_b[li],
                  rwkv_mu=rwkv_mu[li], rwkv_w0=rwkv_w0[li], rwkv_w2=rwkv_w2[li], rwkv_a0=rwkv_a0[li],
                  rwkv_a2=rwkv_a2[li], rwkv_g2=rwkv_g2[li], rwkv_k_k=rwkv_k_k[li], rwkv_k_a=rwkv_k_a[li],
                  rwkv_r_k=rwkv_r_k[li], rwkv_ln_g=rwkv_ln_g[li], rwkv_ln_b=rwkv_ln_b[li])
        mods_p = tuple(mod[li, :nb, j][:, None, :] for j in range(6))
        mods_s = tuple(jnp.repeat(mod[li, nb:rows, j], ts, axis=0)[None] for j in range(6))
        yp, st_p = _layer(yp, nb, tp, mods_p, lp, att, None, li)
        ys, st_s = _layer(ys, ns, ts, mods_s, lp, att, state, li)
        outs_p.append(st_p)
        outs_s.append(st_s)
    stack = lambda outs, i: jnp.stack([o[i] for o in outs])
    return (yp.reshape(nb, tp, D_MODEL), ys.reshape(ns, ts, D_MODEL),
            stack(outs_p, 0), stack(outs_s, 0),
            stack(outs_p, 1), stack(outs_s, 1),
            stack(outs_p, 2), stack(outs_s, 2),
            stack(outs_p, 3), stack(outs_s, 3),
            stack(outs_p, 4), stack(outs_s, 4))
```

```python
import functools
import math

import numpy as np
import jax
import jax.numpy as jnp
from jax import lax
from jax.experimental import pallas as pl
from jax.experimental.pallas import tpu as pltpu

F32 = jnp.float32
BF16 = jnp.bfloat16

D_MODEL = 1024
HEAD_DIM = 64
CONV_WIDTH = 256
RWKV_WIDTH = 256
ATT_WIDTH = 512
N_RWKV_HEADS = RWKV_WIDTH // HEAD_DIM
N_ATT_HEADS = ATT_WIDTH // HEAD_DIM
CONV_K = 31
DECAY_LORA = 64
AAA_LORA = 64
GATE_LORA = 128
CONV_COLS = 2 * CONV_WIDTH
RWKV_COLS = 3 * RWKV_WIDTH + DECAY_LORA + AAA_LORA + GATE_LORA
ATT_COLS = 3 * ATT_WIDTH
IN_COLS = CONV_COLS + RWKV_COLS + ATT_COLS
DILATIONS = ((128, 1), (512, 4), (2048, 16))
MAX_WINDOW = 2048
BLOCK = 128
N_REL_BUCKETS = 32
REL_EXACT = N_REL_BUCKETS // 2
REL_MAX_DIST = MAX_WINDOW
D_FF = 2816
RMS_EPS = 1e-6
LN_EPS = 1e-5
RWKV_GN_EPS = HEAD_DIM * 1e-5
DECAY_SCALE = math.exp(-0.5)
NEG_INF = -1e30

VMEM_LIMIT_BYTES = 56 * 1024 * 1024
SUBLANES = 8
LANES = 128
CONV_HALO = 32
FFN_SPLIT = 2
RWKV_CHUNK = HEAD_DIM
N_PAIRS = ATT_WIDTH // LANES
ATT_RES = max(d for _, d in DILATIONS)
ATT_UNIT = BLOCK * ATT_RES
ATT_GROUP = (8, 8, 8)

NN = (((1,), (0,)), ((), ()))
NT = (((1,), (1,)), ((), ()))


def _cparams(sem):
    return pltpu.CompilerParams(dimension_semantics=sem, vmem_limit_bytes=VMEM_LIMIT_BYTES)


def _split_bf16(x, n):
    if x.dtype == BF16:
        return [x]
    pieces = []
    r = x
    for i in range(n):
        p = r.astype(BF16)
        pieces.append(p)
        if i + 1 < n:
            r = r - p.astype(F32)
    return pieces


def _mmp(ap, bp, dims=NN):
    order = max(len(ap), len(bp))
    out = None
    for i, x in enumerate(ap):
        for j, y in enumerate(bp):
            if i + j < order:
                t = lax.dot_general(x, y, dims, preferred_element_type=F32)
                out = t if out is None else out + t
    return out


def _dot1(a, b, dims=NN):
    return lax.dot_general(a.astype(BF16), b.astype(BF16), dims, preferred_element_type=F32)


def _lockstep(gens):
    results = [None] * len(gens)
    while any(r is None for r in results):
        results = [next(g) if r is None else r for g, r in zip(gens, results)]
    return results


def _sigmoid(x):
    return 1.0 / (1.0 + jnp.exp(-x))


def _silu(x):
    return x * _sigmoid(x)


def _rel_bucket_np(dist):
    d = np.maximum(dist, 1).astype(np.float32)
    large = REL_EXACT + (np.log(d / np.float32(REL_EXACT)) / np.float32(math.log(REL_MAX_DIST / REL_EXACT))
                         * np.float32(N_REL_BUCKETS - REL_EXACT)).astype(np.int32)
    large = np.minimum(large, N_REL_BUCKETS - 1)
    return np.where(dist < REL_EXACT, dist, large)


def _bias_rows(rel_bias, dist):
    idx = _rel_bucket_np(np.asarray(dist, np.int32))
    onehot = jnp.asarray(np.eye(N_REL_BUCKETS, dtype=np.float32)[idx.reshape(-1)])
    rows = jnp.dot(onehot, rel_bias, precision=lax.Precision.HIGHEST)
    return rows.reshape(idx.shape + (rel_bias.shape[1],))


def _ada_kernel(c_ref, w_ref, b_ref, o_ref):
    a = _silu(c_ref[...])
    o_ref[...] = _dot1(a, w_ref[...]) + b_ref[...]


def _ada_modulation(c_all, w_ada, b_ada):
    depth = w_ada.shape[0]
    rows = c_all.shape[0]
    tn = D_MODEL
    return pl.pallas_call(
        _ada_kernel,
        out_shape=jax.ShapeDtypeStruct((depth, rows, 6 * D_MODEL), F32),
        grid=(depth, 6 * D_MODEL // tn),
        in_specs=[
            pl.BlockSpec((rows, D_MODEL), lambda l, j: (0, 0)),
            pl.BlockSpec((None, D_MODEL, tn), lambda l, j: (l, 0, j)),
            pl.BlockSpec((None, 1, tn), lambda l, j: (l, 0, j)),
        ],
        out_specs=pl.BlockSpec((None, rows, tn), lambda l, j: (l, 0, j)),
        compiler_params=_cparams(("arbitrary", "arbitrary")),
        name="ada_modulation",
    )(c_all, w_ada, b_ada.reshape(depth, 1, 6 * D_MODEL))


def _in_proj_kernel(x_ref, sc_ref, sh_ref, g_ref, w_ref, *refs, residue_major, tiles_per_seq=None):
    if residue_major:
        perm_ref, wkt_ref, wvt_ref, uc_ref, ur_ref, q_ref, k_ref, v_ref, kt_ref, vt_ref = refs
    else:
        uc_ref, ur_ref, q_ref, k_ref, v_ref = refs
    x = x_ref[...]
    ms = jnp.mean(x * x, axis=-1, keepdims=True)
    h = x * lax.rsqrt(ms + RMS_EPS) * g_ref[...]
    h = h * (1.0 + sc_ref[...]) + sh_ref[...]
    hb = h.astype(BF16)
    uc_ref[...] = lax.dot_general(hb, w_ref[:, 0:CONV_COLS], NN, preferred_element_type=F32)
    o = CONV_COLS
    ur_ref[...] = lax.dot_general(hb, w_ref[:, o:o + RWKV_COLS], NN, preferred_element_type=F32)
    o += RWKV_COLS
    if residue_major:
        @pl.when(pl.program_id(0) % tiles_per_seq >= tiles_per_seq - ATT_UNIT // hb.shape[0])
        def _():
            kt_ref[...] = lax.dot_general(wkt_ref[...], hb, NT, preferred_element_type=F32)
            vt_ref[...] = lax.dot_general(wvt_ref[...], hb, NT, preferred_element_type=F32)

        hb = lax.dot_general(perm_ref[...], hb, NN, preferred_element_type=F32).astype(BF16)
        run = hb.shape[0] // ATT_RES
    for ref in (q_ref, k_ref, v_ref):
        res = lax.dot_general(hb, w_ref[:, o:o + ATT_WIDTH], NN, preferred_element_type=F32)
        for p in range(N_PAIRS):
            cols = res[:, p * LANES:(p + 1) * LANES]
            if residue_major:
                for r in range(ATT_RES):
                    ref[p, r] = cols[r * run:(r + 1) * run]
            else:
                ref[p] = cols
        o += ATT_WIDTH


def _tile_perm(tm):
    run = tm // ATT_RES
    t = np.arange(tm)
    perm = np.zeros((tm, tm), np.float32)
    perm[(t % ATT_RES) * run + t // ATT_RES, t] = 1.0
    return jnp.asarray(perm, BF16), jnp.asarray(perm.T, BF16)


def _mod_spec(mod, tiles_per_group):
    _, r, d = mod.shape
    return pl.BlockSpec((None, r, d), lambda i: (i // tiles_per_group, 0, 0))


def _in_proj(x2, scale, shift, g, w_bf, tm, tiles_per_group, seq_shape=None):
    m = x2.shape[0]
    flat = lambda w: jax.ShapeDtypeStruct((m, w), F32)
    residue_major = seq_shape is not None
    ins = [x2, scale, shift, g.reshape(1, D_MODEL), w_bf]
    in_specs = [
        pl.BlockSpec((tm, D_MODEL), lambda i: (i, 0)),
        _mod_spec(scale, tiles_per_group),
        _mod_spec(shift, tiles_per_group),
        pl.BlockSpec((1, D_MODEL), lambda i: (0, 0)),
        pl.BlockSpec((D_MODEL, IN_COLS), lambda i: (0, 0)),
    ]
    if residue_major:
        n, t = seq_shape
        per_unit = ATT_UNIT // tm
        per_seq = t // tm
        rows_res = ATT_UNIT // ATT_RES
        pairs = jax.ShapeDtypeStruct((N_PAIRS, n, t // ATT_UNIT, ATT_RES, rows_res, LANES), F32)
        pair_spec = pl.BlockSpec((N_PAIRS, None, None, ATT_RES, tm // ATT_RES, LANES),
                                 lambda i: (0, i // per_seq, (i % per_seq) // per_unit, 0, i % per_unit, 0))
        perm, _ = _tile_perm(tm)
        k0 = CONV_COLS + RWKV_COLS + ATT_WIDTH
        wkt = w_bf[:, k0:k0 + ATT_WIDTH].T
        wvt = w_bf[:, k0 + ATT_WIDTH:k0 + 2 * ATT_WIDTH].T
        ins += [perm, wkt, wvt]
        in_specs += [pl.BlockSpec((tm, tm), lambda i: (0, 0)),
                     pl.BlockSpec((ATT_WIDTH, D_MODEL), lambda i: (0, 0)),
                     pl.BlockSpec((ATT_WIDTH, D_MODEL), lambda i: (0, 0))]
        win_t = jax.ShapeDtypeStruct((n, ATT_WIDTH, ATT_UNIT), F32)
        win_spec = pl.BlockSpec((None, ATT_WIDTH, tm),
                                lambda i: (i // per_seq, 0, jnp.maximum(i % per_seq - (per_seq - per_unit), 0)))
        extra_out, extra_specs = (win_t, win_t), (win_spec, win_spec)
        kernel_fn = functools.partial(_in_proj_kernel, residue_major=True, tiles_per_seq=per_seq)
    else:
        pairs = jax.ShapeDtypeStruct((N_PAIRS, m, LANES), F32)
        pair_spec = pl.BlockSpec((N_PAIRS, tm, LANES), lambda i: (0, i, 0))
        extra_out, extra_specs = (), ()
        kernel_fn = functools.partial(_in_proj_kernel, residue_major=False)
    return pl.pallas_call(
        kernel_fn,
        out_shape=(flat(CONV_COLS), flat(RWKV_COLS), pairs, pairs, pairs) + extra_out,
        grid=(m // tm,),
        in_specs=in_specs,
        out_specs=(pl.BlockSpec((tm, CONV_COLS), lambda i: (i, 0)), pl.BlockSpec((tm, RWKV_COLS), lambda i: (i, 0)),
                   pair_spec, pair_spec, pair_spec) + extra_specs,
        compiler_params=_cparams(("arbitrary",)),
        name="in_proj",
    )(*ins)


def _pairs_to_rows(x):
    return jnp.transpose(x, (1, 0, 2)).reshape(x.shape[1], ATT_WIDTH)


def _rows_to_pairs(x):
    return jnp.transpose(x.reshape(x.shape[0], N_PAIRS, LANES), (1, 0, 2))


def _conv_core(zs_ref, zsh_ref, w_ref, b_ref, lg_ref, lb_ref, y_ref, tt):
    n_rows = CONV_HALO + tt
    zs_ref[n_rows:n_rows + SUBLANES, :] = jnp.zeros((SUBLANES, CONV_WIDTH), F32)
    for s in range(SUBLANES):
        zsh_ref[s] = zs_ref[pl.ds(s, n_rows), :]
    off = CONV_HALO - (CONV_K - 1)
    sub = min(tt, 64)
    for r0 in range(0, tt, sub):
        acc = jnp.zeros((sub, CONV_WIDTH), F32)
        for j in range(CONV_K):
            a, b = divmod(off + j, SUBLANES)
            acc = acc + w_ref[j:j + 1, :] * zsh_ref[b, pl.ds(r0 + SUBLANES * a, sub), :]
        y = acc + b_ref[...]
        mu = jnp.mean(y, axis=-1, keepdims=True)
        yc = y - mu
        var = jnp.mean(yc * yc, axis=-1, keepdims=True)
        yn = yc * lax.rsqrt(var + LN_EPS) * lg_ref[...] + lb_ref[...]
        y_ref[r0:r0 + sub, :] = _silu(yn)


def _glu(u):
    return u[:, :CONV_WIDTH] * _sigmoid(u[:, CONV_WIDTH:])


def _conv_prompt_kernel(u_ref, uh_ref, w_ref, b_ref, lg_ref, lb_ref, y_ref, zt_ref, zs_ref, zsh_ref, *, tt):
    j = pl.program_id(1)
    zh = _glu(uh_ref[...])
    zs_ref[0:CONV_HALO, :] = jnp.where(j > 0, zh, 0.0)
    zs_ref[CONV_HALO:CONV_HALO + tt, :] = _glu(u_ref[...])
    _conv_core(zs_ref, zsh_ref, w_ref, b_ref, lg_ref, lb_ref, y_ref, tt)
    zt_ref[...] = zs_ref[tt:tt + CONV_HALO, :]


def _conv_sample_kernel(u_ref, zh_ref, w_ref, b_ref, lg_ref, lb_ref, y_ref, z_ref, zs_ref, zsh_ref, *, tt):
    z = _glu(u_ref[...])
    zs_ref[0:CONV_HALO, :] = zh_ref[...]
    zs_ref[CONV_HALO:CONV_HALO + tt, :] = z
    _conv_core(zs_ref, zsh_ref, w_ref, b_ref, lg_ref, lb_ref, y_ref, tt)
    z_ref[...] = z


def _conv_param_specs(nargs_grid):
    cmap = (lambda n, j: (0, 0)) if nargs_grid == 2 else (lambda n: (0, 0))
    return [
        pl.BlockSpec((CONV_HALO, CONV_WIDTH), cmap),
        pl.BlockSpec((1, CONV_WIDTH), cmap),
        pl.BlockSpec((1, CONV_WIDTH), cmap),
        pl.BlockSpec((1, CONV_WIDTH), cmap),
    ]


def _conv_params(conv_w, conv_b, ln_g, ln_b):
    w = jnp.pad(conv_w, ((0, CONV_HALO - CONV_K), (0, 0)))
    return w, conv_b.reshape(1, -1), ln_g.reshape(1, -1), ln_b.reshape(1, -1)


def _conv_prompt(uc, n, t, params, tt=512):
    u3 = uc.reshape(n, t, CONV_COLS)
    hb = tt // CONV_HALO
    y, zt = pl.pallas_call(
        functools.partial(_conv_prompt_kernel, tt=tt),
        out_shape=(jax.ShapeDtypeStruct((n, t, CONV_WIDTH), F32),
                   jax.ShapeDtypeStruct((n, CONV_HALO, CONV_WIDTH), F32)),
        grid=(n, t // tt),
        in_specs=[
            pl.BlockSpec((None, tt, CONV_COLS), lambda b, j: (b, j, 0)),
            pl.BlockSpec((None, CONV_HALO, CONV_COLS), lambda b, j: (b, jnp.maximum(j * hb - 1, 0), 0)),
        ] + _conv_param_specs(2),
        out_specs=(pl.BlockSpec((None, tt, CONV_WIDTH), lambda b, j: (b, j, 0)),
                   pl.BlockSpec((None, CONV_HALO, CONV_WIDTH), lambda b, j: (b, 0, 0))),
        scratch_shapes=[pltpu.VMEM((CONV_HALO + tt + SUBLANES, CONV_WIDTH), F32),
                        pltpu.VMEM((SUBLANES, CONV_HALO + tt, CONV_WIDTH), F32)],
        compiler_params=_cparams(("arbitrary", "arbitrary")),
        name="conv_prompt",
    )(u3, u3, *params)
    return y.reshape(n * t, CONV_WIDTH), zt[:, CONV_HALO - (CONV_K - 1):]


def _conv_sample(uc, cache, n, t, params):
    tt = SUBLANES
    u3 = jnp.pad(uc.reshape(n, t, CONV_COLS), ((0, 0), (0, tt - t), (0, 0)))
    zh = jnp.pad(cache, ((0, 0), (CONV_HALO - (CONV_K - 1), 0), (0, 0)))
    y, z = pl.pallas_call(
        functools.partial(_conv_sample_kernel, tt=tt),
        out_shape=(jax.ShapeDtypeStruct((n, tt, CONV_WIDTH), F32),
                   jax.ShapeDtypeStruct((n, tt, CONV_WIDTH), F32)),
        grid=(n,),
        in_specs=[
            pl.BlockSpec((None, tt, CONV_COLS), lambda b: (b, 0, 0)),
            pl.BlockSpec((None, CONV_HALO, CONV_WIDTH), lambda b: (b, 0, 0)),
        ] + _conv_param_specs(1),
        out_specs=(pl.BlockSpec((None, tt, CONV_WIDTH), lambda b: (b, 0, 0)),
                   pl.BlockSpec((None, tt, CONV_WIDTH), lambda b: (b, 0, 0))),
        scratch_shapes=[pltpu.VMEM((CONV_HALO + tt + SUBLANES, CONV_WIDTH), F32),
                        pltpu.VMEM((SUBLANES, CONV_HALO + tt, CONV_WIDTH), F32)],
        compiler_params=_cparams(("arbitrary",)),
        name="conv_sample",
    )(u3, zh, *params)
    new_cache = jnp.concatenate([cache[:, t:], z[:, :t]], axis=1)
    return y[:, :t].reshape(n * t, CONV_WIDTH), new_cache


def _rwkv_consts():
    c = RWKV_CHUNK
    rows = np.arange(c)
    tri = (rows[None, :] <= rows[:, None]).astype(np.float32)
    lanes = np.arange(RWKV_WIDTH)
    bd = (lanes[:, None] // HEAD_DIM == lanes[None, :] // HEAD_DIM).astype(np.float32)
    return jnp.asarray(tri, BF16), jnp.asarray(bd, BF16)


def _rwkv_kernel(u_ref, up_ref, sh0_ref, s0_ref, mu_ref, w0_ref, wwa_ref, a0_ref, g2_ref, kk_ref, ka_ref, rk_ref,
                 lg_ref, lb_ref, tri_ref, bd_ref,
                 y_ref, so_ref,
                 us_ref, lw_ref, kn_ref, kb_ref, k2_ref, rr_ref, vv_ref, bon_ref, gate_ref, yy_ref, s_ref,
                 *, tr, t_valid, nseq):
    j = pl.program_id(1)
    c = RWKV_CHUNK
    bd = bd_ref[...]
    o = 3 * RWKV_WIDTH

    @pl.when(j == 0)
    def _():
        s_ref[...] = s0_ref[...]

    for sq in range(nseq):
        u = u_ref[sq]
        prev = jnp.where(j > 0, up_ref[sq], jnp.broadcast_to(sh0_ref[sq], (SUBLANES, RWKV_COLS)))
        us_ref[sq, 0:SUBLANES, :] = prev
        us_ref[sq, SUBLANES:SUBLANES + tr, :] = u
        u_prev = us_ref[sq, pl.ds(SUBLANES - 1, tr), :]
        xs = u + (u_prev - u) * mu_ref[...]
        r = xs[:, :RWKV_WIDTH]
        k = xs[:, RWKV_WIDTH:2 * RWKV_WIDTH]
        v = xs[:, 2 * RWKV_WIDTH:o]
        lo = xs[:, o:o + DECAY_LORA + AAA_LORA]
        lane = lax.broadcasted_iota(jnp.int32, lo.shape, 1)
        lo = jnp.where(lane < DECAY_LORA, jnp.tanh(lo), lo)
        wa = _dot1(lo, wwa_ref[...])
        g = _dot1(_sigmoid(xs[:, o + DECAY_LORA + AAA_LORA:]), g2_ref[...])
        lw = -DECAY_SCALE * _sigmoid(w0_ref[...] + wa[:, :RWKV_WIDTH])
        a = _sigmoid(a0_ref[...] + wa[:, RWKV_WIDTH:])
        kkr = k * kk_ref[...]
        ss = _dot1(kkr * kkr, bd)
        kk = kkr * lax.rsqrt(jnp.maximum(ss, 1e-24))
        k2 = k * (1.0 + (a - 1.0) * ka_ref[...])
        if t_valid < tr:
            valid = lax.broadcasted_iota(jnp.int32, (tr, RWKV_WIDTH), 0) < t_valid
            zero = jnp.zeros_like(r)
            r, k2, v, kk, lw = (jnp.where(valid, z, zero) for z in (r, k2, v, kk, lw))
        lw_ref[sq] = lw
        kn_ref[sq] = -kk
        kb_ref[sq] = kk * a
        k2_ref[sq] = k2
        rr_ref[sq] = r
        vv_ref[sq] = v
        bon_ref[sq] = _dot1(r * k2 * rk_ref[...], bd) * v
        gate_ref[sq] = g

    row = lax.broadcasted_iota(jnp.int32, (c, LANES), 0)
    col = lax.broadcasted_iota(jnp.int32, (c, LANES), 1) & (HEAD_DIM - 1)
    strict = col < row
    incl = col <= row
    ident = jnp.where(col == row, 1.0, 0.0)
    bd2 = bd[0:LANES, 0:LANES]
    bd2_mask = bd2 > 0.5
    n_steps = int(round(math.log2(c)))

    def dot(a, b, dims=NN):
        return lax.dot_general(a.astype(BF16), b, dims, preferred_element_type=F32)

    def stack(x):
        xb = x.astype(BF16)
        return jnp.concatenate([xb, xb], axis=0) * bd2

    def rows2(a, b):
        return jnp.concatenate([a, b], axis=0)

    def chunk_pair(pt, bt, kt, rt, vc, gc, s0):
        zero = jnp.zeros((c, LANES), F32)
        pr = rows2(pt, rt)
        xb = dot(pr, stack(bt), NT)
        xk = dot(pr, stack(kt), NT)
        yield None
        lm = jnp.where(strict, xb[:c], zero)
        mm_ = jnp.where(strict, xk[:c], zero)
        qb = jnp.where(incl, xb[c:], zero)
        qk = jnp.where(incl, xk[c:], zero)
        tinv = ident + lm
        lk = dot(lm, stack(lm))
        mq = dot(rows2(mm_, qk), stack(vc))
        yield None
        for step in range(1, n_steps):
            if step + 1 < n_steps:
                both = dot(rows2(lk, tinv), stack(lk))
                lk = both[:c]
                tinv = tinv + both[c:]
            else:
                tinv = tinv + dot(tinv, stack(lk))
            yield None
        p2 = dot(tinv, stack(pt))
        u0 = dot(tinv, stack(mq[:c]))
        yield None
        xs_ = dot(rows2(p2, rt), stack(s0), NT)
        yield None
        uu = xs_[:c] + u0
        y = xs_[c:] + dot(qb, stack(uu)) + mq[c:]
        uvt = rows2(uu, vc).T
        bk = rows2(bt * gc, kt * gc).astype(BF16)
        z = jnp.where(bd2_mask, dot(uvt, bk), 0.0)
        yield y, s0 * gc + z[0:c] + z[c:2 * c]

    def chunk(ci, carry):
        sl = pl.ds(pl.multiple_of(ci * c, c), c)
        work = []
        for sq in range(nseq):
            lw_c = lw_ref[sq, sl, :]
            gi = _mmp([tri_ref[...]], _split_bf16(lw_c, 3))
            e_neg = jnp.exp(-gi)
            gc = jnp.exp(gi[c - 1:c, :])
            pt = kn_ref[sq, sl, :] * jnp.exp(gi - lw_c)
            bt = kb_ref[sq, sl, :] * e_neg
            kt = k2_ref[sq, sl, :] * e_neg
            rt = rr_ref[sq, sl, :] * jnp.exp(gi)
            vc = vv_ref[sq, sl, :]
            s0 = s_ref[sq]
            for p in range(RWKV_WIDTH // LANES):
                ln = slice(p * LANES, (p + 1) * LANES)
                work.append((sq, ln, tuple(x[:, ln] for x in (pt, bt, kt, rt, vc, gc, s0))))
        results = _lockstep([chunk_pair(*args) for _, _, args in work])
        done = [(sq, ln, res) for (sq, ln, _), res in zip(work, results)]
        for sq, ln, (y, s_new) in done:
            yy_ref[sq, sl, ln] = y
            s_ref[sq, :, ln] = s_new
        return carry

    lax.fori_loop(0, tr // c, chunk, 0)

    inv = 1.0 / HEAD_DIM
    for sq in range(nseq):
        y = yy_ref[sq]
        mu = _dot1(y, bd) * inv
        yc = y - mu
        var = _dot1(yc * yc, bd) * inv
        yn = yc * lax.rsqrt(var + RWKV_GN_EPS) * lg_ref[...] + lb_ref[...]
        y_ref[sq] = (yn + bon_ref[sq]) * gate_ref[sq]
    so_ref[...] = s_ref[...]


def _rwkv_params(p):
    z = jnp.zeros((DECAY_LORA, RWKV_WIDTH), F32)
    wwa = jnp.concatenate([jnp.concatenate([p['rwkv_w2'], z], axis=1),
                           jnp.concatenate([z, p['rwkv_a2']], axis=1)], axis=0)
    row = lambda x: x.reshape(1, -1)
    return (row(p['rwkv_mu']), row(p['rwkv_w0']), wwa, row(p['rwkv_a0']), p['rwkv_g2'], row(p['rwkv_k_k']),
            row(p['rwkv_k_a']), row(p['rwkv_r_k']), row(p['rwkv_ln_g']), row(p['rwkv_ln_b']))


def _rwkv(u3, shift0, s0, params, tr, t_valid, nseq):
    n, t, _ = u3.shape
    assert n % nseq == 0
    consts = _rwkv_consts()
    hb = tr // SUBLANES
    const2 = lambda b, j: (0, 0)
    full = lambda arr: pl.BlockSpec(arr.shape, const2)
    tile = lambda: pltpu.VMEM((nseq, tr, RWKV_WIDTH), F32)
    y, s_out = pl.pallas_call(
        functools.partial(_rwkv_kernel, tr=tr, t_valid=t_valid, nseq=nseq),
        out_shape=(jax.ShapeDtypeStruct((n, t, RWKV_WIDTH), F32),
                   jax.ShapeDtypeStruct((n, HEAD_DIM, RWKV_WIDTH), F32)),
        grid=(n // nseq, t // tr),
        in_specs=[
            pl.BlockSpec((nseq, tr, RWKV_COLS), lambda b, j: (b, j, 0)),
            pl.BlockSpec((nseq, SUBLANES, RWKV_COLS), lambda b, j: (b, jnp.maximum(j * hb - 1, 0), 0)),
            pl.BlockSpec((nseq, 1, RWKV_COLS), lambda b, j: (b, 0, 0)),
            pl.BlockSpec((nseq, HEAD_DIM, RWKV_WIDTH), lambda b, j: (b, 0, 0)),
        ] + [full(x) for x in params] + [full(x) for x in consts],
        out_specs=(pl.BlockSpec((nseq, tr, RWKV_WIDTH), lambda b, j: (b, j, 0)),
                   pl.BlockSpec((nseq, HEAD_DIM, RWKV_WIDTH), lambda b, j: (b, 0, 0))),
        scratch_shapes=[pltpu.VMEM((nseq, SUBLANES + tr, RWKV_COLS), F32)] + [tile() for _ in range(9)]
                       + [pltpu.VMEM((nseq, HEAD_DIM, RWKV_WIDTH), F32)],
        compiler_params=_cparams(("arbitrary", "arbitrary")),
        name="rwkv7",
    )(u3, u3, shift0, s0, *params, *consts)
    return y, s_out


def _state_to_lanes(s):
    n = s.shape[0]
    return jnp.transpose(s, (0, 2, 1, 3)).reshape(n, HEAD_DIM, RWKV_WIDTH)


def _lanes_to_state(s):
    n = s.shape[0]
    return jnp.transpose(s.reshape(n, HEAD_DIM, N_RWKV_HEADS, HEAD_DIM), (0, 2, 1, 3))


def _att_natural_index(dil):
    runs = ATT_RES // dil
    p = np.arange(BLOCK)
    return runs * (p % (BLOCK // runs)) + p // (BLOCK // runs)


def _attn_prompt_kernel(q_ref, kc_ref, kp_ref, vc_ref, vp_ref, bias_ref, o_ref, kcat, vcat, acc, m0, m1, l0, l1):
    u = pl.program_id(2)
    unit = ATT_UNIT
    scale = HEAD_DIM ** -0.5
    kcat[0:unit, :] = kp_ref[...]
    kcat[unit:2 * unit, :] = kc_ref[...]
    vcat[0:unit, :] = vp_ref[...]
    vcat[unit:2 * unit, :] = vc_ref[...]
    lo = lax.broadcasted_iota(jnp.int32, (BLOCK, LANES), 1) < HEAD_DIM
    n_br = len(DILATIONS)
    rows2 = lambda a, b: jnp.concatenate([a, b], axis=0)
    for bi, (window, dil) in enumerate(DILATIONS):
        first = bi == 0
        last = bi == n_br - 1
        shift = int(round(math.log2(dil)))
        n_runs = ATT_RES // dil
        run_len = BLOCK // n_runs
        n_blk = unit // (BLOCK * dil)

        def runs_of(r, g):
            base = (g // n_blk) * unit + (g % n_blk) * run_len
            return [base + (dil * m + r) * BLOCK for m in range(n_runs)]

        def gather(ref, starts):
            return jnp.concatenate([ref[pl.ds(pl.multiple_of(s0, SUBLANES), run_len), :] for s0 in starts], axis=0)

        def sub_block(j, bi, dil, shift, first, last):
            r = j & (dil - 1)
            blk = j >> shift
            q_rows = runs_of(r, blk)
            k_rows = runs_of(r, n_blk + blk - 1) + runs_of(r, n_blk + blk)
            q = gather(q_ref, q_rows) * scale
            kk = gather(kcat, k_rows).astype(BF16)
            vv = gather(vcat, k_rows).astype(BF16)
            q2 = rows2(jnp.where(lo, q, 0.0), jnp.where(lo, 0.0, q)).astype(BF16)
            s = lax.dot_general(q2, kk, NT, preferred_element_type=F32)
            if not first:
                m_old = rows2(gather(m0, q_rows), gather(m1, q_rows))
                l_old = rows2(gather(l0, q_rows), gather(l1, q_rows))
                acc_old = gather(acc, q_rows)
            yield None
            no_prev = ((u == 0) & (blk == 0)).astype(jnp.int32)
            s = s + bias_ref[bi + n_br * no_prev]
            m_cur = jnp.max(s, axis=-1, keepdims=True)
            if first:
                m_new = jnp.broadcast_to(m_cur, (2 * BLOCK, LANES))
            else:
                m_new = jnp.maximum(m_old, m_cur)
                alpha = jnp.exp(m_old - m_new)
            p = jnp.exp(s - jnp.concatenate([m_new, m_new], axis=1))
            l_new = jnp.sum(p, axis=-1, keepdims=True)
            if first:
                l_new = jnp.broadcast_to(l_new, (2 * BLOCK, LANES))
            else:
                l_new = alpha * l_old + l_new
            o2 = lax.dot_general(p.astype(BF16), vv, NN, preferred_element_type=F32)
            yield None
            o_pair = jnp.where(lo, o2[:BLOCK], o2[BLOCK:])
            if not first:
                o_pair = o_pair + jnp.where(lo, alpha[:BLOCK], alpha[BLOCK:]) * acc_old
            if last:
                linv = 1.0 / l_new
                yield [(o_ref, q_rows, o_pair * jnp.where(lo, linv[:BLOCK], linv[BLOCK:]))]
            else:
                yield [(m0, q_rows, m_new[:BLOCK]), (m1, q_rows, m_new[BLOCK:]),
                       (l0, q_rows, l_new[:BLOCK]), (l1, q_rows, l_new[BLOCK:]), (acc, q_rows, o_pair)]

        group = ATT_GROUP[bi]

        def body(jg, carry, bi=bi, dil=dil, shift=shift, first=first, last=last, group=group, run_len=run_len):
            groups = _lockstep([sub_block(jg * group + g, bi, dil, shift, first, last) for g in range(group)])
            for stores in groups:
                for ref, starts, val in stores:
                    for m, s0 in enumerate(starts):
                        ref[pl.ds(pl.multiple_of(s0, SUBLANES), run_len), :] = val[m * run_len:(m + 1) * run_len]
            return carry

        lax.fori_loop(0, unit // BLOCK // group, body, 0)


def _attn_prompt_bias(rel_bias):
    tabs = []
    for _, dil in DILATIONS:
        a = _att_natural_index(dil)
        c = np.concatenate([a, BLOCK + a])
        delta = a[:, None] + BLOCK - c[None, :]
        band = (delta >= 0) & (delta <= BLOCK)
        b = _bias_rows(rel_bias, np.clip(delta, 0, BLOCK) * dil)
        b = jnp.where(jnp.asarray(band)[:, :, None], b, NEG_INF)
        tabs.append(jnp.transpose(b, (2, 0, 1)))
    w = jnp.stack(tabs)
    first = jnp.where(jnp.asarray(np.arange(2 * BLOCK) < BLOCK), NEG_INF, w)
    w = jnp.concatenate([w, first], axis=0)
    w = w.reshape(2 * len(DILATIONS), N_PAIRS, 2 * BLOCK, 2 * BLOCK)
    return jnp.transpose(w, (1, 0, 2, 3))


def _attn_prompt(q, k, v, btab):
    unit = ATT_UNIT
    n_pairs, n, units = q.shape[:3]
    r5 = lambda x: x.reshape(n_pairs, n, units, unit, LANES)
    cur = pl.BlockSpec((None, None, None, unit, LANES), lambda a, p, u: (p, a, u, 0, 0))
    prv = pl.BlockSpec((None, None, None, unit, LANES), lambda a, p, u: (p, a, jnp.maximum(u - 1, 0), 0, 0))
    stat = lambda: pltpu.VMEM((unit, LANES), F32)
    out = pl.pallas_call(
        _attn_prompt_kernel,
        out_shape=jax.ShapeDtypeStruct((n_pairs, n, units, unit, LANES), F32),
        grid=(n, n_pairs, units),
        in_specs=[cur, cur, prv, cur, prv,
                  pl.BlockSpec((None, 2 * len(DILATIONS), 2 * BLOCK, 2 * BLOCK), lambda a, p, u: (p, 0, 0, 0))],
        out_specs=cur,
        scratch_shapes=[pltpu.VMEM((2 * unit, LANES), F32), pltpu.VMEM((2 * unit, LANES), F32),
                        stat(), stat(), stat(), stat(), stat()],
        compiler_params=_cparams(("arbitrary", "arbitrary", "arbitrary")),
        name="attn_prompt",
    )(r5(q), r5(k), r5(k), r5(v), r5(v), btab)
    return out.reshape(q.shape)


def _attn_sample_kernel(q_ref, kn_ref, vn_ref, kc_ref, vc_ref, bt_ref, bn_ref, cnt_ref, cn_ref, hm_ref, o_ref,
                        kn_s, vn_s, *, s_new):
    scale = HEAD_DIM ** -0.5
    hm = hm_ref[...]
    qm = jnp.concatenate([(q_ref[s:s + 1, :] * scale) * hm for s in range(s_new)], axis=0).astype(BF16)
    kn_s[...] = jnp.zeros_like(kn_s)
    vn_s[...] = jnp.zeros_like(vn_s)
    kn_s[0:s_new, :] = kn_ref[...]
    vn_s[0:s_new, :] = vn_ref[...]
    sc = lax.dot_general(qm, kc_ref[...].astype(BF16), NN, preferred_element_type=F32) + bt_ref[...]
    sn = lax.dot_general(qm, kn_s[...].astype(BF16), NT, preferred_element_type=F32) + bn_ref[...]
    cnt = cnt_ref[...]
    cn = cn_ref[...]
    sc = jnp.where(cnt > 0.0, sc, NEG_INF)
    sn = jnp.where(cn > 0.0, sn, NEG_INF)
    m = jnp.maximum(jnp.max(sc, axis=-1, keepdims=True), jnp.max(sn, axis=-1, keepdims=True))
    p = cnt * jnp.exp(sc - m)
    pn = cn * jnp.exp(sn - m)
    l = jnp.sum(p, axis=-1, keepdims=True) + jnp.sum(pn, axis=-1, keepdims=True)
    o = (lax.dot_general(p.astype(BF16), vc_ref[...].astype(BF16), NT, preferred_element_type=F32)
         + lax.dot_general(pn.astype(BF16), vn_s[...].astype(BF16), NN, preferred_element_type=F32))
    o = o / l
    for s in range(s_new):
        o_ref[s:s + 1, :] = jnp.sum(o[s * N_ATT_HEADS:(s + 1) * N_ATT_HEADS] * hm, axis=0, keepdims=True)


def _attn_sample_tables(rel_bias, s_new, win):
    s = np.arange(s_new)[:, None]
    dist_c = win + s - np.arange(win)[None, :]
    dist_n = s - np.arange(LANES)[None, :]
    cnt_c = np.zeros(dist_c.shape, np.float32)
    cnt_n = np.zeros(dist_n.shape, np.float32)
    for window, dil in DILATIONS:
        cnt_c += ((dist_c % dil == 0) & (dist_c <= window)).astype(np.float32)
        cnt_n += ((dist_n >= 0) & (dist_n % dil == 0) & (dist_n <= window)).astype(np.float32)
    rep = lambda x: jnp.asarray(np.repeat(x, N_ATT_HEADS, axis=0))
    rows = lambda b: jnp.transpose(b, (0, 2, 1)).reshape(s_new * N_ATT_HEADS, -1)
    return (rows(_bias_rows(rel_bias, dist_c)), rows(_bias_rows(rel_bias, np.maximum(dist_n, 0))),
            rep(cnt_c), rep(cnt_n))


def _attn_sample(q, k, v, k_cache, v_cache, li, tables, n, s_new):
    win = k_cache.shape[-1]
    assert win == MAX_WINDOW
    lanes = np.arange(ATT_WIDTH)
    hm = jnp.asarray((np.arange(N_ATT_HEADS)[:, None] == lanes[None, :] // HEAD_DIM).astype(np.float32))
    new = pl.BlockSpec((None, s_new, ATT_WIDTH), lambda b: (b, 0, 0))
    buf = pl.BlockSpec((None, None, ATT_WIDTH, win), lambda b: (li, b, 0, 0))
    full = lambda arr: pl.BlockSpec(arr.shape, lambda b: (0, 0))
    r3 = lambda x: x.reshape(n, s_new, ATT_WIDTH)
    out = pl.pallas_call(
        functools.partial(_attn_sample_kernel, s_new=s_new),
        out_shape=jax.ShapeDtypeStruct((n, s_new, ATT_WIDTH), F32),
        grid=(n,),
        in_specs=[new, new, new, buf, buf] + [full(x) for x in tables] + [full(hm)],
        out_specs=new,
        scratch_shapes=[pltpu.VMEM((LANES, ATT_WIDTH), F32), pltpu.VMEM((LANES, ATT_WIDTH), F32)],
        compiler_params=_cparams(("arbitrary",)),
        name="attn_sample",
    )(r3(q), r3(k), r3(v), k_cache, v_cache, *tables, hm)
    return out.reshape(n * s_new, ATT_WIDTH)


def _mix_ffn_kernel(ya_ref, yb_ref, yc_ref, x_ref, gate1_ref, gmix_ref, wout_ref,
                    sc_ref, sh_ref, gate2_ref, gpre_ref, gpost_ref, wg_ref, wu_ref, wo_ref, *refs, residue_major):
    a = CONV_WIDTH
    b = CONV_WIDTH + RWKV_WIDTH
    if residue_major:
        permt_ref, o_ref = refs
        yc = jnp.concatenate([jnp.concatenate([yc_ref[p, r] for r in range(ATT_RES)], axis=0)
                              for p in range(N_PAIRS)], axis=1).astype(BF16)
        yc = lax.dot_general(permt_ref[...], yc, NN, preferred_element_type=F32)
    else:
        o_ref, = refs
        yc = jnp.concatenate([yc_ref[p] for p in range(N_PAIRS)], axis=1)
    mix = (_dot1(ya_ref[...], wout_ref[0:a, :]) + _dot1(yb_ref[...], wout_ref[a:b, :])
           + _dot1(yc, wout_ref[b:, :]))
    ms = jnp.mean(mix * mix, axis=-1, keepdims=True)
    x = x_ref[...] + gate1_ref[...] * (mix * lax.rsqrt(ms + RMS_EPS) * gmix_ref[...])

    ms = jnp.mean(x * x, axis=-1, keepdims=True)
    h = x * lax.rsqrt(ms + RMS_EPS) * gpre_ref[...]
    hb = (h * (1.0 + sc_ref[...]) + sh_ref[...]).astype(BF16)
    y = None
    for c in range(FFN_SPLIT):
        cols = slice(c * (D_FF // FFN_SPLIT), (c + 1) * (D_FF // FFN_SPLIT))
        gg = lax.dot_general(hb, wg_ref[:, cols], NN, preferred_element_type=F32)
        uu = lax.dot_general(hb, wu_ref[:, cols], NN, preferred_element_type=F32)
        act = (_silu(gg) * uu).astype(BF16)
        part = lax.dot_general(act, wo_ref[cols, :], NN, preferred_element_type=F32)
        y = part if y is None else y + part
    ms = jnp.mean(y * y, axis=-1, keepdims=True)
    o_ref[...] = x + gate2_ref[...] * (y * lax.rsqrt(ms + RMS_EPS) * gpost_ref[...])


def _mix_ffn(ya, yb, yc, x2, gate1, g_mix, w_out, scale2, shift2, gate2, g_pre, g_post, wg, wu, wo, tm,
             tiles_per_group):
    m = x2.shape[0]
    rows = lambda w: pl.BlockSpec((tm, w), lambda i: (i, 0))
    vec = pl.BlockSpec((1, D_MODEL), lambda i: (0, 0))
    resident = lambda arr: pl.BlockSpec(arr.shape, lambda i: (0, 0), pipeline_mode=pl.Buffered(1))
    residue_major = yc.ndim == 6
    extra, extra_specs = [], []
    if residue_major:
        t = yc.shape[2] * ATT_UNIT
        per_unit = ATT_UNIT // tm
        per_seq = t // tm
        yc_spec = pl.BlockSpec((N_PAIRS, None, None, ATT_RES, tm // ATT_RES, LANES),
                               lambda i: (0, i // per_seq, (i % per_seq) // per_unit, 0, i % per_unit, 0))
        _, permt = _tile_perm(tm)
        extra, extra_specs = [permt], [resident(permt)]
    else:
        yc_spec = pl.BlockSpec((N_PAIRS, tm, LANES), lambda i: (0, i, 0))
    mod = lambda arr: _mod_spec(arr, tiles_per_group)
    return pl.pallas_call(
        functools.partial(_mix_ffn_kernel, residue_major=residue_major),
        out_shape=jax.ShapeDtypeStruct((m, D_MODEL), F32),
        grid=(m // tm,),
        in_specs=[rows(CONV_WIDTH), rows(RWKV_WIDTH), yc_spec, rows(D_MODEL), mod(gate1), vec, resident(w_out),
                  mod(scale2), mod(shift2), mod(gate2), vec, vec, resident(wg), resident(wu), resident(wo)]
                 + extra_specs,
        out_specs=rows(D_MODEL),
        compiler_params=_cparams(("arbitrary",)),
        name="mix_ffn",
    )(ya, yb, yc, x2, gate1, g_mix.reshape(1, D_MODEL), w_out, scale2, shift2, gate2,
      g_pre.reshape(1, D_MODEL), g_post.reshape(1, D_MODEL), wg, wu, wo, *extra)


def _window_to_heads(xt):
    n, _, win = xt.shape
    return jnp.transpose(xt.reshape(n, N_ATT_HEADS, HEAD_DIM, win), (0, 3, 1, 2))


def _layer(x2, n, t, mods, lp, att, state, li):
    shift1, scale1, gate1, shift2, scale2, gate2 = mods
    prompt = state is None
    tm = 512 if prompt else n * t
    tpg = t // tm if prompt else 1
    uc, ur, q, k, v, *windows = _in_proj(x2, scale1, shift1, lp['g_pre_mix'], lp['w_in_bf'], tm, tpg,
                                         seq_shape=(n, t) if prompt else None)
    conv_params = _conv_params(lp['conv_w'], lp['conv_b'], lp['conv_ln_g'], lp['conv_ln_b'])
    rwkv_params = _rwkv_params(lp)
    ur3 = ur.reshape(n, t, RWKV_COLS)
    new_shift = ur3[:, -1]
    if prompt:
        ya, conv_cache = _conv_prompt(uc, n, t, conv_params)
        shift0 = jnp.zeros((n, 1, RWKV_COLS), F32)
        s0 = jnp.zeros((n, HEAD_DIM, RWKV_WIDTH), F32)
        yb, s_out = _rwkv(ur3, shift0, s0, rwkv_params, tr=512, t_valid=512, nseq=4)
        yc = _attn_prompt(q, k, v, att['prompt'])
        assert min(MAX_WINDOW, t) == ATT_UNIT
        new_k, new_v = (_window_to_heads(w) for w in windows)
    else:
        cache_conv, state_shift, state_wkv, k_cache, v_cache = state
        ya, conv_cache = _conv_sample(uc, cache_conv[li], n, t, conv_params)
        tr = RWKV_CHUNK
        ur_pad = jnp.pad(ur3, ((0, 0), (0, tr - t), (0, 0)))
        yb, s_out = _rwkv(ur_pad, state_shift[li][:, None, :], _state_to_lanes(state_wkv[li]), rwkv_params,
                          tr=tr, t_valid=t, nseq=8)
        yb = yb[:, :t]
        q_r, k_r, v_r = _pairs_to_rows(q), _pairs_to_rows(k), _pairs_to_rows(v)
        yc = _rows_to_pairs(_attn_sample(q_r, k_r, v_r, k_cache, v_cache, li, att['sample'], n, t))
        new_k = k_r.reshape(n, t, N_ATT_HEADS, HEAD_DIM)
        new_v = v_r.reshape(n, t, N_ATT_HEADS, HEAD_DIM)
    yb = yb.reshape(n * t, RWKV_WIDTH)
    x2 = _mix_ffn(ya, yb, yc, x2, gate1, lp['g_post_mix'], lp['w_out_bf'], scale2, shift2, gate2,
                  lp['g_pre_ffn'], lp['g_post_ffn'], lp['w_ffn_g_bf'], lp['w_ffn_u_bf'], lp['w_ffn_o_bf'], tm, tpg)
    return x2, (conv_cache, new_shift, _lanes_to_state(s_out), new_k, new_v)


def kernel(x_prompt, x_sample, c_prompt, c_sample, cache_conv, state_shift, state_wkv, cache_k_win, cache_v_win,
           w_ada, b_ada, g_pre_mix, g_post_mix, g_pre_ffn, g_post_ffn, w_in, w_out,
           conv_w, conv_b, conv_ln_g, conv_ln_b,
           rwkv_mu, rwkv_w0, rwkv_w2, rwkv_a0, rwkv_a2, rwkv_g2, rwkv_k_k, rwkv_k_a, rwkv_r_k,
           rwkv_ln_g, rwkv_ln_b, rel_bias, w_ffn_in, w_ffn_out):
    depth = w_in.shape[0]
    nb, tp, _ = x_prompt.shape
    ns, ts, _ = x_sample.shape
    rows = nb + ns
    rows_pad = -(-rows // SUBLANES) * SUBLANES
    c_all = jnp.pad(jnp.concatenate([c_prompt, c_sample], axis=0), ((0, rows_pad - rows), (0, 0)))
    mod = _ada_modulation(c_all, w_ada, b_ada)
    mod = mod.reshape(depth, rows_pad, 6, D_MODEL)

    win = cache_k_win.shape[2]
    att = {'prompt': _attn_prompt_bias(rel_bias), 'sample': _attn_sample_tables(rel_bias, ts, win)}
    to_pos_minor = lambda c: jnp.transpose(c, (0, 1, 3, 4, 2)).reshape(depth, ns, ATT_WIDTH, win)
    state = (cache_conv, state_shift, state_wkv, to_pos_minor(cache_k_win), to_pos_minor(cache_v_win))

    yp = x_prompt.reshape(nb * tp, D_MODEL)
    ys = x_sample.reshape(ns * ts, D_MODEL)
    outs_p, outs_s = [], []
    for li in range(depth):
        lp = dict(g_pre_mix=g_pre_mix[li], g_post_mix=g_post_mix[li], g_pre_ffn=g_pre_ffn[li],
                  g_post_ffn=g_post_ffn[li],
                  w_in_bf=w_in[li].astype(BF16), w_out_bf=w_out[li].astype(BF16),
                  w_ffn_g_bf=w_ffn_in[li, :, :D_FF].astype(BF16), w_ffn_u_bf=w_ffn_in[li, :, D_FF:].astype(BF16),
                  w_ffn_o_bf=w_ffn_out[li].astype(BF16),
                  conv_w=conv_w[li], conv_b=conv_b[li], conv_ln_g=conv_ln_g[li], conv_ln_b=conv_ln_b[li],
                  rwkv_mu=rwkv_mu[li], rwkv_w0=rwkv_w0[li], rwkv_w2=rwkv_w2[li], rwkv_a0=rwkv_a0[li],
                  rwkv_a2=rwkv_a2[li], rwkv_g2=rwkv_g2[li], rwkv_k_k=rwkv_k_k[li], rwkv_k_a=rwkv_k_a[li],
                  rwkv_r_k=rwkv_r_k[li], rwkv_ln_g=rwkv_ln_g[li], rwkv_ln_b=rwkv_ln_b[li])
        mods_p = tuple(mod[li, :nb, j][:, None, :] for j in range(6))
        mods_s = tuple(jnp.repeat(mod[li, nb:rows, j], ts, axis=0)[None] for j in range(6))
        yp, st_p = _layer(yp, nb, tp, mods_p, lp, att, None, li)
        ys, st_s = _layer(ys, ns, ts, mods_s, lp, att, state, li)
        outs_p.append(st_p)
        outs_s.append(st_s)
    stack = lambda outs, i: jnp.stack([o[i] for o in outs])
    return (yp.reshape(nb, tp, D_MODEL), ys.reshape(ns, ts, D_MODEL),
            stack(outs_p, 0), stack(outs_s, 0),
            stack(outs_p, 1), stack(outs_s, 1),
            stack(outs_p, 2), stack(outs_s, 2),
            stack(outs_p, 3), stack(outs_s, 3),
            stack(outs_p, 4), stack(outs_s, 4))
```

```python
import functools
import math

import numpy as np
import jax
import jax.numpy as jnp
from jax import lax
from jax.experimental import pallas as pl
from jax.experimental.pallas import tpu as pltpu

F32 = jnp.float32
BF16 = jnp.bfloat16

D_MODEL = 1024
HEAD_DIM = 64
CONV_WIDTH = 256
RWKV_WIDTH = 256
ATT_WIDTH = 512
N_RWKV_HEADS = RWKV_WIDTH // HEAD_DIM
N_ATT_HEADS = ATT_WIDTH // HEAD_DIM
CONV_K = 31
DECAY_LORA = 64
AAA_LORA = 64
GATE_LORA = 128
CONV_COLS = 2 * CONV_WIDTH
RWKV_COLS = 3 * RWKV_WIDTH + DECAY_LORA + AAA_LORA + GATE_LORA
ATT_COLS = 3 * ATT_WIDTH
IN_COLS = CONV_COLS + RWKV_COLS + ATT_COLS
DILATIONS = ((128, 1), (512, 4), (2048, 16))
MAX_WINDOW = 2048
BLOCK = 128
N_REL_BUCKETS = 32
REL_EXACT = N_REL_BUCKETS // 2
REL_MAX_DIST = MAX_WINDOW
D_FF = 2816
RMS_EPS = 1e-6
LN_EPS = 1e-5
RWKV_GN_EPS = HEAD_DIM * 1e-5
DECAY_SCALE = math.exp(-0.5)
NEG_INF = -1e30

VMEM_LIMIT_BYTES = 56 * 1024 * 1024
SUBLANES = 8
LANES = 128
CONV_HALO = 32
CONV_SUB = 64
FFN_SPLIT = 2
RWKV_CHUNK = HEAD_DIM
N_PAIRS = ATT_WIDTH // LANES
ATT_RES = max(d for _, d in DILATIONS)
ATT_UNIT = BLOCK * ATT_RES
ATT_GROUP = (8, 8, 8)

NN = (((1,), (0,)), ((), ()))
NT = (((1,), (1,)), ((), ()))


def _cparams(sem):
    return pltpu.CompilerParams(dimension_semantics=sem, vmem_limit_bytes=VMEM_LIMIT_BYTES)


def _split_bf16(x, n):
    if x.dtype == BF16:
        return [x]
    pieces = []
    r = x
    for i in range(n):
        p = r.astype(BF16)
        pieces.append(p)
        if i + 1 < n:
            r = r - p.astype(F32)
    return pieces


def _mmp(ap, bp, dims=NN):
    order = max(len(ap), len(bp))
    out = None
    for i, x in enumerate(ap):
        for j, y in enumerate(bp):
            if i + j < order:
                t = lax.dot_general(x, y, dims, preferred_element_type=F32)
                out = t if out is None else out + t
    return out


def _dot1(a, b, dims=NN):
    return lax.dot_general(a.astype(BF16), b.astype(BF16), dims, preferred_element_type=F32)


def _lockstep(gens):
    results = [None] * len(gens)
    while any(r is None for r in results):
        results = [next(g) if r is None else r for g, r in zip(gens, results)]
    return results


def _sigmoid(x):
    return 1.0 / (1.0 + jnp.exp(-x))


def _silu(x):
    return x * _sigmoid(x)


def _rel_bucket_np(dist):
    d = np.maximum(dist, 1).astype(np.float32)
    large = REL_EXACT + (np.log(d / np.float32(REL_EXACT)) / np.float32(math.log(REL_MAX_DIST / REL_EXACT))
                         * np.float32(N_REL_BUCKETS - REL_EXACT)).astype(np.int32)
    large = np.minimum(large, N_REL_BUCKETS - 1)
    return np.where(dist < REL_EXACT, dist, large)


def _bias_rows(rel_bias, dist):
    idx = _rel_bucket_np(np.asarray(dist, np.int32))
    onehot = jnp.asarray(np.eye(N_REL_BUCKETS, dtype=np.float32)[idx.reshape(-1)])
    rows = jnp.dot(onehot, rel_bias, precision=lax.Precision.HIGHEST)
    return rows.reshape(idx.shape + (rel_bias.shape[1],))


def _ada_kernel(c_ref, w_ref, b_ref, o_ref):
    a = _silu(c_ref[...])
    o_ref[...] = _dot1(a, w_ref[...]) + b_ref[...]


def _ada_modulation(c_all, w_ada, b_ada):
    depth = w_ada.shape[0]
    rows = c_all.shape[0]
    tn = D_MODEL
    return pl.pallas_call(
        _ada_kernel,
        out_shape=jax.ShapeDtypeStruct((depth, rows, 6 * D_MODEL), F32),
        grid=(depth, 6 * D_MODEL // tn),
        in_specs=[
            pl.BlockSpec((rows, D_MODEL), lambda l, j: (0, 0)),
            pl.BlockSpec((None, D_MODEL, tn), lambda l, j: (l, 0, j)),
            pl.BlockSpec((None, 1, tn), lambda l, j: (l, 0, j)),
        ],
        out_specs=pl.BlockSpec((None, rows, tn), lambda l, j: (l, 0, j)),
        compiler_params=_cparams(("arbitrary", "arbitrary")),
        name="ada_modulation",
    )(c_all, w_ada, b_ada.reshape(depth, 1, 6 * D_MODEL))


def _in_proj_kernel(x_ref, sc_ref, sh_ref, g_ref, w_ref, *refs, residue_major, tiles_per_seq=None):
    if residue_major:
        (perm_ref, wkt_ref, wvt_ref, cw_ref, cb_ref, clg_ref, clb_ref,
         ya_ref, zt_ref, ur_ref, q_ref, k_ref, v_ref, kt_ref, vt_ref, zs_ref, zsh_ref) = refs
    else:
        uc_ref, ur_ref, q_ref, k_ref, v_ref = refs
    x = x_ref[...]
    ms = jnp.mean(x * x, axis=-1, keepdims=True)
    h = x * lax.rsqrt(ms + RMS_EPS) * g_ref[...]
    h = h * (1.0 + sc_ref[...]) + sh_ref[...]
    hb = h.astype(BF16)
    uc = lax.dot_general(hb, w_ref[:, 0:CONV_COLS], NN, preferred_element_type=F32)
    if residue_major:
        tt = uc.shape[0]

        @pl.when(pl.program_id(0) % tiles_per_seq == 0)
        def _():
            zs_ref[0:CONV_HALO, :] = jnp.zeros((CONV_HALO, CONV_WIDTH), F32)

        zs_ref[CONV_HALO:CONV_HALO + tt, :] = _glu(uc)
    else:
        uc_ref[...] = uc

    def projections():
        o = CONV_COLS
        half = RWKV_COLS // 2
        for c in range(2):
            ur_ref[:, c * half:(c + 1) * half] = lax.dot_general(hb, w_ref[:, o:o + half], NN,
                                                                 preferred_element_type=F32)
            o += half
            yield None
        hq = hb
        if residue_major:
            hq = lax.dot_general(perm_ref[...], hb, NN, preferred_element_type=F32).astype(BF16)
            run = hb.shape[0] // ATT_RES
            yield None
        for ref in (q_ref, k_ref, v_ref):
            res = lax.dot_general(hq, w_ref[:, o:o + ATT_WIDTH], NN, preferred_element_type=F32)
            for p in range(N_PAIRS):
                cols = res[:, p * LANES:(p + 1) * LANES]
                if residue_major:
                    for r in range(ATT_RES):
                        ref[p, r] = cols[r * run:(r + 1) * run]
                else:
                    ref[p] = cols
            o += ATT_WIDTH
            yield None
        yield True

    if residue_major:
        _lockstep([projections(), _conv_stages(zs_ref, zsh_ref, cw_ref, cb_ref, clg_ref, clb_ref, ya_ref, tt)])
        tail = zs_ref[tt:tt + CONV_HALO, :]
        zt_ref[...] = tail
        zs_ref[0:CONV_HALO, :] = tail

        @pl.when(pl.program_id(0) % tiles_per_seq >= tiles_per_seq - ATT_UNIT // hb.shape[0])
        def _():
            kt_ref[...] = lax.dot_general(wkt_ref[...], hb, NT, preferred_element_type=F32)
            vt_ref[...] = lax.dot_general(wvt_ref[...], hb, NT, preferred_element_type=F32)
    else:
        _lockstep([projections()])


def _tile_perm(tm):
    run = tm // ATT_RES
    t = np.arange(tm)
    perm = np.zeros((tm, tm), np.float32)
    perm[(t % ATT_RES) * run + t // ATT_RES, t] = 1.0
    return jnp.asarray(perm, BF16), jnp.asarray(perm.T, BF16)


def _mod_spec(mod, tiles_per_group):
    _, r, d = mod.shape
    return pl.BlockSpec((None, r, d), lambda i: (i // tiles_per_group, 0, 0))


def _in_proj(x2, scale, shift, g, w_bf, tm, tiles_per_group, seq_shape=None, conv_params=None):
    m = x2.shape[0]
    flat = lambda w: jax.ShapeDtypeStruct((m, w), F32)
    row_spec = lambda w: pl.BlockSpec((tm, w), lambda i: (i, 0))
    residue_major = seq_shape is not None
    ins = [x2, scale, shift, g.reshape(1, D_MODEL), w_bf]
    in_specs = [
        pl.BlockSpec((tm, D_MODEL), lambda i: (i, 0)),
        _mod_spec(scale, tiles_per_group),
        _mod_spec(shift, tiles_per_group),
        pl.BlockSpec((1, D_MODEL), lambda i: (0, 0)),
        pl.BlockSpec((D_MODEL, IN_COLS), lambda i: (0, 0)),
    ]
    if residue_major:
        n, t = seq_shape
        per_unit = ATT_UNIT // tm
        per_seq = t // tm
        rows_res = ATT_UNIT // ATT_RES
        pairs = jax.ShapeDtypeStruct((N_PAIRS, n, t // ATT_UNIT, ATT_RES, rows_res, LANES), F32)
        pair_spec = pl.BlockSpec((N_PAIRS, None, None, ATT_RES, tm // ATT_RES, LANES),
                                 lambda i: (0, i // per_seq, (i % per_seq) // per_unit, 0, i % per_unit, 0))
        perm, _ = _tile_perm(tm)
        k0 = CONV_COLS + RWKV_COLS + ATT_WIDTH
        wkt = w_bf[:, k0:k0 + ATT_WIDTH].T
        wvt = w_bf[:, k0 + ATT_WIDTH:k0 + 2 * ATT_WIDTH].T
        ins += [perm, wkt, wvt, *conv_params]
        in_specs += [pl.BlockSpec((tm, tm), lambda i: (0, 0)),
                     pl.BlockSpec((ATT_WIDTH, D_MODEL), lambda i: (0, 0)),
                     pl.BlockSpec((ATT_WIDTH, D_MODEL), lambda i: (0, 0))] + _conv_param_specs(1)
        win_t = jax.ShapeDtypeStruct((n, ATT_WIDTH, ATT_UNIT), F32)
        win_spec = pl.BlockSpec((None, ATT_WIDTH, tm),
                                lambda i: (i // per_seq, 0, jnp.maximum(i % per_seq - (per_seq - per_unit), 0)))
        extra_out, extra_specs = (win_t, win_t), (win_spec, win_spec)
        conv_out = (flat(CONV_WIDTH), jax.ShapeDtypeStruct((n, CONV_HALO, CONV_WIDTH), F32))
        conv_specs = (row_spec(CONV_WIDTH), pl.BlockSpec((None, CONV_HALO, CONV_WIDTH), lambda i: (i // per_seq, 0, 0)))
        scratch = [pltpu.VMEM((CONV_HALO + tm + SUBLANES, CONV_WIDTH), F32),
                   pltpu.VMEM((SUBLANES, CONV_HALO + tm, CONV_WIDTH), F32)]
        kernel_fn = functools.partial(_in_proj_kernel, residue_major=True, tiles_per_seq=per_seq)
    else:
        pairs = jax.ShapeDtypeStruct((N_PAIRS, m, LANES), F32)
        pair_spec = pl.BlockSpec((N_PAIRS, tm, LANES), lambda i: (0, i, 0))
        extra_out, extra_specs = (), ()
        conv_out, conv_specs, scratch = (flat(CONV_COLS),), (row_spec(CONV_COLS),), []
        kernel_fn = functools.partial(_in_proj_kernel, residue_major=False)
    return pl.pallas_call(
        kernel_fn,
        out_shape=conv_out + (flat(RWKV_COLS), pairs, pairs, pairs) + extra_out,
        grid=(m // tm,),
        in_specs=in_specs,
        out_specs=conv_specs + (row_spec(RWKV_COLS), pair_spec, pair_spec, pair_spec) + extra_specs,
        scratch_shapes=scratch,
        compiler_params=_cparams(("arbitrary",)),
        name="in_proj",
    )(*ins)


def _pairs_to_rows(x):
    return jnp.transpose(x, (1, 0, 2)).reshape(x.shape[1], ATT_WIDTH)


def _rows_to_pairs(x):
    return jnp.transpose(x.reshape(x.shape[0], N_PAIRS, LANES), (1, 0, 2))


def _conv_stages(zs_ref, zsh_ref, w_ref, b_ref, lg_ref, lb_ref, y_ref, tt):
    n_rows = CONV_HALO + tt
    zs_ref[n_rows:n_rows + SUBLANES, :] = jnp.zeros((SUBLANES, CONV_WIDTH), F32)
    for s in range(SUBLANES):
        zsh_ref[s] = zs_ref[pl.ds(s, n_rows), :]
    off = CONV_HALO - (CONV_K - 1)
    sub = min(tt, CONV_SUB)
    for r0 in range(0, tt, sub):
        acc = jnp.zeros((sub, CONV_WIDTH), F32)
        for j in range(CONV_K):
            a, b = divmod(off + j, SUBLANES)
            acc = acc + w_ref[j:j + 1, :] * zsh_ref[b, pl.ds(r0 + SUBLANES * a, sub), :]
        y = acc + b_ref[...]
        mu = jnp.mean(y, axis=-1, keepdims=True)
        yc = y - mu
        var = jnp.mean(yc * yc, axis=-1, keepdims=True)
        yn = yc * lax.rsqrt(var + LN_EPS) * lg_ref[...] + lb_ref[...]
        y_ref[r0:r0 + sub, :] = _silu(yn)
        yield None
    yield True


def _conv_core(*args):
    _lockstep([_conv_stages(*args)])


def _glu(u):
    return u[:, :CONV_WIDTH] * _sigmoid(u[:, CONV_WIDTH:])


def _conv_sample_kernel(u_ref, zh_ref, w_ref, b_ref, lg_ref, lb_ref, y_ref, z_ref, zs_ref, zsh_ref, *, tt):
    z = _glu(u_ref[...])
    zs_ref[0:CONV_HALO, :] = zh_ref[...]
    zs_ref[CONV_HALO:CONV_HALO + tt, :] = z
    _conv_core(zs_ref, zsh_ref, w_ref, b_ref, lg_ref, lb_ref, y_ref, tt)
    z_ref[...] = z


def _conv_param_specs(nargs_grid):
    cmap = (lambda n, j: (0, 0)) if nargs_grid == 2 else (lambda n: (0, 0))
    return [
        pl.BlockSpec((CONV_HALO, CONV_WIDTH), cmap),
        pl.BlockSpec((1, CONV_WIDTH), cmap),
        pl.BlockSpec((1, CONV_WIDTH), cmap),
        pl.BlockSpec((1, CONV_WIDTH), cmap),
    ]


def _conv_params(conv_w, conv_b, ln_g, ln_b):
    w = jnp.pad(conv_w, ((0, CONV_HALO - CONV_K), (0, 0)))
    return w, conv_b.reshape(1, -1), ln_g.reshape(1, -1), ln_b.reshape(1, -1)


def _conv_sample(uc, cache, n, t, params):
    tt = SUBLANES
    u3 = jnp.pad(uc.reshape(n, t, CONV_COLS), ((0, 0), (0, tt - t), (0, 0)))
    zh = jnp.pad(cache, ((0, 0), (CONV_HALO - (CONV_K - 1), 0), (0, 0)))
    y, z = pl.pallas_call(
        functools.partial(_conv_sample_kernel, tt=tt),
        out_shape=(jax.ShapeDtypeStruct((n, tt, CONV_WIDTH), F32),
                   jax.ShapeDtypeStruct((n, tt, CONV_WIDTH), F32)),
        grid=(n,),
        in_specs=[
            pl.BlockSpec((None, tt, CONV_COLS), lambda b: (b, 0, 0)),
            pl.BlockSpec((None, CONV_HALO, CONV_WIDTH), lambda b: (b, 0, 0)),
        ] + _conv_param_specs(1),
        out_specs=(pl.BlockSpec((None, tt, CONV_WIDTH), lambda b: (b, 0, 0)),
                   pl.BlockSpec((None, tt, CONV_WIDTH), lambda b: (b, 0, 0))),
        scratch_shapes=[pltpu.VMEM((CONV_HALO + tt + SUBLANES, CONV_WIDTH), F32),
                        pltpu.VMEM((SUBLANES, CONV_HALO + tt, CONV_WIDTH), F32)],
        compiler_params=_cparams(("arbitrary",)),
        name="conv_sample",
    )(u3, zh, *params)
    new_cache = jnp.concatenate([cache[:, t:], z[:, :t]], axis=1)
    return y[:, :t].reshape(n * t, CONV_WIDTH), new_cache


def _rwkv_consts():
    c = RWKV_CHUNK
    rows = np.arange(c)
    tri = (rows[None, :] <= rows[:, None]).astype(np.float32)
    lanes = np.arange(RWKV_WIDTH)
    bd = (lanes[:, None] // HEAD_DIM == lanes[None, :] // HEAD_DIM).astype(np.float32)
    return jnp.asarray(tri, BF16), jnp.asarray(bd, BF16)


def _rwkv_kernel(u_ref, up_ref, sh0_ref, s0_ref, mu_ref, w0_ref, wwa_ref, a0_ref, g2_ref, kk_ref, ka_ref, rk_ref,
                 lg_ref, lb_ref, tri_ref, bd_ref,
                 y_ref, so_ref,
                 us_ref, lw_ref, kn_ref, kb_ref, k2_ref, rr_ref, vv_ref, bon_ref, gate_ref, yy_ref, s_ref,
                 *, tr, t_valid, nseq):
    j = pl.program_id(1)
    c = RWKV_CHUNK
    bd = bd_ref[...]
    o = 3 * RWKV_WIDTH

    @pl.when(j == 0)
    def _():
        s_ref[...] = s0_ref[...]

    for sq in range(nseq):
        u = u_ref[sq]
        prev = jnp.where(j > 0, up_ref[sq], jnp.broadcast_to(sh0_ref[sq], (SUBLANES, RWKV_COLS)))
        us_ref[sq, 0:SUBLANES, :] = prev
        us_ref[sq, SUBLANES:SUBLANES + tr, :] = u
        u_prev = us_ref[sq, pl.ds(SUBLANES - 1, tr), :]
        xs = u + (u_prev - u) * mu_ref[...]
        r = xs[:, :RWKV_WIDTH]
        k = xs[:, RWKV_WIDTH:2 * RWKV_WIDTH]
        v = xs[:, 2 * RWKV_WIDTH:o]
        lo = xs[:, o:o + DECAY_LORA + AAA_LORA]
        lane = lax.broadcasted_iota(jnp.int32, lo.shape, 1)
        lo = jnp.where(lane < DECAY_LORA, jnp.tanh(lo), lo)
        wa = _dot1(lo, wwa_ref[...])
        g = _dot1(_sigmoid(xs[:, o + DECAY_LORA + AAA_LORA:]), g2_ref[...])
        lw = -DECAY_SCALE * _sigmoid(w0_ref[...] + wa[:, :RWKV_WIDTH])
        a = _sigmoid(a0_ref[...] + wa[:, RWKV_WIDTH:])
        kkr = k * kk_ref[...]
        ss = _dot1(kkr * kkr, bd)
        kk = kkr * lax.rsqrt(jnp.maximum(ss, 1e-24))
        k2 = k * (1.0 + (a - 1.0) * ka_ref[...])
        if t_valid < tr:
            valid = lax.broadcasted_iota(jnp.int32, (tr, RWKV_WIDTH), 0) < t_valid
            zero = jnp.zeros_like(r)
            r, k2, v, kk, lw = (jnp.where(valid, z, zero) for z in (r, k2, v, kk, lw))
        lw_ref[sq] = lw
        kn_ref[sq] = -kk
        kb_ref[sq] = kk * a
        k2_ref[sq] = k2
        rr_ref[sq] = r
        vv_ref[sq] = v
        bon_ref[sq] = _dot1(r * k2 * rk_ref[...], bd) * v
        gate_ref[sq] = g

    row = lax.broadcasted_iota(jnp.int32, (c, LANES), 0)
    col = lax.broadcasted_iota(jnp.int32, (c, LANES), 1) & (HEAD_DIM - 1)
    strict = col < row
    incl = col <= row
    ident = jnp.where(col == row, 1.0, 0.0)
    bd2 = bd[0:LANES, 0:LANES]
    bd2_mask = bd2 > 0.5
    n_steps = int(round(math.log2(c)))

    def dot(a, b, dims=NN):
        return lax.dot_general(a.astype(BF16), b, dims, preferred_element_type=F32)

    def stack(x):
        xb = x.astype(BF16)
        return jnp.concatenate([xb, xb], axis=0) * bd2

    def rows2(a, b):
        return jnp.concatenate([a, b], axis=0)

    def chunk_pair(pt, bt, kt, rt, vc, gc, s0):
        zero = jnp.zeros((c, LANES), F32)
        pr = rows2(pt, rt)
        xb = dot(pr, stack(bt), NT)
        xk = dot(pr, stack(kt), NT)
        yield None
        lm = jnp.where(strict, xb[:c], zero)
        mm_ = jnp.where(strict, xk[:c], zero)
        qb = jnp.where(incl, xb[c:], zero)
        qk = jnp.where(incl, xk[c:], zero)
        tinv = ident + lm
        lk = dot(lm, stack(lm))
        mq = dot(rows2(mm_, qk), stack(vc))
        yield None
        for step in range(1, n_steps):
            if step + 1 < n_steps:
                both = dot(rows2(lk, tinv), stack(lk))
                lk = both[:c]
                tinv = tinv + both[c:]
            else:
                tinv = tinv + dot(tinv, stack(lk))
            yield None
        p2 = dot(tinv, stack(pt))
        u0 = dot(tinv, stack(mq[:c]))
        yield None
        xs_ = dot(rows2(p2, rt), stack(s0), NT)
        yield None
        uu = xs_[:c] + u0
        y = xs_[c:] + dot(qb, stack(uu)) + mq[c:]
        uvt = rows2(uu, vc).T
        bk = rows2(bt * gc, kt * gc).astype(BF16)
        z = jnp.where(bd2_mask, dot(uvt, bk), 0.0)
        yield y, s0 * gc + z[0:c] + z[c:2 * c]

    def chunk(ci, carry):
        sl = pl.ds(pl.multiple_of(ci * c, c), c)
        work = []
        for sq in range(nseq):
            lw_c = lw_ref[sq, sl, :]
            gi = _mmp([tri_ref[...]], _split_bf16(lw_c, 3))
            e_neg = jnp.exp(-gi)
            gc = jnp.exp(gi[c - 1:c, :])
            pt = kn_ref[sq, sl, :] * jnp.exp(gi - lw_c)
            bt = kb_ref[sq, sl, :] * e_neg
            kt = k2_ref[sq, sl, :] * e_neg
            rt = rr_ref[sq, sl, :] * jnp.exp(gi)
            vc = vv_ref[sq, sl, :]
            s0 = s_ref[sq]
            for p in range(RWKV_WIDTH // LANES):
                ln = slice(p * LANES, (p + 1) * LANES)
                work.append((sq, ln, tuple(x[:, ln] for x in (pt, bt, kt, rt, vc, gc, s0))))
        results = _lockstep([chunk_pair(*args) for _, _, args in work])
        done = [(sq, ln, res) for (sq, ln, _), res in zip(work, results)]
        for sq, ln, (y, s_new) in done:
            yy_ref[sq, sl, ln] = y
            s_ref[sq, :, ln] = s_new
        return carry

    lax.fori_loop(0, tr // c, chunk, 0)

    inv = 1.0 / HEAD_DIM
    for sq in range(nseq):
        y = yy_ref[sq]
        mu = _dot1(y, bd) * inv
        yc = y - mu
        var = _dot1(yc * yc, bd) * inv
        yn = yc * lax.rsqrt(var + RWKV_GN_EPS) * lg_ref[...] + lb_ref[...]
        y_ref[sq] = (yn + bon_ref[sq]) * gate_ref[sq]
    so_ref[...] = s_ref[...]


def _rwkv_params(p):
    z = jnp.zeros((DECAY_LORA, RWKV_WIDTH), F32)
    wwa = jnp.concatenate([jnp.concatenate([p['rwkv_w2'], z], axis=1),
                           jnp.concatenate([z, p['rwkv_a2']], axis=1)], axis=0)
    row = lambda x: x.reshape(1, -1)
    return (row(p['rwkv_mu']), row(p['rwkv_w0']), wwa, row(p['rwkv_a0']), p['rwkv_g2'], row(p['rwkv_k_k']),
            row(p['rwkv_k_a']), row(p['rwkv_r_k']), row(p['rwkv_ln_g']), row(p['rwkv_ln_b']))


def _rwkv(u3, shift0, s0, params, tr, t_valid, nseq):
    n, t, _ = u3.shape
    assert n % nseq == 0
    consts = _rwkv_consts()
    hb = tr // SUBLANES
    const2 = lambda b, j: (0, 0)
    full = lambda arr: pl.BlockSpec(arr.shape, const2)
    tile = lambda: pltpu.VMEM((nseq, tr, RWKV_WIDTH), F32)
    y, s_out = pl.pallas_call(
        functools.partial(_rwkv_kernel, tr=tr, t_valid=t_valid, nseq=nseq),
        out_shape=(jax.ShapeDtypeStruct((n, t, RWKV_WIDTH), F32),
                   jax.ShapeDtypeStruct((n, HEAD_DIM, RWKV_WIDTH), F32)),
        grid=(n // nseq, t // tr),
        in_specs=[
            pl.BlockSpec((nseq, tr, RWKV_COLS), lambda b, j: (b, j, 0)),
            pl.BlockSpec((nseq, SUBLANES, RWKV_COLS), lambda b, j: (b, jnp.maximum(j * hb - 1, 0), 0)),
            pl.BlockSpec((nseq, 1, RWKV_COLS), lambda b, j: (b, 0, 0)),
            pl.BlockSpec((nseq, HEAD_DIM, RWKV_WIDTH), lambda b, j: (b, 0, 0)),
        ] + [full(x) for x in params] + [full(x) for x in consts],
        out_specs=(pl.BlockSpec((nseq, tr, RWKV_WIDTH), lambda b, j: (b, j, 0)),
                   pl.BlockSpec((nseq, HEAD_DIM, RWKV_WIDTH), lambda b, j: (b, 0, 0))),
        scratch_shapes=[pltpu.VMEM((nseq, SUBLANES + tr, RWKV_COLS), F32)] + [tile() for _ in range(9)]
                       + [pltpu.VMEM((nseq, HEAD_DIM, RWKV_WIDTH), F32)],
        compiler_params=_cparams(("arbitrary", "arbitrary")),
        name="rwkv7",
    )(u3, u3, shift0, s0, *params, *consts)
    return y, s_out


def _state_to_lanes(s):
    n = s.shape[0]
    return jnp.transpose(s, (0, 2, 1, 3)).reshape(n, HEAD_DIM, RWKV_WIDTH)


def _lanes_to_state(s):
    n = s.shape[0]
    return jnp.transpose(s.reshape(n, HEAD_DIM, N_RWKV_HEADS, HEAD_DIM), (0, 2, 1, 3))


def _att_natural_index(dil):
    runs = ATT_RES // dil
    p = np.arange(BLOCK)
    return runs * (p % (BLOCK // runs)) + p // (BLOCK // runs)


def _attn_prompt_kernel(q_ref, kc_ref, kp_ref, vc_ref, vp_ref, bias_ref, o_ref, kcat, vcat, acc, m0, m1, l0, l1):
    u = pl.program_id(2)
    unit = ATT_UNIT
    scale = HEAD_DIM ** -0.5
    kcat[0:unit, :] = kp_ref[...]
    kcat[unit:2 * unit, :] = kc_ref[...]
    vcat[0:unit, :] = vp_ref[...]
    vcat[unit:2 * unit, :] = vc_ref[...]
    lo = lax.broadcasted_iota(jnp.int32, (BLOCK, LANES), 1) < HEAD_DIM
    n_br = len(DILATIONS)
    rows2 = lambda a, b: jnp.concatenate([a, b], axis=0)
    for bi, (window, dil) in enumerate(DILATIONS):
        first = bi == 0
        last = bi == n_br - 1
        shift = int(round(math.log2(dil)))
        n_runs = ATT_RES // dil
        run_len = BLOCK // n_runs
        n_blk = unit // (BLOCK * dil)

        def runs_of(r, g):
            base = (g // n_blk) * unit + (g % n_blk) * run_len
            return [base + (dil * m + r) * BLOCK for m in range(n_runs)]

        def gather(ref, starts):
            return jnp.concatenate([ref[pl.ds(pl.multiple_of(s0, SUBLANES), run_len), :] for s0 in starts], axis=0)

        def sub_block(j, bi, dil, shift, first, last):
            r = j & (dil - 1)
            blk = j >> shift
            q_rows = runs_of(r, blk)
            k_rows = runs_of(r, n_blk + blk - 1) + runs_of(r, n_blk + blk)
            q = gather(q_ref, q_rows) * scale
            kk = gather(kcat, k_rows).astype(BF16)
            vv = gather(vcat, k_rows).astype(BF16)
            q2 = rows2(jnp.where(lo, q, 0.0), jnp.where(lo, 0.0, q)).astype(BF16)
            s = lax.dot_general(q2, kk, NT, preferred_element_type=F32)
            if not first:
                m_old = rows2(gather(m0, q_rows), gather(m1, q_rows))
                l_old = rows2(gather(l0, q_rows), gather(l1, q_rows))
                acc_old = gather(acc, q_rows)
            yield None
            no_prev = ((u == 0) & (blk == 0)).astype(jnp.int32)
            s = s + bias_ref[bi + n_br * no_prev]
            m_cur = jnp.max(s, axis=-1, keepdims=True)
            if first:
                m_new = jnp.broadcast_to(m_cur, (2 * BLOCK, LANES))
            else:
                m_new = jnp.maximum(m_old, m_cur)
                alpha = jnp.exp(m_old - m_new)
            p = jnp.exp(s - jnp.concatenate([m_new, m_new], axis=1))
            l_new = jnp.sum(p, axis=-1, keepdims=True)
            if first:
                l_new = jnp.broadcast_to(l_new, (2 * BLOCK, LANES))
            else:
                l_new = alpha * l_old + l_new
            o2 = lax.dot_general(p.astype(BF16), vv, NN, preferred_element_type=F32)
            yield None
            o_pair = jnp.where(lo, o2[:BLOCK], o2[BLOCK:])
            if not first:
                o_pair = o_pair + jnp.where(lo, alpha[:BLOCK], alpha[BLOCK:]) * acc_old
            if last:
                linv = 1.0 / l_new
                yield [(o_ref, q_rows, o_pair * jnp.where(lo, linv[:BLOCK], linv[BLOCK:]))]
            else:
                yield [(m0, q_rows, m_new[:BLOCK]), (m1, q_rows, m_new[BLOCK:]),
                       (l0, q_rows, l_new[:BLOCK]), (l1, q_rows, l_new[BLOCK:]), (acc, q_rows, o_pair)]

        group = ATT_GROUP[bi]

        def body(jg, carry, bi=bi, dil=dil, shift=shift, first=first, last=last, group=group, run_len=run_len):
            groups = _lockstep([sub_block(jg * group + g, bi, dil, shift, first, last) for g in range(group)])
            for stores in groups:
                for ref, starts, val in stores:
                    for m, s0 in enumerate(starts):
                        ref[pl.ds(pl.multiple_of(s0, SUBLANES), run_len), :] = val[m * run_len:(m + 1) * run_len]
            return carry

        lax.fori_loop(0, unit // BLOCK // group, body, 0)


def _attn_prompt_bias(rel_bias):
    tabs = []
    for _, dil in DILATIONS:
        a = _att_natural_index(dil)
        c = np.concatenate([a, BLOCK + a])
        delta = a[:, None] + BLOCK - c[None, :]
        band = (delta >= 0) & (delta <= BLOCK)
        b = _bias_rows(rel_bias, np.clip(delta, 0, BLOCK) * dil)
        b = jnp.where(jnp.asarray(band)[:, :, None], b, NEG_INF)
        tabs.append(jnp.transpose(b, (2, 0, 1)))
    w = jnp.stack(tabs)
    first = jnp.where(jnp.asarray(np.arange(2 * BLOCK) < BLOCK), NEG_INF, w)
    w = jnp.concatenate([w, first], axis=0)
    w = w.reshape(2 * len(DILATIONS), N_PAIRS, 2 * BLOCK, 2 * BLOCK)
    return jnp.transpose(w, (1, 0, 2, 3))


def _attn_prompt(q, k, v, btab):
    unit = ATT_UNIT
    n_pairs, n, units = q.shape[:3]
    r5 = lambda x: x.reshape(n_pairs, n, units, unit, LANES)
    cur = pl.BlockSpec((None, None, None, unit, LANES), lambda a, p, u: (p, a, u, 0, 0))
    prv = pl.BlockSpec((None, None, None, unit, LANES), lambda a, p, u: (p, a, jnp.maximum(u - 1, 0), 0, 0))
    stat = lambda: pltpu.VMEM((unit, LANES), F32)
    out = pl.pallas_call(
        _attn_prompt_kernel,
        out_shape=jax.ShapeDtypeStruct((n_pairs, n, units, unit, LANES), F32),
        grid=(n, n_pairs, units),
        in_specs=[cur, cur, prv, cur, prv,
                  pl.BlockSpec((None, 2 * len(DILATIONS), 2 * BLOCK, 2 * BLOCK), lambda a, p, u: (p, 0, 0, 0))],
        out_specs=cur,
        scratch_shapes=[pltpu.VMEM((2 * unit, LANES), F32), pltpu.VMEM((2 * unit, LANES), F32),
                        stat(), stat(), stat(), stat(), stat()],
        compiler_params=_cparams(("arbitrary", "arbitrary", "arbitrary")),
        name="attn_prompt",
    )(r5(q), r5(k), r5(k), r5(v), r5(v), btab)
    return out.reshape(q.shape)


def _attn_sample_kernel(q_ref, kn_ref, vn_ref, kc_ref, vc_ref, bt_ref, bn_ref, cnt_ref, cn_ref, hm_ref, o_ref,
                        kn_s, vn_s, *, s_new):
    scale = HEAD_DIM ** -0.5
    hm = hm_ref[...]
    qm = jnp.concatenate([(q_ref[s:s + 1, :] * scale) * hm for s in range(s_new)], axis=0).astype(BF16)
    kn_s[...] = jnp.zeros_like(kn_s)
    vn_s[...] = jnp.zeros_like(vn_s)
    kn_s[0:s_new, :] = kn_ref[...]
    vn_s[0:s_new, :] = vn_ref[...]
    sc = lax.dot_general(qm, kc_ref[...].astype(BF16), NN, preferred_element_type=F32) + bt_ref[...]
    sn = lax.dot_general(qm, kn_s[...].astype(BF16), NT, preferred_element_type=F32) + bn_ref[...]
    cnt = cnt_ref[...]
    cn = cn_ref[...]
    sc = jnp.where(cnt > 0.0, sc, NEG_INF)
    sn = jnp.where(cn > 0.0, sn, NEG_INF)
    m = jnp.maximum(jnp.max(sc, axis=-1, keepdims=True), jnp.max(sn, axis=-1, keepdims=True))
    p = cnt * jnp.exp(sc - m)
    pn = cn * jnp.exp(sn - m)
    l = jnp.sum(p, axis=-1, keepdims=True) + jnp.sum(pn, axis=-1, keepdims=True)
    o = (lax.dot_general(p.astype(BF16), vc_ref[...].astype(BF16), NT, preferred_element_type=F32)
         + lax.dot_general(pn.astype(BF16), vn_s[...].astype(BF16), NN, preferred_element_type=F32))
    o = o / l
    for s in range(s_new):
        o_ref[s:s + 1, :] = jnp.sum(o[s * N_ATT_HEADS:(s + 1) * N_ATT_HEADS] * hm, axis=0, keepdims=True)


def _attn_sample_tables(rel_bias, s_new, win):
    s = np.arange(s_new)[:, None]
    dist_c = win + s - np.arange(win)[None, :]
    dist_n = s - np.arange(LANES)[None, :]
    cnt_c = np.zeros(dist_c.shape, np.float32)
    cnt_n = np.zeros(dist_n.shape, np.float32)
    for window, dil in DILATIONS:
        cnt_c += ((dist_c % dil == 0) & (dist_c <= window)).astype(np.float32)
        cnt_n += ((dist_n >= 0) & (dist_n % dil == 0) & (dist_n <= window)).astype(np.float32)
    rep = lambda x: jnp.asarray(np.repeat(x, N_ATT_HEADS, axis=0))
    rows = lambda b: jnp.transpose(b, (0, 2, 1)).reshape(s_new * N_ATT_HEADS, -1)
    return (rows(_bias_rows(rel_bias, dist_c)), rows(_bias_rows(rel_bias, np.maximum(dist_n, 0))),
            rep(cnt_c), rep(cnt_n))


def _attn_sample(q, k, v, k_cache, v_cache, li, tables, n, s_new):
    win = k_cache.shape[-1]
    assert win == MAX_WINDOW
    lanes = np.arange(ATT_WIDTH)
    hm = jnp.asarray((np.arange(N_ATT_HEADS)[:, None] == lanes[None, :] // HEAD_DIM).astype(np.float32))
    new = pl.BlockSpec((None, s_new, ATT_WIDTH), lambda b: (b, 0, 0))
    buf = pl.BlockSpec((None, None, ATT_WIDTH, win), lambda b: (li, b, 0, 0))
    full = lambda arr: pl.BlockSpec(arr.shape, lambda b: (0, 0))
    r3 = lambda x: x.reshape(n, s_new, ATT_WIDTH)
    out = pl.pallas_call(
        functools.partial(_attn_sample_kernel, s_new=s_new),
        out_shape=jax.ShapeDtypeStruct((n, s_new, ATT_WIDTH), F32),
        grid=(n,),
        in_specs=[new, new, new, buf, buf] + [full(x) for x in tables] + [full(hm)],
        out_specs=new,
        scratch_shapes=[pltpu.VMEM((LANES, ATT_WIDTH), F32), pltpu.VMEM((LANES, ATT_WIDTH), F32)],
        compiler_params=_cparams(("arbitrary",)),
        name="attn_sample",
    )(r3(q), r3(k), r3(v), k_cache, v_cache, *tables, hm)
    return out.reshape(n * s_new, ATT_WIDTH)


def _mix_ffn_kernel(ya_ref, yb_ref, yc_ref, x_ref, gate1_ref, gmix_ref, wout_ref,
                    sc_ref, sh_ref, gate2_ref, gpre_ref, gpost_ref, wg_ref, wu_ref, wo_ref, *refs, residue_major):
    a = CONV_WIDTH
    b = CONV_WIDTH + RWKV_WIDTH
    if residue_major:
        permt_ref, o_ref = refs
        yc = jnp.concatenate([jnp.concatenate([yc_ref[p, r] for r in range(ATT_RES)], axis=0)
                              for p in range(N_PAIRS)], axis=1).astype(BF16)
        yc = lax.dot_general(permt_ref[...], yc, NN, preferred_element_type=F32)
    else:
        o_ref, = refs
        yc = jnp.concatenate([yc_ref[p] for p in range(N_PAIRS)], axis=1)
    mix = (_dot1(ya_ref[...], wout_ref[0:a, :]) + _dot1(yb_ref[...], wout_ref[a:b, :])
           + _dot1(yc, wout_ref[b:, :]))
    ms = jnp.mean(mix * mix, axis=-1, keepdims=True)
    x = x_ref[...] + gate1_ref[...] * (mix * lax.rsqrt(ms + RMS_EPS) * gmix_ref[...])

    ms = jnp.mean(x * x, axis=-1, keepdims=True)
    h = x * lax.rsqrt(ms + RMS_EPS) * gpre_ref[...]
    hb = (h * (1.0 + sc_ref[...]) + sh_ref[...]).astype(BF16)
    y = None
    for c in range(FFN_SPLIT):
        cols = slice(c * (D_FF // FFN_SPLIT), (c + 1) * (D_FF // FFN_SPLIT))
        gg = lax.dot_general(hb, wg_ref[:, cols], NN, preferred_element_type=F32)
        uu = lax.dot_general(hb, wu_ref[:, cols], NN, preferred_element_type=F32)
        act = (_silu(gg) * uu).astype(BF16)
        part = lax.dot_general(act, wo_ref[cols, :], NN, preferred_element_type=F32)
        y = part if y is None else y + part
    ms = jnp.mean(y * y, axis=-1, keepdims=True)
    o_ref[...] = x + gate2_ref[...] * (y * lax.rsqrt(ms + RMS_EPS) * gpost_ref[...])


def _mix_ffn(ya, yb, yc, x2, gate1, g_mix, w_out, scale2, shift2, gate2, g_pre, g_post, wg, wu, wo, tm,
             tiles_per_group):
    m = x2.shape[0]
    rows = lambda w: pl.BlockSpec((tm, w), lambda i: (i, 0))
    vec = pl.BlockSpec((1, D_MODEL), lambda i: (0, 0))
    resident = lambda arr: pl.BlockSpec(arr.shape, lambda i: (0, 0), pipeline_mode=pl.Buffered(1))
    residue_major = yc.ndim == 6
    extra, extra_specs = [], []
    if residue_major:
        t = yc.shape[2] * ATT_UNIT
        per_unit = ATT_UNIT // tm
        per_seq = t // tm
        yc_spec = pl.BlockSpec((N_PAIRS, None, None, ATT_RES, tm // ATT_RES, LANES),
                               lambda i: (0, i // per_seq, (i % per_seq) // per_unit, 0, i % per_unit, 0))
        _, permt = _tile_perm(tm)
        extra, extra_specs = [permt], [resident(permt)]
    else:
        yc_spec = pl.BlockSpec((N_PAIRS, tm, LANES), lambda i: (0, i, 0))
    mod = lambda arr: _mod_spec(arr, tiles_per_group)
    return pl.pallas_call(
        functools.partial(_mix_ffn_kernel, residue_major=residue_major),
        out_shape=jax.ShapeDtypeStruct((m, D_MODEL), F32),
        grid=(m // tm,),
        in_specs=[rows(CONV_WIDTH), rows(RWKV_WIDTH), yc_spec, rows(D_MODEL), mod(gate1), vec, resident(w_out),
                  mod(scale2), mod(shift2), mod(gate2), vec, vec, resident(wg), resident(wu), resident(wo)]
                 + extra_specs,
        out_specs=rows(D_MODEL),
        compiler_params=_cparams(("arbitrary",)),
        name="mix_ffn",
    )(ya, yb, yc, x2, gate1, g_mix.reshape(1, D_MODEL), w_out, scale2, shift2, gate2,
      g_pre.reshape(1, D_MODEL), g_post.reshape(1, D_MODEL), wg, wu, wo, *extra)


def _window_to_heads(xt):
    n, _, win = xt.shape
    return jnp.transpose(xt.reshape(n, N_ATT_HEADS, HEAD_DIM, win), (0, 3, 1, 2))


def _layer(x2, n, t, mods, lp, att, state, li):
    shift1, scale1, gate1, shift2, scale2, gate2 = mods
    prompt = state is None
    tm = 512 if prompt else n * t
    tpg = t // tm if prompt else 1
    conv_params = _conv_params(lp['conv_w'], lp['conv_b'], lp['conv_ln_g'], lp['conv_ln_b'])
    rwkv_params = _rwkv_params(lp)
    if prompt:
        ya, zt, ur, q, k, v, *windows = _in_proj(x2, scale1, shift1, lp['g_pre_mix'], lp['w_in_bf'], tm, tpg,
                                                 seq_shape=(n, t), conv_params=conv_params)
        conv_cache = zt[:, CONV_HALO - (CONV_K - 1):]
    else:
        uc, ur, q, k, v = _in_proj(x2, scale1, shift1, lp['g_pre_mix'], lp['w_in_bf'], tm, tpg)
    ur3 = ur.reshape(n, t, RWKV_COLS)
    new_shift = ur3[:, -1]
    if prompt:
        shift0 = jnp.zeros((n, 1, RWKV_COLS), F32)
        s0 = jnp.zeros((n, HEAD_DIM, RWKV_WIDTH), F32)
        yb, s_out = _rwkv(ur3, shift0, s0, rwkv_params, tr=512, t_valid=512, nseq=4)
        yc = _attn_prompt(q, k, v, att['prompt'])
        assert min(MAX_WINDOW, t) == ATT_UNIT
        new_k, new_v = (_window_to_heads(w) for w in windows)
    else:
        cache_conv, state_shift, state_wkv, k_cache, v_cache = state
        ya, conv_cache = _conv_sample(uc, cache_conv[li], n, t, conv_params)
        tr = RWKV_CHUNK
        ur_pad = jnp.pad(ur3, ((0, 0), (0, tr - t), (0, 0)))
        yb, s_out = _rwkv(ur_pad, state_shift[li][:, None, :], _state_to_lanes(state_wkv[li]), rwkv_params,
                          tr=tr, t_valid=t, nseq=8)
        yb = yb[:, :t]
        q_r, k_r, v_r = _pairs_to_rows(q), _pairs_to_rows(k), _pairs_to_rows(v)
        yc = _rows_to_pairs(_attn_sample(q_r, k_r, v_r, k_cache, v_cache, li, att['sample'], n, t))
        new_k = k_r.reshape(n, t, N_ATT_HEADS, HEAD_DIM)
        new_v = v_r.reshape(n, t, N_ATT_HEADS, HEAD_DIM)
    yb = yb.reshape(n * t, RWKV_WIDTH)
    x2 = _mix_ffn(ya, yb, yc, x2, gate1, lp['g_post_mix'], lp['w_out_bf'], scale2, shift2, gate2,
                  lp['g_pre_ffn'], lp['g_post_ffn'], lp['w_ffn_g_bf'], lp['w_ffn_u_bf'], lp['w_ffn_o_bf'], tm, tpg)
    return x2, (conv_cache, new_shift, _lanes_to_state(s_out), new_k, new_v)


def kernel(x_prompt, x_sample, c_prompt, c_sample, cache_conv, state_shift, state_wkv, cache_k_win, cache_v_win,
           w_ada, b_ada, g_pre_mix, g_post_mix, g_pre_ffn, g_post_ffn, w_in, w_out,
           conv_w, conv_b, conv_ln_g, conv_ln_b,
           rwkv_mu, rwkv_w0, rwkv_w2, rwkv_a0, rwkv_a2, rwkv_g2, rwkv_k_k, rwkv_k_a, rwkv_r_k,
           rwkv_ln_g, rwkv_ln_b, rel_bias, w_ffn_in, w_ffn_out):
    depth = w_in.shape[0]
    nb, tp, _ = x_prompt.shape
    ns, ts, _ = x_sample.shape
    rows = nb + ns
    rows_pad = -(-rows // SUBLANES) * SUBLANES
    c_all = jnp.pad(jnp.concatenate([c_prompt, c_sample], axis=0), ((0, rows_pad - rows), (0, 0)))
    mod = _ada_modulation(c_all, w_ada, b_ada)
    mod = mod.reshape(depth, rows_pad, 6, D_MODEL)

    win = cache_k_win.shape[2]
    att = {'prompt': _attn_prompt_bias(rel_bias), 'sample': _attn_sample_tables(rel_bias, ts, win)}
    to_pos_minor = lambda c: jnp.transpose(c, (0, 1, 3, 4, 2)).reshape(depth, ns, ATT_WIDTH, win)
    state = (cache_conv, state_shift, state_wkv, to_pos_minor(cache_k_win), to_pos_minor(cache_v_win))

    yp = x_prompt.reshape(nb * tp, D_MODEL)
    ys = x_sample.reshape(ns * ts, D_MODEL)
    outs_p, outs_s = [], []
    for li in range(depth):
        lp = dict(g_pre_mix=g_pre_mix[li], g_post_mix=g_post_mix[li], g_pre_ffn=g_pre_ffn[li],
                  g_post_ffn=g_post_ffn[li],
                  w_in_bf=w_in[li].astype(BF16), w_out_bf=w_out[li].astype(BF16),
                  w_ffn_g_bf=w_ffn_in[li, :, :D_FF].astype(BF16), w_ffn_u_bf=w_ffn_in[li, :, D_FF:].astype(BF16),
                  w_ffn_o_bf=w_ffn_out[li].astype(BF16),
                  conv_w=conv_w[li], conv_b=conv_b[li], conv_ln_g=conv_ln_g[li], conv_ln_b=conv_ln_b[li],
                  rwkv_mu=rwkv_mu[li], rwkv_w0=rwkv_w0[li], rwkv_w2=rwkv_w2[li], rwkv_a0=rwkv_a0[li],
                  rwkv_a2=rwkv_a2[li], rwkv_g2=rwkv_g2[li], rwkv_k_k=rwkv_k_k[li], rwkv_k_a=rwkv_k_a[li],
                  rwkv_r_k=rwkv_r_k[li], rwkv_ln_g=rwkv_ln_g[li], rwkv_ln_b=rwkv_ln_b[li])
        mods_p = tuple(mod[li, :nb, j][:, None, :] for j in range(6))
        mods_s = tuple(jnp.repeat(mod[li, nb:rows, j], ts, axis=0)[None] for j in range(6))
        yp, st_p = _layer(yp, nb, tp, mods_p, lp, att, None, li)
        ys, st_s = _layer(ys, ns, ts, mods_s, lp, att, state, li)
        outs_p.append(st_p)
        outs_s.append(st_s)
    stack = lambda outs, i: jnp.stack([o[i] for o in outs])
    return (yp.reshape(nb, tp, D_MODEL), ys.reshape(ns, ts, D_MODEL),
            stack(outs_p, 0), stack(outs_s, 0),
            stack(outs_p, 1), stack(outs_s, 1),
            stack(outs_p, 2), stack(outs_s, 2),
            stack(outs_p, 3), stack(outs_s, 3),
            stack(outs_p, 4), stack(outs_s, 4))
```

```python
import functools
import math

import numpy as np
import jax
import jax.numpy as jnp
from jax import lax
from jax.experimental import pallas as pl
from jax.experimental.pallas import tpu as pltpu

F32 = jnp.float32
BF16 = jnp.bfloat16

D_MODEL = 1024
HEAD_DIM = 64
CONV_WIDTH = 256
RWKV_WIDTH = 256
ATT_WIDTH = 512
N_RWKV_HEADS = RWKV_WIDTH // HEAD_DIM
N_ATT_HEADS = ATT_WIDTH // HEAD_DIM
CONV_K = 31
DECAY_LORA = 64
AAA_LORA = 64
GATE_LORA = 128
CONV_COLS = 2 * CONV_WIDTH
RWKV_COLS = 3 * RWKV_WIDTH + DECAY_LORA + AAA_LORA + GATE_LORA
ATT_COLS = 3 * ATT_WIDTH
IN_COLS = CONV_COLS + RWKV_COLS + ATT_COLS
DILATIONS = ((128, 1), (512, 4), (2048, 16))
MAX_WINDOW = 2048
BLOCK = 128
N_REL_BUCKETS = 32
REL_EXACT = N_REL_BUCKETS // 2
REL_MAX_DIST = MAX_WINDOW
D_FF = 2816
RMS_EPS = 1e-6
LN_EPS = 1e-5
RWKV_GN_EPS = HEAD_DIM * 1e-5
DECAY_SCALE = math.exp(-0.5)
NEG_INF = -1e30

VMEM_LIMIT_BYTES = 56 * 1024 * 1024
SUBLANES = 8
LANES = 128
CONV_HALO = 32
CONV_SUB = 64
FFN_SPLIT = 1
RWKV_CHUNK = HEAD_DIM
N_PAIRS = ATT_WIDTH // LANES
ATT_RES = max(d for _, d in DILATIONS)
ATT_UNIT = BLOCK * ATT_RES
ATT_GROUP = (16, 16, 16)

NN = (((1,), (0,)), ((), ()))
NT = (((1,), (1,)), ((), ()))


def _cparams(sem):
    return pltpu.CompilerParams(dimension_semantics=sem, vmem_limit_bytes=VMEM_LIMIT_BYTES)


def _split_bf16(x, n):
    if x.dtype == BF16:
        return [x]
    pieces = []
    r = x
    for i in range(n):
        p = r.astype(BF16)
        pieces.append(p)
        if i + 1 < n:
            r = r - p.astype(F32)
    return pieces


def _mmp(ap, bp, dims=NN):
    order = max(len(ap), len(bp))
    out = None
    for i, x in enumerate(ap):
        for j, y in enumerate(bp):
            if i + j < order:
                t = lax.dot_general(x, y, dims, preferred_element_type=F32)
                out = t if out is None else out + t
    return out


def _dot1(a, b, dims=NN):
    return lax.dot_general(a.astype(BF16), b.astype(BF16), dims, preferred_element_type=F32)


def _lockstep(gens):
    results = [None] * len(gens)
    while any(r is None for r in results):
        results = [next(g) if r is None else r for g, r in zip(gens, results)]
    return results


def _sigmoid(x):
    return 1.0 / (1.0 + jnp.exp(-x))


def _silu(x):
    return x * _sigmoid(x)


def _rel_bucket_np(dist):
    d = np.maximum(dist, 1).astype(np.float32)
    large = REL_EXACT + (np.log(d / np.float32(REL_EXACT)) / np.float32(math.log(REL_MAX_DIST / REL_EXACT))
                         * np.float32(N_REL_BUCKETS - REL_EXACT)).astype(np.int32)
    large = np.minimum(large, N_REL_BUCKETS - 1)
    return np.where(dist < REL_EXACT, dist, large)


def _bias_rows(rel_bias, dist):
    idx = _rel_bucket_np(np.asarray(dist, np.int32))
    onehot = jnp.asarray(np.eye(N_REL_BUCKETS, dtype=np.float32)[idx.reshape(-1)])
    rows = jnp.dot(onehot, rel_bias, precision=lax.Precision.HIGHEST)
    return rows.reshape(idx.shape + (rel_bias.shape[1],))


def _ada_kernel(c_ref, w_ref, b_ref, o_ref):
    a = _silu(c_ref[...])
    o_ref[...] = _dot1(a, w_ref[...]) + b_ref[...]


def _ada_modulation(c_all, w_ada, b_ada):
    depth = w_ada.shape[0]
    rows = c_all.shape[0]
    tn = D_MODEL
    return pl.pallas_call(
        _ada_kernel,
        out_shape=jax.ShapeDtypeStruct((depth, rows, 6 * D_MODEL), F32),
        grid=(depth, 6 * D_MODEL // tn),
        in_specs=[
            pl.BlockSpec((rows, D_MODEL), lambda l, j: (0, 0)),
            pl.BlockSpec((None, D_MODEL, tn), lambda l, j: (l, 0, j)),
            pl.BlockSpec((None, 1, tn), lambda l, j: (l, 0, j)),
        ],
        out_specs=pl.BlockSpec((None, rows, tn), lambda l, j: (l, 0, j)),
        compiler_params=_cparams(("arbitrary", "arbitrary")),
        name="ada_modulation",
    )(c_all, w_ada, b_ada.reshape(depth, 1, 6 * D_MODEL))


def _in_proj_kernel(x_ref, sc_ref, sh_ref, g_ref, w_ref, *refs, residue_major, tiles_per_seq=None):
    if residue_major:
        (perm_ref, wkt_ref, wvt_ref, cw_ref, cb_ref, clg_ref, clb_ref,
         ya_ref, zt_ref, ur_ref, q_ref, k_ref, v_ref, kt_ref, vt_ref, zs_ref, zsh_ref) = refs
    else:
        uc_ref, ur_ref, q_ref, k_ref, v_ref = refs
    x = x_ref[...]
    ms = jnp.mean(x * x, axis=-1, keepdims=True)
    h = x * lax.rsqrt(ms + RMS_EPS) * g_ref[...]
    h = h * (1.0 + sc_ref[...]) + sh_ref[...]
    hb = h.astype(BF16)
    uc = lax.dot_general(hb, w_ref[:, 0:CONV_COLS], NN, preferred_element_type=F32)
    if residue_major:
        tt = uc.shape[0]

        @pl.when(pl.program_id(0) % tiles_per_seq == 0)
        def _():
            zs_ref[0:CONV_HALO, :] = jnp.zeros((CONV_HALO, CONV_WIDTH), F32)

        zs_ref[CONV_HALO:CONV_HALO + tt, :] = _glu(uc)
    else:
        uc_ref[...] = uc

    def projections():
        o = CONV_COLS
        half = RWKV_COLS // 2
        for c in range(2):
            ur_ref[:, c * half:(c + 1) * half] = lax.dot_general(hb, w_ref[:, o:o + half], NN,
                                                                 preferred_element_type=F32)
            o += half
            yield None
        hq = hb
        if residue_major:
            hq = lax.dot_general(perm_ref[...], hb, NN, preferred_element_type=F32).astype(BF16)
            run = hb.shape[0] // ATT_RES
            yield None
        for ref in (q_ref, k_ref, v_ref):
            res = lax.dot_general(hq, w_ref[:, o:o + ATT_WIDTH], NN, preferred_element_type=F32)
            for p in range(N_PAIRS):
                cols = res[:, p * LANES:(p + 1) * LANES]
                if residue_major:
                    for r in range(ATT_RES):
                        ref[p, r] = cols[r * run:(r + 1) * run]
                else:
                    ref[p] = cols
            o += ATT_WIDTH
            yield None
        yield True

    if residue_major:
        _lockstep([projections(), _conv_stages(zs_ref, zsh_ref, cw_ref, cb_ref, clg_ref, clb_ref, ya_ref, tt)])
        tail = zs_ref[tt:tt + CONV_HALO, :]
        zt_ref[...] = tail
        zs_ref[0:CONV_HALO, :] = tail

        @pl.when(pl.program_id(0) % tiles_per_seq >= tiles_per_seq - ATT_UNIT // hb.shape[0])
        def _():
            kt_ref[...] = lax.dot_general(wkt_ref[...], hb, NT, preferred_element_type=F32)
            vt_ref[...] = lax.dot_general(wvt_ref[...], hb, NT, preferred_element_type=F32)
    else:
        _lockstep([projections()])


def _tile_perm(tm):
    run = tm // ATT_RES
    t = np.arange(tm)
    perm = np.zeros((tm, tm), np.float32)
    perm[(t % ATT_RES) * run + t // ATT_RES, t] = 1.0
    return jnp.asarray(perm, BF16), jnp.asarray(perm.T, BF16)


def _mod_spec(mod, tiles_per_group):
    _, r, d = mod.shape
    return pl.BlockSpec((None, r, d), lambda i: (i // tiles_per_group, 0, 0))


def _in_proj(x2, scale, shift, g, w_bf, tm, tiles_per_group, seq_shape=None, conv_params=None):
    m = x2.shape[0]
    flat = lambda w: jax.ShapeDtypeStruct((m, w), F32)
    row_spec = lambda w: pl.BlockSpec((tm, w), lambda i: (i, 0))
    residue_major = seq_shape is not None
    ins = [x2, scale, shift, g.reshape(1, D_MODEL), w_bf]
    in_specs = [
        pl.BlockSpec((tm, D_MODEL), lambda i: (i, 0)),
        _mod_spec(scale, tiles_per_group),
        _mod_spec(shift, tiles_per_group),
        pl.BlockSpec((1, D_MODEL), lambda i: (0, 0)),
        pl.BlockSpec((D_MODEL, IN_COLS), lambda i: (0, 0)),
    ]
    if residue_major:
        n, t = seq_shape
        per_unit = ATT_UNIT // tm
        per_seq = t // tm
        rows_res = ATT_UNIT // ATT_RES
        pairs = jax.ShapeDtypeStruct((N_PAIRS, n, t // ATT_UNIT, ATT_RES, rows_res, LANES), F32)
        pair_spec = pl.BlockSpec((N_PAIRS, None, None, ATT_RES, tm // ATT_RES, LANES),
                                 lambda i: (0, i // per_seq, (i % per_seq) // per_unit, 0, i % per_unit, 0))
        perm, _ = _tile_perm(tm)
        k0 = CONV_COLS + RWKV_COLS + ATT_WIDTH
        wkt = w_bf[:, k0:k0 + ATT_WIDTH].T
        wvt = w_bf[:, k0 + ATT_WIDTH:k0 + 2 * ATT_WIDTH].T
        ins += [perm, wkt, wvt, *conv_params]
        in_specs += [pl.BlockSpec((tm, tm), lambda i: (0, 0)),
                     pl.BlockSpec((ATT_WIDTH, D_MODEL), lambda i: (0, 0)),
                     pl.BlockSpec((ATT_WIDTH, D_MODEL), lambda i: (0, 0))] + _conv_param_specs(1)
        win_t = jax.ShapeDtypeStruct((n, ATT_WIDTH, ATT_UNIT), F32)
        win_spec = pl.BlockSpec((None, ATT_WIDTH, tm),
                                lambda i: (i // per_seq, 0, jnp.maximum(i % per_seq - (per_seq - per_unit), 0)))
        extra_out, extra_specs = (win_t, win_t), (win_spec, win_spec)
        conv_out = (flat(CONV_WIDTH), jax.ShapeDtypeStruct((n, CONV_HALO, CONV_WIDTH), F32))
        conv_specs = (row_spec(CONV_WIDTH), pl.BlockSpec((None, CONV_HALO, CONV_WIDTH), lambda i: (i // per_seq, 0, 0)))
        scratch = [pltpu.VMEM((CONV_HALO + tm + SUBLANES, CONV_WIDTH), F32),
                   pltpu.VMEM((SUBLANES, CONV_HALO + tm, CONV_WIDTH), F32)]
        kernel_fn = functools.partial(_in_proj_kernel, residue_major=True, tiles_per_seq=per_seq)
    else:
        pairs = jax.ShapeDtypeStruct((N_PAIRS, m, LANES), F32)
        pair_spec = pl.BlockSpec((N_PAIRS, tm, LANES), lambda i: (0, i, 0))
        extra_out, extra_specs = (), ()
        conv_out, conv_specs, scratch = (flat(CONV_COLS),), (row_spec(CONV_COLS),), []
        kernel_fn = functools.partial(_in_proj_kernel, residue_major=False)
    return pl.pallas_call(
        kernel_fn,
        out_shape=conv_out + (flat(RWKV_COLS), pairs, pairs, pairs) + extra_out,
        grid=(m // tm,),
        in_specs=in_specs,
        out_specs=conv_specs + (row_spec(RWKV_COLS), pair_spec, pair_spec, pair_spec) + extra_specs,
        scratch_shapes=scratch,
        compiler_params=_cparams(("arbitrary",)),
        name="in_proj",
    )(*ins)


def _pairs_to_rows(x):
    return jnp.transpose(x, (1, 0, 2)).reshape(x.shape[1], ATT_WIDTH)


def _rows_to_pairs(x):
    return jnp.transpose(x.reshape(x.shape[0], N_PAIRS, LANES), (1, 0, 2))


def _conv_stages(zs_ref, zsh_ref, w_ref, b_ref, lg_ref, lb_ref, y_ref, tt):
    n_rows = CONV_HALO + tt
    zs_ref[n_rows:n_rows + SUBLANES, :] = jnp.zeros((SUBLANES, CONV_WIDTH), F32)
    for s in range(SUBLANES):
        zsh_ref[s] = zs_ref[pl.ds(s, n_rows), :]
    off = CONV_HALO - (CONV_K - 1)
    sub = min(tt, CONV_SUB)
    for r0 in range(0, tt, sub):
        acc = jnp.zeros((sub, CONV_WIDTH), F32)
        for j in range(CONV_K):
            a, b = divmod(off + j, SUBLANES)
            acc = acc + w_ref[j:j + 1, :] * zsh_ref[b, pl.ds(r0 + SUBLANES * a, sub), :]
        y = acc + b_ref[...]
        mu = jnp.mean(y, axis=-1, keepdims=True)
        yc = y - mu
        var = jnp.mean(yc * yc, axis=-1, keepdims=True)
        yn = yc * lax.rsqrt(var + LN_EPS) * lg_ref[...] + lb_ref[...]
        y_ref[r0:r0 + sub, :] = _silu(yn)
        yield None
    yield True


def _conv_core(*args):
    _lockstep([_conv_stages(*args)])


def _glu(u):
    return u[:, :CONV_WIDTH] * _sigmoid(u[:, CONV_WIDTH:])


def _conv_sample_kernel(u_ref, zh_ref, w_ref, b_ref, lg_ref, lb_ref, y_ref, z_ref, zs_ref, zsh_ref, *, tt):
    z = _glu(u_ref[...])
    zs_ref[0:CONV_HALO, :] = zh_ref[...]
    zs_ref[CONV_HALO:CONV_HALO + tt, :] = z
    _conv_core(zs_ref, zsh_ref, w_ref, b_ref, lg_ref, lb_ref, y_ref, tt)
    z_ref[...] = z


def _conv_param_specs(nargs_grid):
    cmap = (lambda n, j: (0, 0)) if nargs_grid == 2 else (lambda n: (0, 0))
    return [
        pl.BlockSpec((CONV_HALO, CONV_WIDTH), cmap),
        pl.BlockSpec((1, CONV_WIDTH), cmap),
        pl.BlockSpec((1, CONV_WIDTH), cmap),
        pl.BlockSpec((1, CONV_WIDTH), cmap),
    ]


def _conv_params(conv_w, conv_b, ln_g, ln_b):
    w = jnp.pad(conv_w, ((0, CONV_HALO - CONV_K), (0, 0)))
    return w, conv_b.reshape(1, -1), ln_g.reshape(1, -1), ln_b.reshape(1, -1)


def _conv_sample(uc, cache, n, t, params):
    tt = SUBLANES
    u3 = jnp.pad(uc.reshape(n, t, CONV_COLS), ((0, 0), (0, tt - t), (0, 0)))
    zh = jnp.pad(cache, ((0, 0), (CONV_HALO - (CONV_K - 1), 0), (0, 0)))
    y, z = pl.pallas_call(
        functools.partial(_conv_sample_kernel, tt=tt),
        out_shape=(jax.ShapeDtypeStruct((n, tt, CONV_WIDTH), F32),
                   jax.ShapeDtypeStruct((n, tt, CONV_WIDTH), F32)),
        grid=(n,),
        in_specs=[
            pl.BlockSpec((None, tt, CONV_COLS), lambda b: (b, 0, 0)),
            pl.BlockSpec((None, CONV_HALO, CONV_WIDTH), lambda b: (b, 0, 0)),
        ] + _conv_param_specs(1),
        out_specs=(pl.BlockSpec((None, tt, CONV_WIDTH), lambda b: (b, 0, 0)),
                   pl.BlockSpec((None, tt, CONV_WIDTH), lambda b: (b, 0, 0))),
        scratch_shapes=[pltpu.VMEM((CONV_HALO + tt + SUBLANES, CONV_WIDTH), F32),
                        pltpu.VMEM((SUBLANES, CONV_HALO + tt, CONV_WIDTH), F32)],
        compiler_params=_cparams(("arbitrary",)),
        name="conv_sample",
    )(u3, zh, *params)
    new_cache = jnp.concatenate([cache[:, t:], z[:, :t]], axis=1)
    return y[:, :t].reshape(n * t, CONV_WIDTH), new_cache


def _rwkv_consts():
    c = RWKV_CHUNK
    rows = np.arange(c)
    tri = (rows[None, :] <= rows[:, None]).astype(np.float32)
    lanes = np.arange(RWKV_WIDTH)
    bd = (lanes[:, None] // HEAD_DIM == lanes[None, :] // HEAD_DIM).astype(np.float32)
    return jnp.asarray(tri, BF16), jnp.asarray(bd, BF16)


def _rwkv_kernel(u_ref, up_ref, sh0_ref, s0_ref, mu_ref, w0_ref, wwa_ref, a0_ref, g2_ref, kk_ref, ka_ref, rk_ref,
                 lg_ref, lb_ref, tri_ref, bd_ref,
                 y_ref, so_ref,
                 us_ref, lw_ref, kn_ref, kb_ref, k2_ref, rr_ref, vv_ref, bon_ref, gate_ref, yy_ref, s_ref,
                 *, tr, t_valid, nseq):
    j = pl.program_id(1)
    c = RWKV_CHUNK
    bd = bd_ref[...]
    o = 3 * RWKV_WIDTH

    @pl.when(j == 0)
    def _():
        s_ref[...] = s0_ref[...]

    for sq in range(nseq):
        u = u_ref[sq]
        prev = jnp.where(j > 0, up_ref[sq], jnp.broadcast_to(sh0_ref[sq], (SUBLANES, RWKV_COLS)))
        us_ref[sq, 0:SUBLANES, :] = prev
        us_ref[sq, SUBLANES:SUBLANES + tr, :] = u
        u_prev = us_ref[sq, pl.ds(SUBLANES - 1, tr), :]
        xs = u + (u_prev - u) * mu_ref[...]
        r = xs[:, :RWKV_WIDTH]
        k = xs[:, RWKV_WIDTH:2 * RWKV_WIDTH]
        v = xs[:, 2 * RWKV_WIDTH:o]
        lo = xs[:, o:o + DECAY_LORA + AAA_LORA]
        lane = lax.broadcasted_iota(jnp.int32, lo.shape, 1)
        lo = jnp.where(lane < DECAY_LORA, jnp.tanh(lo), lo)
        wa = _dot1(lo, wwa_ref[...])
        g = _dot1(_sigmoid(xs[:, o + DECAY_LORA + AAA_LORA:]), g2_ref[...])
        lw = -DECAY_SCALE * _sigmoid(w0_ref[...] + wa[:, :RWKV_WIDTH])
        a = _sigmoid(a0_ref[...] + wa[:, RWKV_WIDTH:])
        kkr = k * kk_ref[...]
        ss = _dot1(kkr * kkr, bd)
        kk = kkr * lax.rsqrt(jnp.maximum(ss, 1e-24))
        k2 = k * (1.0 + (a - 1.0) * ka_ref[...])
        if t_valid < tr:
            valid = lax.broadcasted_iota(jnp.int32, (tr, RWKV_WIDTH), 0) < t_valid
            zero = jnp.zeros_like(r)
            r, k2, v, kk, lw = (jnp.where(valid, z, zero) for z in (r, k2, v, kk, lw))
        lw_ref[sq] = lw
        kn_ref[sq] = -kk
        kb_ref[sq] = kk * a
        k2_ref[sq] = k2
        rr_ref[sq] = r
        vv_ref[sq] = v
        bon_ref[sq] = _dot1(r * k2 * rk_ref[...], bd) * v
        gate_ref[sq] = g

    row = lax.broadcasted_iota(jnp.int32, (c, LANES), 0)
    col = lax.broadcasted_iota(jnp.int32, (c, LANES), 1) & (HEAD_DIM - 1)
    strict = col < row
    incl = col <= row
    ident = jnp.where(col == row, 1.0, 0.0)
    bd2 = bd[0:LANES, 0:LANES]
    bd2_mask = bd2 > 0.5
    n_steps = int(round(math.log2(c)))

    def dot(a, b, dims=NN):
        return lax.dot_general(a.astype(BF16), b, dims, preferred_element_type=F32)

    def stack(x):
        xb = x.astype(BF16)
        return jnp.concatenate([xb, xb], axis=0) * bd2

    def rows2(a, b):
        return jnp.concatenate([a, b], axis=0)

    def chunk_pair(pt, bt, kt, rt, vc, gc, s0):
        zero = jnp.zeros((c, LANES), F32)
        pr = rows2(pt, rt)
        xb = dot(pr, stack(bt), NT)
        xk = dot(pr, stack(kt), NT)
        yield None
        lm = jnp.where(strict, xb[:c], zero)
        mm_ = jnp.where(strict, xk[:c], zero)
        qb = jnp.where(incl, xb[c:], zero)
        qk = jnp.where(incl, xk[c:], zero)
        tinv = ident + lm
        lk = dot(lm, stack(lm))
        mq = dot(rows2(mm_, qk), stack(vc))
        yield None
        for step in range(1, n_steps):
            if step + 1 < n_steps:
                both = dot(rows2(lk, tinv), stack(lk))
                lk = both[:c]
                tinv = tinv + both[c:]
            else:
                tinv = tinv + dot(tinv, stack(lk))
            yield None
        p2 = dot(tinv, stack(pt))
        u0 = dot(tinv, stack(mq[:c]))
        yield None
        xs_ = dot(rows2(p2, rt), stack(s0), NT)
        yield None
        uu = xs_[:c] + u0
        y = xs_[c:] + dot(qb, stack(uu)) + mq[c:]
        uvt = rows2(uu, vc).T
        bk = rows2(bt * gc, kt * gc).astype(BF16)
        z = jnp.where(bd2_mask, dot(uvt, bk), 0.0)
        yield y, s0 * gc + z[0:c] + z[c:2 * c]

    def chunk(ci, carry):
        sl = pl.ds(pl.multiple_of(ci * c, c), c)
        work = []
        for sq in range(nseq):
            lw_c = lw_ref[sq, sl, :]
            gi = _mmp([tri_ref[...]], _split_bf16(lw_c, 3))
            e_neg = jnp.exp(-gi)
            gc = jnp.exp(gi[c - 1:c, :])
            pt = kn_ref[sq, sl, :] * jnp.exp(gi - lw_c)
            bt = kb_ref[sq, sl, :] * e_neg
            kt = k2_ref[sq, sl, :] * e_neg
            rt = rr_ref[sq, sl, :] * jnp.exp(gi)
            vc = vv_ref[sq, sl, :]
            s0 = s_ref[sq]
            for p in range(RWKV_WIDTH // LANES):
                ln = slice(p * LANES, (p + 1) * LANES)
                work.append((sq, ln, tuple(x[:, ln] for x in (pt, bt, kt, rt, vc, gc, s0))))
        results = _lockstep([chunk_pair(*args) for _, _, args in work])
        done = [(sq, ln, res) for (sq, ln, _), res in zip(work, results)]
        for sq, ln, (y, s_new) in done:
            yy_ref[sq, sl, ln] = y
            s_ref[sq, :, ln] = s_new
        return carry

    lax.fori_loop(0, tr // c, chunk, 0)

    inv = 1.0 / HEAD_DIM
    for sq in range(nseq):
        y = yy_ref[sq]
        mu = _dot1(y, bd) * inv
        yc = y - mu
        var = _dot1(yc * yc, bd) * inv
        yn = yc * lax.rsqrt(var + RWKV_GN_EPS) * lg_ref[...] + lb_ref[...]
        y_ref[sq] = (yn + bon_ref[sq]) * gate_ref[sq]
    so_ref[...] = s_ref[...]


def _rwkv_params(p):
    z = jnp.zeros((DECAY_LORA, RWKV_WIDTH), F32)
    wwa = jnp.concatenate([jnp.concatenate([p['rwkv_w2'], z], axis=1),
                           jnp.concatenate([z, p['rwkv_a2']], axis=1)], axis=0)
    row = lambda x: x.reshape(1, -1)
    return (row(p['rwkv_mu']), row(p['rwkv_w0']), wwa, row(p['rwkv_a0']), p['rwkv_g2'], row(p['rwkv_k_k']),
            row(p['rwkv_k_a']), row(p['rwkv_r_k']), row(p['rwkv_ln_g']), row(p['rwkv_ln_b']))


def _rwkv(u3, shift0, s0, params, tr, t_valid, nseq):
    n, t, _ = u3.shape
    assert n % nseq == 0
    consts = _rwkv_consts()
    hb = tr // SUBLANES
    const2 = lambda b, j: (0, 0)
    full = lambda arr: pl.BlockSpec(arr.shape, const2)
    tile = lambda: pltpu.VMEM((nseq, tr, RWKV_WIDTH), F32)
    y, s_out = pl.pallas_call(
        functools.partial(_rwkv_kernel, tr=tr, t_valid=t_valid, nseq=nseq),
        out_shape=(jax.ShapeDtypeStruct((n, t, RWKV_WIDTH), F32),
                   jax.ShapeDtypeStruct((n, HEAD_DIM, RWKV_WIDTH), F32)),
        grid=(n // nseq, t // tr),
        in_specs=[
            pl.BlockSpec((nseq, tr, RWKV_COLS), lambda b, j: (b, j, 0)),
            pl.BlockSpec((nseq, SUBLANES, RWKV_COLS), lambda b, j: (b, jnp.maximum(j * hb - 1, 0), 0)),
            pl.BlockSpec((nseq, 1, RWKV_COLS), lambda b, j: (b, 0, 0)),
            pl.BlockSpec((nseq, HEAD_DIM, RWKV_WIDTH), lambda b, j: (b, 0, 0)),
        ] + [full(x) for x in params] + [full(x) for x in consts],
        out_specs=(pl.BlockSpec((nseq, tr, RWKV_WIDTH), lambda b, j: (b, j, 0)),
                   pl.BlockSpec((nseq, HEAD_DIM, RWKV_WIDTH), lambda b, j: (b, 0, 0))),
        scratch_shapes=[pltpu.VMEM((nseq, SUBLANES + tr, RWKV_COLS), F32)] + [tile() for _ in range(9)]
                       + [pltpu.VMEM((nseq, HEAD_DIM, RWKV_WIDTH), F32)],
        compiler_params=_cparams(("arbitrary", "arbitrary")),
        name="rwkv7",
    )(u3, u3, shift0, s0, *params, *consts)
    return y, s_out


def _state_to_lanes(s):
    n = s.shape[0]
    return jnp.transpose(s, (0, 2, 1, 3)).reshape(n, HEAD_DIM, RWKV_WIDTH)


def _lanes_to_state(s):
    n = s.shape[0]
    return jnp.transpose(s.reshape(n, HEAD_DIM, N_RWKV_HEADS, HEAD_DIM), (0, 2, 1, 3))


def _att_natural_index(dil):
    runs = ATT_RES // dil
    p = np.arange(BLOCK)
    return runs * (p % (BLOCK // runs)) + p // (BLOCK // runs)


def _attn_prompt_kernel(q_ref, kc_ref, kp_ref, vc_ref, vp_ref, bias_ref, o_ref, kcat, vcat, acc, m0, m1, l0, l1):
    u = pl.program_id(2)
    unit = ATT_UNIT
    scale = HEAD_DIM ** -0.5
    kcat[0:unit, :] = kp_ref[...]
    kcat[unit:2 * unit, :] = kc_ref[...]
    vcat[0:unit, :] = vp_ref[...]
    vcat[unit:2 * unit, :] = vc_ref[...]
    lo = lax.broadcasted_iota(jnp.int32, (BLOCK, LANES), 1) < HEAD_DIM
    n_br = len(DILATIONS)
    rows2 = lambda a, b: jnp.concatenate([a, b], axis=0)
    for bi, (window, dil) in enumerate(DILATIONS):
        first = bi == 0
        last = bi == n_br - 1
        shift = int(round(math.log2(dil)))
        n_runs = ATT_RES // dil
        run_len = BLOCK // n_runs
        n_blk = unit // (BLOCK * dil)

        def runs_of(r, g):
            base = (g // n_blk) * unit + (g % n_blk) * run_len
            return [base + (dil * m + r) * BLOCK for m in range(n_runs)]

        def gather(ref, starts):
            return jnp.concatenate([ref[pl.ds(pl.multiple_of(s0, SUBLANES), run_len), :] for s0 in starts], axis=0)

        def sub_block(j, bi, dil, shift, first, last):
            r = j & (dil - 1)
            blk = j >> shift
            q_rows = runs_of(r, blk)
            k_rows = runs_of(r, n_blk + blk - 1) + runs_of(r, n_blk + blk)
            q = gather(q_ref, q_rows) * scale
            kk = gather(kcat, k_rows).astype(BF16)
            vv = gather(vcat, k_rows).astype(BF16)
            q2 = rows2(jnp.where(lo, q, 0.0), jnp.where(lo, 0.0, q)).astype(BF16)
            s = lax.dot_general(q2, kk, NT, preferred_element_type=F32)
            if not first:
                m_old = rows2(gather(m0, q_rows), gather(m1, q_rows))
                l_old = rows2(gather(l0, q_rows), gather(l1, q_rows))
                acc_old = gather(acc, q_rows)
            yield None
            no_prev = ((u == 0) & (blk == 0)).astype(jnp.int32)
            s = s + bias_ref[bi + n_br * no_prev]
            m_cur = jnp.max(s, axis=-1, keepdims=True)
            if first:
                m_new = jnp.broadcast_to(m_cur, (2 * BLOCK, LANES))
            else:
                m_new = jnp.maximum(m_old, m_cur)
                alpha = jnp.exp(m_old - m_new)
            p = jnp.exp(s - jnp.concatenate([m_new, m_new], axis=1))
            l_new = jnp.sum(p, axis=-1, keepdims=True)
            if first:
                l_new = jnp.broadcast_to(l_new, (2 * BLOCK, LANES))
            else:
                l_new = alpha * l_old + l_new
            o2 = lax.dot_general(p.astype(BF16), vv, NN, preferred_element_type=F32)
            yield None
            o_pair = jnp.where(lo, o2[:BLOCK], o2[BLOCK:])
            if not first:
                o_pair = o_pair + jnp.where(lo, alpha[:BLOCK], alpha[BLOCK:]) * acc_old
            if last:
                linv = 1.0 / l_new
                yield [(o_ref, q_rows, o_pair * jnp.where(lo, linv[:BLOCK], linv[BLOCK:]))]
            else:
                yield [(m0, q_rows, m_new[:BLOCK]), (m1, q_rows, m_new[BLOCK:]),
                       (l0, q_rows, l_new[:BLOCK]), (l1, q_rows, l_new[BLOCK:]), (acc, q_rows, o_pair)]

        group = ATT_GROUP[bi]

        def body(jg, carry, bi=bi, dil=dil, shift=shift, first=first, last=last, group=group, run_len=run_len):
            groups = _lockstep([sub_block(jg * group + g, bi, dil, shift, first, last) for g in range(group)])
            for stores in groups:
                for ref, starts, val in stores:
                    for m, s0 in enumerate(starts):
                        ref[pl.ds(pl.multiple_of(s0, SUBLANES), run_len), :] = val[m * run_len:(m + 1) * run_len]
            return carry

        lax.fori_loop(0, unit // BLOCK // group, body, 0)


def _attn_prompt_bias(rel_bias):
    tabs = []
    for _, dil in DILATIONS:
        a = _att_natural_index(dil)
        c = np.concatenate([a, BLOCK + a])
        delta = a[:, None] + BLOCK - c[None, :]
        band = (delta >= 0) & (delta <= BLOCK)
        b = _bias_rows(rel_bias, np.clip(delta, 0, BLOCK) * dil)
        b = jnp.where(jnp.asarray(band)[:, :, None], b, NEG_INF)
        tabs.append(jnp.transpose(b, (2, 0, 1)))
    w = jnp.stack(tabs)
    first = jnp.where(jnp.asarray(np.arange(2 * BLOCK) < BLOCK), NEG_INF, w)
    w = jnp.concatenate([w, first], axis=0)
    w = w.reshape(2 * len(DILATIONS), N_PAIRS, 2 * BLOCK, 2 * BLOCK)
    return jnp.transpose(w, (1, 0, 2, 3))


def _attn_prompt(q, k, v, btab):
    unit = ATT_UNIT
    n_pairs, n, units = q.shape[:3]
    r5 = lambda x: x.reshape(n_pairs, n, units, unit, LANES)
    cur = pl.BlockSpec((None, None, None, unit, LANES), lambda a, p, u: (p, a, u, 0, 0))
    prv = pl.BlockSpec((None, None, None, unit, LANES), lambda a, p, u: (p, a, jnp.maximum(u - 1, 0), 0, 0))
    stat = lambda: pltpu.VMEM((unit, LANES), F32)
    out = pl.pallas_call(
        _attn_prompt_kernel,
        out_shape=jax.ShapeDtypeStruct((n_pairs, n, units, unit, LANES), F32),
        grid=(n, n_pairs, units),
        in_specs=[cur, cur, prv, cur, prv,
                  pl.BlockSpec((None, 2 * len(DILATIONS), 2 * BLOCK, 2 * BLOCK), lambda a, p, u: (p, 0, 0, 0))],
        out_specs=cur,
        scratch_shapes=[pltpu.VMEM((2 * unit, LANES), F32), pltpu.VMEM((2 * unit, LANES), F32),
                        stat(), stat(), stat(), stat(), stat()],
        compiler_params=_cparams(("arbitrary", "arbitrary", "arbitrary")),
        name="attn_prompt",
    )(r5(q), r5(k), r5(k), r5(v), r5(v), btab)
    return out.reshape(q.shape)


def _attn_sample_kernel(q_ref, kn_ref, vn_ref, kc_ref, vc_ref, bt_ref, bn_ref, cnt_ref, cn_ref, hm_ref, o_ref,
                        kn_s, vn_s, *, s_new):
    scale = HEAD_DIM ** -0.5
    hm = hm_ref[...]
    qm = jnp.concatenate([(q_ref[s:s + 1, :] * scale) * hm for s in range(s_new)], axis=0).astype(BF16)
    kn_s[...] = jnp.zeros_like(kn_s)
    vn_s[...] = jnp.zeros_like(vn_s)
    kn_s[0:s_new, :] = kn_ref[...]
    vn_s[0:s_new, :] = vn_ref[...]
    sc = lax.dot_general(qm, kc_ref[...].astype(BF16), NN, preferred_element_type=F32) + bt_ref[...]
    sn = lax.dot_general(qm, kn_s[...].astype(BF16), NT, preferred_element_type=F32) + bn_ref[...]
    cnt = cnt_ref[...]
    cn = cn_ref[...]
    sc = jnp.where(cnt > 0.0, sc, NEG_INF)
    sn = jnp.where(cn > 0.0, sn, NEG_INF)
    m = jnp.maximum(jnp.max(sc, axis=-1, keepdims=True), jnp.max(sn, axis=-1, keepdims=True))
    p = cnt * jnp.exp(sc - m)
    pn = cn * jnp.exp(sn - m)
    l = jnp.sum(p, axis=-1, keepdims=True) + jnp.sum(pn, axis=-1, keepdims=True)
    o = (lax.dot_general(p.astype(BF16), vc_ref[...].astype(BF16), NT, preferred_element_type=F32)
         + lax.dot_general(pn.astype(BF16), vn_s[...].astype(BF16), NN, preferred_element_type=F32))
    o = o / l
    for s in range(s_new):
        o_ref[s:s + 1, :] = jnp.sum(o[s * N_ATT_HEADS:(s + 1) * N_ATT_HEADS] * hm, axis=0, keepdims=True)


def _attn_sample_tables(rel_bias, s_new, win):
    s = np.arange(s_new)[:, None]
    dist_c = win + s - np.arange(win)[None, :]
    dist_n = s - np.arange(LANES)[None, :]
    cnt_c = np.zeros(dist_c.shape, np.float32)
    cnt_n = np.zeros(dist_n.shape, np.float32)
    for window, dil in DILATIONS:
        cnt_c += ((dist_c % dil == 0) & (dist_c <= window)).astype(np.float32)
        cnt_n += ((dist_n >= 0) & (dist_n % dil == 0) & (dist_n <= window)).astype(np.float32)
    rep = lambda x: jnp.asarray(np.repeat(x, N_ATT_HEADS, axis=0))
    rows = lambda b: jnp.transpose(b, (0, 2, 1)).reshape(s_new * N_ATT_HEADS, -1)
    return (rows(_bias_rows(rel_bias, dist_c)), rows(_bias_rows(rel_bias, np.maximum(dist_n, 0))),
            rep(cnt_c), rep(cnt_n))


def _attn_sample(q, k, v, k_cache, v_cache, li, tables, n, s_new):
    win = k_cache.shape[-1]
    assert win == MAX_WINDOW
    lanes = np.arange(ATT_WIDTH)
    hm = jnp.asarray((np.arange(N_ATT_HEADS)[:, None] == lanes[None, :] // HEAD_DIM).astype(np.float32))
    new = pl.BlockSpec((None, s_new, ATT_WIDTH), lambda b: (b, 0, 0))
    buf = pl.BlockSpec((None, None, ATT_WIDTH, win), lambda b: (li, b, 0, 0))
    full = lambda arr: pl.BlockSpec(arr.shape, lambda b: (0, 0))
    r3 = lambda x: x.reshape(n, s_new, ATT_WIDTH)
    out = pl.pallas_call(
        functools.partial(_attn_sample_kernel, s_new=s_new),
        out_shape=jax.ShapeDtypeStruct((n, s_new, ATT_WIDTH), F32),
        grid=(n,),
        in_specs=[new, new, new, buf, buf] + [full(x) for x in tables] + [full(hm)],
        out_specs=new,
        scratch_shapes=[pltpu.VMEM((LANES, ATT_WIDTH), F32), pltpu.VMEM((LANES, ATT_WIDTH), F32)],
        compiler_params=_cparams(("arbitrary",)),
        name="attn_sample",
    )(r3(q), r3(k), r3(v), k_cache, v_cache, *tables, hm)
    return out.reshape(n * s_new, ATT_WIDTH)


def _mix_ffn_kernel(ya_ref, yb_ref, yc_ref, x_ref, gate1_ref, gmix_ref, wout_ref,
                    sc_ref, sh_ref, gate2_ref, gpre_ref, gpost_ref, wg_ref, wu_ref, wo_ref, *refs, residue_major):
    a = CONV_WIDTH
    b = CONV_WIDTH + RWKV_WIDTH
    if residue_major:
        permt_ref, o_ref = refs
        yc = jnp.concatenate([jnp.concatenate([yc_ref[p, r] for r in range(ATT_RES)], axis=0)
                              for p in range(N_PAIRS)], axis=1).astype(BF16)
        yc = lax.dot_general(permt_ref[...], yc, NN, preferred_element_type=F32)
    else:
        o_ref, = refs
        yc = jnp.concatenate([yc_ref[p] for p in range(N_PAIRS)], axis=1)
    mix = (_dot1(ya_ref[...], wout_ref[0:a, :]) + _dot1(yb_ref[...], wout_ref[a:b, :])
           + _dot1(yc, wout_ref[b:, :]))
    ms = jnp.mean(mix * mix, axis=-1, keepdims=True)
    x = x_ref[...] + gate1_ref[...] * (mix * lax.rsqrt(ms + RMS_EPS) * gmix_ref[...])

    ms = jnp.mean(x * x, axis=-1, keepdims=True)
    h = x * lax.rsqrt(ms + RMS_EPS) * gpre_ref[...]
    hb = (h * (1.0 + sc_ref[...]) + sh_ref[...]).astype(BF16)
    y = None
    for c in range(FFN_SPLIT):
        cols = slice(c * (D_FF // FFN_SPLIT), (c + 1) * (D_FF // FFN_SPLIT))
        gg = lax.dot_general(hb, wg_ref[:, cols], NN, preferred_element_type=F32)
        uu = lax.dot_general(hb, wu_ref[:, cols], NN, preferred_element_type=F32)
        act = (_silu(gg) * uu).astype(BF16)
        part = lax.dot_general(act, wo_ref[cols, :], NN, preferred_element_type=F32)
        y = part if y is None else y + part
    ms = jnp.mean(y * y, axis=-1, keepdims=True)
    o_ref[...] = x + gate2_ref[...] * (y * lax.rsqrt(ms + RMS_EPS) * gpost_ref[...])


def _mix_ffn(ya, yb, yc, x2, gate1, g_mix, w_out, scale2, shift2, gate2, g_pre, g_post, wg, wu, wo, tm,
             tiles_per_group):
    m = x2.shape[0]
    rows = lambda w: pl.BlockSpec((tm, w), lambda i: (i, 0))
    vec = pl.BlockSpec((1, D_MODEL), lambda i: (0, 0))
    resident = lambda arr: pl.BlockSpec(arr.shape, lambda i: (0, 0), pipeline_mode=pl.Buffered(1))
    residue_major = yc.ndim == 6
    extra, extra_specs = [], []
    if residue_major:
        t = yc.shape[2] * ATT_UNIT
        per_unit = ATT_UNIT // tm
        per_seq = t // tm
        yc_spec = pl.BlockSpec((N_PAIRS, None, None, ATT_RES, tm // ATT_RES, LANES),
                               lambda i: (0, i // per_seq, (i % per_seq) // per_unit, 0, i % per_unit, 0))
        _, permt = _tile_perm(tm)
        extra, extra_specs = [permt], [resident(permt)]
    else:
        yc_spec = pl.BlockSpec((N_PAIRS, tm, LANES), lambda i: (0, i, 0))
    mod = lambda arr: _mod_spec(arr, tiles_per_group)
    return pl.pallas_call(
        functools.partial(_mix_ffn_kernel, residue_major=residue_major),
        out_shape=jax.ShapeDtypeStruct((m, D_MODEL), F32),
        grid=(m // tm,),
        in_specs=[rows(CONV_WIDTH), rows(RWKV_WIDTH), yc_spec, rows(D_MODEL), mod(gate1), vec, resident(w_out),
                  mod(scale2), mod(shift2), mod(gate2), vec, vec, resident(wg), resident(wu), resident(wo)]
                 + extra_specs,
        out_specs=rows(D_MODEL),
        compiler_params=_cparams(("arbitrary",)),
        name="mix_ffn",
    )(ya, yb, yc, x2, gate1, g_mix.reshape(1, D_MODEL), w_out, scale2, shift2, gate2,
      g_pre.reshape(1, D_MODEL), g_post.reshape(1, D_MODEL), wg, wu, wo, *extra)


def _window_to_heads(xt):
    n, _, win = xt.shape
    return jnp.transpose(xt.reshape(n, N_ATT_HEADS, HEAD_DIM, win), (0, 3, 1, 2))


def _layer(x2, n, t, mods, lp, att, state, li):
    shift1, scale1, gate1, shift2, scale2, gate2 = mods
    prompt = state is None
    tm = 512 if prompt else n * t
    tpg = t // tm if prompt else 1
    conv_params = _conv_params(lp['conv_w'], lp['conv_b'], lp['conv_ln_g'], lp['conv_ln_b'])
    rwkv_params = _rwkv_params(lp)
    if prompt:
        ya, zt, ur, q, k, v, *windows = _in_proj(x2, scale1, shift1, lp['g_pre_mix'], lp['w_in_bf'], tm, tpg,
                                                 seq_shape=(n, t), conv_params=conv_params)
        conv_cache = zt[:, CONV_HALO - (CONV_K - 1):]
    else:
        uc, ur, q, k, v = _in_proj(x2, scale1, shift1, lp['g_pre_mix'], lp['w_in_bf'], tm, tpg)
    ur3 = ur.reshape(n, t, RWKV_COLS)
    new_shift = ur3[:, -1]
    if prompt:
        shift0 = jnp.zeros((n, 1, RWKV_COLS), F32)
        s0 = jnp.zeros((n, HEAD_DIM, RWKV_WIDTH), F32)
        yb, s_out = _rwkv(ur3, shift0, s0, rwkv_params, tr=512, t_valid=512, nseq=4)
        yc = _attn_prompt(q, k, v, att['prompt'])
        assert min(MAX_WINDOW, t) == ATT_UNIT
        new_k, new_v = (_window_to_heads(w) for w in windows)
    else:
        cache_conv, state_shift, state_wkv, k_cache, v_cache = state
        ya, conv_cache = _conv_sample(uc, cache_conv[li], n, t, conv_params)
        tr = RWKV_CHUNK
        ur_pad = jnp.pad(ur3, ((0, 0), (0, tr - t), (0, 0)))
        yb, s_out = _rwkv(ur_pad, state_shift[li][:, None, :], _state_to_lanes(state_wkv[li]), rwkv_params,
                          tr=tr, t_valid=t, nseq=8)
        yb = yb[:, :t]
        q_r, k_r, v_r = _pairs_to_rows(q), _pairs_to_rows(k), _pairs_to_rows(v)
        yc = _rows_to_pairs(_attn_sample(q_r, k_r, v_r, k_cache, v_cache, li, att['sample'], n, t))
        new_k = k_r.reshape(n, t, N_ATT_HEADS, HEAD_DIM)
        new_v = v_r.reshape(n, t, N_ATT_HEADS, HEAD_DIM)
    yb = yb.reshape(n * t, RWKV_WIDTH)
    x2 = _mix_ffn(ya, yb, yc, x2, gate1, lp['g_post_mix'], lp['w_out_bf'], scale2, shift2, gate2,
                  lp['g_pre_ffn'], lp['g_post_ffn'], lp['w_ffn_g_bf'], lp['w_ffn_u_bf'], lp['w_ffn_o_bf'], tm, tpg)
    return x2, (conv_cache, new_shift, _lanes_to_state(s_out), new_k, new_v)


def kernel(x_prompt, x_sample, c_prompt, c_sample, cache_conv, state_shift, state_wkv, cache_k_win, cache_v_win,
           w_ada, b_ada, g_pre_mix, g_post_mix, g_pre_ffn, g_post_ffn, w_in, w_out,
           conv_w, conv_b, conv_ln_g, conv_ln_b,
           rwkv_mu, rwkv_w0, rwkv_w2, rwkv_a0, rwkv_a2, rwkv_g2, rwkv_k_k, rwkv_k_a, rwkv_r_k,
           rwkv_ln_g, rwkv_ln_b, rel_bias, w_ffn_in, w_ffn_out):
    depth = w_in.shape[0]
    nb, tp, _ = x_prompt.shape
    ns, ts, _ = x_sample.shape
    rows = nb + ns
    rows_pad = -(-rows // SUBLANES) * SUBLANES
    c_all = jnp.pad(jnp.concatenate([c_prompt, c_sample], axis=0), ((0, rows_pad - rows), (0, 0)))
    mod = _ada_modulation(c_all, w_ada, b_ada)
    mod = mod.reshape(depth, rows_pad, 6, D_MODEL)

    win = cache_k_win.shape[2]
    att = {'prompt': _attn_prompt_bias(rel_bias), 'sample': _attn_sample_tables(rel_bias, ts, win)}
    to_pos_minor = lambda c: jnp.transpose(c, (0, 1, 3, 4, 2)).reshape(depth, ns, ATT_WIDTH, win)
    state = (cache_conv, state_shift, state_wkv, to_pos_minor(cache_k_win), to_pos_minor(cache_v_win))

    yp = x_prompt.reshape(nb * tp, D_MODEL)
    ys = x_sample.reshape(ns * ts, D_MODEL)
    outs_p, outs_s = [], []
    for li in range(depth):
        lp = dict(g_pre_mix=g_pre_mix[li], g_post_mix=g_post_mix[li], g_pre_ffn=g_pre_ffn[li],
                  g_post_ffn=g_post_ffn[li],
                  w_in_bf=w_in[li].astype(BF16), w_out_bf=w_out[li].astype(BF16),
                  w_ffn_g_bf=w_ffn_in[li, :, :D_FF].astype(BF16), w_ffn_u_bf=w_ffn_in[li, :, D_FF:].astype(BF16),
                  w_ffn_o_bf=w_ffn_out[li].astype(BF16),
                  conv_w=conv_w[li], conv_b=conv_b[li], conv_ln_g=conv_ln_g[li], conv_ln_b=conv_ln_b[li],
                  rwkv_mu=rwkv_mu[li], rwkv_w0=rwkv_w0[li], rwkv_w2=rwkv_w2[li], rwkv_a0=rwkv_a0[li],
                  rwkv_a2=rwkv_a2[li], rwkv_g2=rwkv_g2[li], rwkv_k_k=rwkv_k_k[li], rwkv_k_a=rwkv_k_a[li],
                  rwkv_r_k=rwkv_r_k[li], rwkv_ln_g=rwkv_ln_g[li], rwkv_ln_b=rwkv_ln_b[li])
        mods_p = tuple(mod[li, :nb, j][:, None, :] for j in range(6))
        mods_s = tuple(jnp.repeat(mod[li, nb:rows, j], ts, axis=0)[None] for j in range(6))
        yp, st_p = _layer(yp, nb, tp, mods_p, lp, att, None, li)
        ys, st_s = _layer(ys, ns, ts, mods_s, lp, att, state, li)
        outs_p.append(st_p)
        outs_s.append(st_s)
    stack = lambda outs, i: jnp.stack([o[i] for o in outs])
    return (yp.reshape(nb, tp, D_MODEL), ys.reshape(ns, ts, D_MODEL),
            stack(outs_p, 0), stack(outs_s, 0),
            stack(outs_p, 1), stack(outs_s, 1),
            stack(outs_p, 2), stack(outs_s, 2),
            stack(outs_p, 3), stack(outs_s, 3),
            stack(outs_p, 4), stack(outs_s, 4))
```

```python
import functools
import math

import numpy as np
import jax
import jax.numpy as jnp
from jax import lax
from jax.experimental import pallas as pl
from jax.experimental.pallas import tpu as pltpu

F32 = jnp.float32
BF16 = jnp.bfloat16

D_MODEL = 1024
HEAD_DIM = 64
CONV_WIDTH = 256
RWKV_WIDTH = 256
ATT_WIDTH = 512
N_RWKV_HEADS = RWKV_WIDTH // HEAD_DIM
N_ATT_HEADS = ATT_WIDTH // HEAD_DIM
CONV_K = 31
DECAY_LORA = 64
AAA_LORA = 64
GATE_LORA = 128
CONV_COLS = 2 * CONV_WIDTH
RWKV_COLS = 3 * RWKV_WIDTH + DECAY_LORA + AAA_LORA + GATE_LORA
ATT_COLS = 3 * ATT_WIDTH
IN_COLS = CONV_COLS + RWKV_COLS + ATT_COLS
DILATIONS = ((128, 1), (512, 4), (2048, 16))
MAX_WINDOW = 2048
BLOCK = 128
N_REL_BUCKETS = 32
REL_EXACT = N_REL_BUCKETS // 2
REL_MAX_DIST = MAX_WINDOW
D_FF = 2816
RMS_EPS = 1e-6
LN_EPS = 1e-5
RWKV_GN_EPS = HEAD_DIM * 1e-5
DECAY_SCALE = math.exp(-0.5)
NEG_INF = -1e30

VMEM_LIMIT_BYTES = 56 * 1024 * 1024
SUBLANES = 8
LANES = 128
ROW_TILE = 512
RWKV_SEQS_PROMPT = 4
RWKV_SEQS_SAMPLE = 8
CONV_HALO = 32
CONV_SUB = 64
FFN_SPLIT = 1
RWKV_CHUNK = HEAD_DIM
N_PAIRS = ATT_WIDTH // LANES
ATT_RES = max(d for _, d in DILATIONS)
ATT_UNIT = BLOCK * ATT_RES
ATT_GROUP = (16, 16, 16)

NN = (((1,), (0,)), ((), ()))
NT = (((1,), (1,)), ((), ()))


def _cparams(sem):
    return pltpu.CompilerParams(dimension_semantics=sem, vmem_limit_bytes=VMEM_LIMIT_BYTES)


def _split_bf16(x, n):
    if x.dtype == BF16:
        return [x]
    pieces = []
    r = x
    for i in range(n):
        p = r.astype(BF16)
        pieces.append(p)
        if i + 1 < n:
            r = r - p.astype(F32)
    return pieces


def _mmp(ap, bp, dims=NN):
    order = max(len(ap), len(bp))
    out = None
    for i, x in enumerate(ap):
        for j, y in enumerate(bp):
            if i + j < order:
                t = lax.dot_general(x, y, dims, preferred_element_type=F32)
                out = t if out is None else out + t
    return out


def _dot1(a, b, dims=NN):
    return lax.dot_general(a.astype(BF16), b.astype(BF16), dims, preferred_element_type=F32)


def _lockstep(gens):
    results = [None] * len(gens)
    while any(r is None for r in results):
        results = [next(g) if r is None else r for g, r in zip(gens, results)]
    return results


def _sigmoid(x):
    return 1.0 / (1.0 + jnp.exp(-x))


def _silu(x):
    return x * _sigmoid(x)


def _rel_bucket_np(dist):
    d = np.maximum(dist, 1).astype(np.float32)
    large = REL_EXACT + (np.log(d / np.float32(REL_EXACT)) / np.float32(math.log(REL_MAX_DIST / REL_EXACT))
                         * np.float32(N_REL_BUCKETS - REL_EXACT)).astype(np.int32)
    large = np.minimum(large, N_REL_BUCKETS - 1)
    return np.where(dist < REL_EXACT, dist, large)


def _bias_rows(rel_bias, dist):
    idx = _rel_bucket_np(np.asarray(dist, np.int32))
    onehot = jnp.asarray(np.eye(N_REL_BUCKETS, dtype=np.float32)[idx.reshape(-1)])
    rows = jnp.dot(onehot, rel_bias, precision=lax.Precision.HIGHEST)
    return rows.reshape(idx.shape + (rel_bias.shape[1],))


def _ada_kernel(c_ref, w_ref, b_ref, o_ref):
    a = _silu(c_ref[...])
    o_ref[...] = _dot1(a, w_ref[...]) + b_ref[...]


def _ada_modulation(c_all, w_ada, b_ada):
    depth = w_ada.shape[0]
    rows = c_all.shape[0]
    tn = D_MODEL
    return pl.pallas_call(
        _ada_kernel,
        out_shape=jax.ShapeDtypeStruct((depth, rows, 6 * D_MODEL), F32),
        grid=(depth, 6 * D_MODEL // tn),
        in_specs=[
            pl.BlockSpec((rows, D_MODEL), lambda l, j: (0, 0)),
            pl.BlockSpec((None, D_MODEL, tn), lambda l, j: (l, 0, j)),
            pl.BlockSpec((None, 1, tn), lambda l, j: (l, 0, j)),
        ],
        out_specs=pl.BlockSpec((None, rows, tn), lambda l, j: (l, 0, j)),
        compiler_params=_cparams(("arbitrary", "arbitrary")),
        name="ada_modulation",
    )(c_all, w_ada, b_ada.reshape(depth, 1, 6 * D_MODEL))


def _in_proj_kernel(x_ref, sc_ref, sh_ref, g_ref, w_ref, *refs, residue_major, tiles_per_seq=None):
    if residue_major:
        (perm_ref, wkt_ref, wvt_ref, cw_ref, cb_ref, clg_ref, clb_ref,
         ya_ref, zt_ref, ur_ref, q_ref, k_ref, v_ref, kt_ref, vt_ref, zs_ref, zsh_ref) = refs
    else:
        uc_ref, ur_ref, q_ref, k_ref, v_ref = refs
    x = x_ref[...]
    ms = jnp.mean(x * x, axis=-1, keepdims=True)
    h = x * lax.rsqrt(ms + RMS_EPS) * g_ref[...]
    h = h * (1.0 + sc_ref[...]) + sh_ref[...]
    hb = h.astype(BF16)
    uc = lax.dot_general(hb, w_ref[:, 0:CONV_COLS], NN, preferred_element_type=F32)
    if residue_major:
        tt = uc.shape[0]

        @pl.when(pl.program_id(0) % tiles_per_seq == 0)
        def _():
            zs_ref[0:CONV_HALO, :] = jnp.zeros((CONV_HALO, CONV_WIDTH), F32)

        zs_ref[CONV_HALO:CONV_HALO + tt, :] = _glu(uc)
    else:
        uc_ref[...] = uc

    def projections():
        o = CONV_COLS
        half = RWKV_COLS // 2
        for c in range(2):
            ur_ref[:, c * half:(c + 1) * half] = lax.dot_general(hb, w_ref[:, o:o + half], NN,
                                                                 preferred_element_type=F32)
            o += half
            yield None
        hq = hb
        if residue_major:
            hq = lax.dot_general(perm_ref[...], hb, NN, preferred_element_type=F32).astype(BF16)
            run = hb.shape[0] // ATT_RES
            yield None
        for ref in (q_ref, k_ref, v_ref):
            res = lax.dot_general(hq, w_ref[:, o:o + ATT_WIDTH], NN, preferred_element_type=F32)
            for p in range(N_PAIRS):
                cols = res[:, p * LANES:(p + 1) * LANES]
                if residue_major:
                    for r in range(ATT_RES):
                        ref[p, r] = cols[r * run:(r + 1) * run]
                else:
                    ref[p] = cols
            o += ATT_WIDTH
            yield None
        yield True

    if residue_major:
        _lockstep([projections(), _conv_stages(zs_ref, zsh_ref, cw_ref, cb_ref, clg_ref, clb_ref, ya_ref, tt)])
        tail = zs_ref[tt:tt + CONV_HALO, :]
        zt_ref[...] = tail
        zs_ref[0:CONV_HALO, :] = tail

        @pl.when(pl.program_id(0) % tiles_per_seq >= tiles_per_seq - ATT_UNIT // hb.shape[0])
        def _():
            kt_ref[...] = lax.dot_general(wkt_ref[...], hb, NT, preferred_element_type=F32)
            vt_ref[...] = lax.dot_general(wvt_ref[...], hb, NT, preferred_element_type=F32)
    else:
        _lockstep([projections()])


def _tile_perm(tm):
    run = tm // ATT_RES
    t = np.arange(tm)
    perm = np.zeros((tm, tm), np.float32)
    perm[(t % ATT_RES) * run + t // ATT_RES, t] = 1.0
    return jnp.asarray(perm, BF16), jnp.asarray(perm.T, BF16)


def _mod_spec(mod, tiles_per_group):
    _, r, d = mod.shape
    return pl.BlockSpec((None, r, d), lambda i: (i // tiles_per_group, 0, 0))


def _in_proj(x2, scale, shift, g, w_bf, tm, tiles_per_group, seq_shape=None, conv_params=None):
    m = x2.shape[0]
    flat = lambda w: jax.ShapeDtypeStruct((m, w), F32)
    row_spec = lambda w: pl.BlockSpec((tm, w), lambda i: (i, 0))
    residue_major = seq_shape is not None
    ins = [x2, scale, shift, g.reshape(1, D_MODEL), w_bf]
    in_specs = [
        pl.BlockSpec((tm, D_MODEL), lambda i: (i, 0)),
        _mod_spec(scale, tiles_per_group),
        _mod_spec(shift, tiles_per_group),
        pl.BlockSpec((1, D_MODEL), lambda i: (0, 0)),
        pl.BlockSpec((D_MODEL, IN_COLS), lambda i: (0, 0)),
    ]
    if residue_major:
        n, t = seq_shape
        per_unit = ATT_UNIT // tm
        per_seq = t // tm
        rows_res = ATT_UNIT // ATT_RES
        pairs = jax.ShapeDtypeStruct((N_PAIRS, n, t // ATT_UNIT, ATT_RES, rows_res, LANES), F32)
        pair_spec = pl.BlockSpec((N_PAIRS, None, None, ATT_RES, tm // ATT_RES, LANES),
                                 lambda i: (0, i // per_seq, (i % per_seq) // per_unit, 0, i % per_unit, 0))
        perm, _ = _tile_perm(tm)
        k0 = CONV_COLS + RWKV_COLS + ATT_WIDTH
        wkt = w_bf[:, k0:k0 + ATT_WIDTH].T
        wvt = w_bf[:, k0 + ATT_WIDTH:k0 + 2 * ATT_WIDTH].T
        ins += [perm, wkt, wvt, *conv_params]
        in_specs += [pl.BlockSpec((tm, tm), lambda i: (0, 0)),
                     pl.BlockSpec((ATT_WIDTH, D_MODEL), lambda i: (0, 0)),
                     pl.BlockSpec((ATT_WIDTH, D_MODEL), lambda i: (0, 0))] + _conv_param_specs()
        win_t = jax.ShapeDtypeStruct((n, ATT_WIDTH, ATT_UNIT), F32)
        win_spec = pl.BlockSpec((None, ATT_WIDTH, tm),
                                lambda i: (i // per_seq, 0, jnp.maximum(i % per_seq - (per_seq - per_unit), 0)))
        extra_out, extra_specs = (win_t, win_t), (win_spec, win_spec)
        conv_out = (flat(CONV_WIDTH), jax.ShapeDtypeStruct((n, CONV_HALO, CONV_WIDTH), F32))
        conv_specs = (row_spec(CONV_WIDTH), pl.BlockSpec((None, CONV_HALO, CONV_WIDTH), lambda i: (i // per_seq, 0, 0)))
        scratch = [pltpu.VMEM((CONV_HALO + tm + SUBLANES, CONV_WIDTH), F32),
                   pltpu.VMEM((SUBLANES, CONV_HALO + tm, CONV_WIDTH), F32)]
        kernel_fn = functools.partial(_in_proj_kernel, residue_major=True, tiles_per_seq=per_seq)
    else:
        pairs = jax.ShapeDtypeStruct((N_PAIRS, m, LANES), F32)
        pair_spec = pl.BlockSpec((N_PAIRS, tm, LANES), lambda i: (0, i, 0))
        extra_out, extra_specs = (), ()
        conv_out, conv_specs, scratch = (flat(CONV_COLS),), (row_spec(CONV_COLS),), []
        kernel_fn = functools.partial(_in_proj_kernel, residue_major=False)
    return pl.pallas_call(
        kernel_fn,
        out_shape=conv_out + (flat(RWKV_COLS), pairs, pairs, pairs) + extra_out,
        grid=(m // tm,),
        in_specs=in_specs,
        out_specs=conv_specs + (row_spec(RWKV_COLS), pair_spec, pair_spec, pair_spec) + extra_specs,
        scratch_shapes=scratch,
        compiler_params=_cparams(("arbitrary",)),
        name="in_proj",
    )(*ins)


def _pairs_to_rows(x):
    return jnp.transpose(x, (1, 0, 2)).reshape(x.shape[1], ATT_WIDTH)


def _rows_to_pairs(x):
    return jnp.transpose(x.reshape(x.shape[0], N_PAIRS, LANES), (1, 0, 2))


def _conv_stages(zs_ref, zsh_ref, w_ref, b_ref, lg_ref, lb_ref, y_ref, tt):
    n_rows = CONV_HALO + tt
    zs_ref[n_rows:n_rows + SUBLANES, :] = jnp.zeros((SUBLANES, CONV_WIDTH), F32)
    for s in range(SUBLANES):
        zsh_ref[s] = zs_ref[pl.ds(s, n_rows), :]
    off = CONV_HALO - (CONV_K - 1)
    sub = min(tt, CONV_SUB)
    for r0 in range(0, tt, sub):
        acc = jnp.zeros((sub, CONV_WIDTH), F32)
        for j in range(CONV_K):
            a, b = divmod(off + j, SUBLANES)
            acc = acc + w_ref[j:j + 1, :] * zsh_ref[b, pl.ds(r0 + SUBLANES * a, sub), :]
        y = acc + b_ref[...]
        mu = jnp.mean(y, axis=-1, keepdims=True)
        yc = y - mu
        var = jnp.mean(yc * yc, axis=-1, keepdims=True)
        yn = yc * lax.rsqrt(var + LN_EPS) * lg_ref[...] + lb_ref[...]
        y_ref[r0:r0 + sub, :] = _silu(yn)
        yield None
    yield True


def _conv_core(*args):
    _lockstep([_conv_stages(*args)])


def _glu(u):
    return u[:, :CONV_WIDTH] * _sigmoid(u[:, CONV_WIDTH:])


def _conv_sample_kernel(u_ref, zh_ref, w_ref, b_ref, lg_ref, lb_ref, y_ref, z_ref, zs_ref, zsh_ref, *, tt):
    z = _glu(u_ref[...])
    zs_ref[0:CONV_HALO, :] = zh_ref[...]
    zs_ref[CONV_HALO:CONV_HALO + tt, :] = z
    _conv_core(zs_ref, zsh_ref, w_ref, b_ref, lg_ref, lb_ref, y_ref, tt)
    z_ref[...] = z


def _conv_param_specs():
    cmap = lambda i: (0, 0)
    return [
        pl.BlockSpec((CONV_HALO, CONV_WIDTH), cmap),
        pl.BlockSpec((1, CONV_WIDTH), cmap),
        pl.BlockSpec((1, CONV_WIDTH), cmap),
        pl.BlockSpec((1, CONV_WIDTH), cmap),
    ]


def _conv_params(conv_w, conv_b, ln_g, ln_b):
    w = jnp.pad(conv_w, ((0, CONV_HALO - CONV_K), (0, 0)))
    return w, conv_b.reshape(1, -1), ln_g.reshape(1, -1), ln_b.reshape(1, -1)


def _conv_sample(uc, cache, n, t, params):
    tt = SUBLANES
    u3 = jnp.pad(uc.reshape(n, t, CONV_COLS), ((0, 0), (0, tt - t), (0, 0)))
    zh = jnp.pad(cache, ((0, 0), (CONV_HALO - (CONV_K - 1), 0), (0, 0)))
    y, z = pl.pallas_call(
        functools.partial(_conv_sample_kernel, tt=tt),
        out_shape=(jax.ShapeDtypeStruct((n, tt, CONV_WIDTH), F32),
                   jax.ShapeDtypeStruct((n, tt, CONV_WIDTH), F32)),
        grid=(n,),
        in_specs=[
            pl.BlockSpec((None, tt, CONV_COLS), lambda b: (b, 0, 0)),
            pl.BlockSpec((None, CONV_HALO, CONV_WIDTH), lambda b: (b, 0, 0)),
        ] + _conv_param_specs(),
        out_specs=(pl.BlockSpec((None, tt, CONV_WIDTH), lambda b: (b, 0, 0)),
                   pl.BlockSpec((None, tt, CONV_WIDTH), lambda b: (b, 0, 0))),
        scratch_shapes=[pltpu.VMEM((CONV_HALO + tt + SUBLANES, CONV_WIDTH), F32),
                        pltpu.VMEM((SUBLANES, CONV_HALO + tt, CONV_WIDTH), F32)],
        compiler_params=_cparams(("arbitrary",)),
        name="conv_sample",
    )(u3, zh, *params)
    new_cache = jnp.concatenate([cache[:, t:], z[:, :t]], axis=1)
    return y[:, :t].reshape(n * t, CONV_WIDTH), new_cache


def _rwkv_consts():
    c = RWKV_CHUNK
    rows = np.arange(c)
    tri = (rows[None, :] <= rows[:, None]).astype(np.float32)
    lanes = np.arange(RWKV_WIDTH)
    bd = (lanes[:, None] // HEAD_DIM == lanes[None, :] // HEAD_DIM).astype(np.float32)
    return jnp.asarray(tri, BF16), jnp.asarray(bd, BF16)


def _rwkv_kernel(u_ref, up_ref, sh0_ref, s0_ref, mu_ref, w0_ref, wwa_ref, a0_ref, g2_ref, kk_ref, ka_ref, rk_ref,
                 lg_ref, lb_ref, tri_ref, bd_ref,
                 y_ref, so_ref,
                 us_ref, lw_ref, kn_ref, kb_ref, k2_ref, rr_ref, vv_ref, bon_ref, gate_ref, yy_ref, s_ref,
                 *, tr, t_valid, nseq):
    j = pl.program_id(1)
    c = RWKV_CHUNK
    bd = bd_ref[...]
    o = 3 * RWKV_WIDTH

    @pl.when(j == 0)
    def _():
        s_ref[...] = s0_ref[...]

    for sq in range(nseq):
        u = u_ref[sq]
        prev = jnp.where(j > 0, up_ref[sq], jnp.broadcast_to(sh0_ref[sq], (SUBLANES, RWKV_COLS)))
        us_ref[sq, 0:SUBLANES, :] = prev
        us_ref[sq, SUBLANES:SUBLANES + tr, :] = u
        u_prev = us_ref[sq, pl.ds(SUBLANES - 1, tr), :]
        xs = u + (u_prev - u) * mu_ref[...]
        r = xs[:, :RWKV_WIDTH]
        k = xs[:, RWKV_WIDTH:2 * RWKV_WIDTH]
        v = xs[:, 2 * RWKV_WIDTH:o]
        lo = xs[:, o:o + DECAY_LORA + AAA_LORA]
        lane = lax.broadcasted_iota(jnp.int32, lo.shape, 1)
        lo = jnp.where(lane < DECAY_LORA, jnp.tanh(lo), lo)
        wa = _dot1(lo, wwa_ref[...])
        g = _dot1(_sigmoid(xs[:, o + DECAY_LORA + AAA_LORA:]), g2_ref[...])
        lw = -DECAY_SCALE * _sigmoid(w0_ref[...] + wa[:, :RWKV_WIDTH])
        a = _sigmoid(a0_ref[...] + wa[:, RWKV_WIDTH:])
        kkr = k * kk_ref[...]
        ss = _dot1(kkr * kkr, bd)
        kk = kkr * lax.rsqrt(jnp.maximum(ss, 1e-24))
        k2 = k * (1.0 + (a - 1.0) * ka_ref[...])
        if t_valid < tr:
            valid = lax.broadcasted_iota(jnp.int32, (tr, RWKV_WIDTH), 0) < t_valid
            zero = jnp.zeros_like(r)
            r, k2, v, kk, lw = (jnp.where(valid, z, zero) for z in (r, k2, v, kk, lw))
        lw_ref[sq] = lw
        kn_ref[sq] = -kk
        kb_ref[sq] = kk * a
        k2_ref[sq] = k2
        rr_ref[sq] = r
        vv_ref[sq] = v
        bon_ref[sq] = _dot1(r * k2 * rk_ref[...], bd) * v
        gate_ref[sq] = g

    row = lax.broadcasted_iota(jnp.int32, (c, LANES), 0)
    col = lax.broadcasted_iota(jnp.int32, (c, LANES), 1) & (HEAD_DIM - 1)
    strict = col < row
    incl = col <= row
    ident = jnp.where(col == row, 1.0, 0.0)
    bd2 = bd[0:LANES, 0:LANES]
    bd2_mask = bd2 > 0.5
    n_steps = int(round(math.log2(c)))

    def dot(a, b, dims=NN):
        return lax.dot_general(a.astype(BF16), b, dims, preferred_element_type=F32)

    def stack(x):
        xb = x.astype(BF16)
        return jnp.concatenate([xb, xb], axis=0) * bd2

    def rows2(a, b):
        return jnp.concatenate([a, b], axis=0)

    def chunk_pair(pt, bt, kt, rt, vc, gc, s0):
        zero = jnp.zeros((c, LANES), F32)
        pr = rows2(pt, rt)
        xb = dot(pr, stack(bt), NT)
        xk = dot(pr, stack(kt), NT)
        yield None
        lm = jnp.where(strict, xb[:c], zero)
        mm_ = jnp.where(strict, xk[:c], zero)
        qb = jnp.where(incl, xb[c:], zero)
        qk = jnp.where(incl, xk[c:], zero)
        tinv = ident + lm
        lk = dot(lm, stack(lm))
        mq = dot(rows2(mm_, qk), stack(vc))
        yield None
        for step in range(1, n_steps):
            if step + 1 < n_steps:
                both = dot(rows2(lk, tinv), stack(lk))
                lk = both[:c]
                tinv = tinv + both[c:]
            else:
                tinv = tinv + dot(tinv, stack(lk))
            yield None
        p2 = dot(tinv, stack(pt))
        u0 = dot(tinv, stack(mq[:c]))
        yield None
        xs_ = dot(rows2(p2, rt), stack(s0), NT)
        yield None
        uu = xs_[:c] + u0
        y = xs_[c:] + dot(qb, stack(uu)) + mq[c:]
        uvt = rows2(uu, vc).T
        bk = rows2(bt * gc, kt * gc).astype(BF16)
        z = jnp.where(bd2_mask, dot(uvt, bk), 0.0)
        yield y, s0 * gc + z[0:c] + z[c:2 * c]

    def chunk(ci, carry):
        sl = pl.ds(pl.multiple_of(ci * c, c), c)
        work = []
        for sq in range(nseq):
            lw_c = lw_ref[sq, sl, :]
            gi = _mmp([tri_ref[...]], _split_bf16(lw_c, 3))
            e_neg = jnp.exp(-gi)
            gc = jnp.exp(gi[c - 1:c, :])
            pt = kn_ref[sq, sl, :] * jnp.exp(gi - lw_c)
            bt = kb_ref[sq, sl, :] * e_neg
            kt = k2_ref[sq, sl, :] * e_neg
            rt = rr_ref[sq, sl, :] * jnp.exp(gi)
            vc = vv_ref[sq, sl, :]
            s0 = s_ref[sq]
            for p in range(RWKV_WIDTH // LANES):
                ln = slice(p * LANES, (p + 1) * LANES)
                work.append((sq, ln, tuple(x[:, ln] for x in (pt, bt, kt, rt, vc, gc, s0))))
        results = _lockstep([chunk_pair(*args) for _, _, args in work])
        done = [(sq, ln, res) for (sq, ln, _), res in zip(work, results)]
        for sq, ln, (y, s_new) in done:
            yy_ref[sq, sl, ln] = y
            s_ref[sq, :, ln] = s_new
        return carry

    lax.fori_loop(0, tr // c, chunk, 0)

    inv = 1.0 / HEAD_DIM
    for sq in range(nseq):
        y = yy_ref[sq]
        mu = _dot1(y, bd) * inv
        yc = y - mu
        var = _dot1(yc * yc, bd) * inv
        yn = yc * lax.rsqrt(var + RWKV_GN_EPS) * lg_ref[...] + lb_ref[...]
        y_ref[sq] = (yn + bon_ref[sq]) * gate_ref[sq]
    so_ref[...] = s_ref[...]


def _rwkv_params(p):
    z = jnp.zeros((DECAY_LORA, RWKV_WIDTH), F32)
    wwa = jnp.concatenate([jnp.concatenate([p['rwkv_w2'], z], axis=1),
                           jnp.concatenate([z, p['rwkv_a2']], axis=1)], axis=0)
    row = lambda x: x.reshape(1, -1)
    return (row(p['rwkv_mu']), row(p['rwkv_w0']), wwa, row(p['rwkv_a0']), p['rwkv_g2'], row(p['rwkv_k_k']),
            row(p['rwkv_k_a']), row(p['rwkv_r_k']), row(p['rwkv_ln_g']), row(p['rwkv_ln_b']))


def _rwkv(u3, shift0, s0, params, tr, t_valid, nseq):
    n, t, _ = u3.shape
    assert n % nseq == 0
    consts = _rwkv_consts()
    hb = tr // SUBLANES
    const2 = lambda b, j: (0, 0)
    full = lambda arr: pl.BlockSpec(arr.shape, const2)
    tile = lambda: pltpu.VMEM((nseq, tr, RWKV_WIDTH), F32)
    y, s_out = pl.pallas_call(
        functools.partial(_rwkv_kernel, tr=tr, t_valid=t_valid, nseq=nseq),
        out_shape=(jax.ShapeDtypeStruct((n, t, RWKV_WIDTH), F32),
                   jax.ShapeDtypeStruct((n, HEAD_DIM, RWKV_WIDTH), F32)),
        grid=(n // nseq, t // tr),
        in_specs=[
            pl.BlockSpec((nseq, tr, RWKV_COLS), lambda b, j: (b, j, 0)),
            pl.BlockSpec((nseq, SUBLANES, RWKV_COLS), lambda b, j: (b, jnp.maximum(j * hb - 1, 0), 0)),
            pl.BlockSpec((nseq, 1, RWKV_COLS), lambda b, j: (b, 0, 0)),
            pl.BlockSpec((nseq, HEAD_DIM, RWKV_WIDTH), lambda b, j: (b, 0, 0)),
        ] + [full(x) for x in params] + [full(x) for x in consts],
        out_specs=(pl.BlockSpec((nseq, tr, RWKV_WIDTH), lambda b, j: (b, j, 0)),
                   pl.BlockSpec((nseq, HEAD_DIM, RWKV_WIDTH), lambda b, j: (b, 0, 0))),
        scratch_shapes=[pltpu.VMEM((nseq, SUBLANES + tr, RWKV_COLS), F32)] + [tile() for _ in range(9)]
                       + [pltpu.VMEM((nseq, HEAD_DIM, RWKV_WIDTH), F32)],
        compiler_params=_cparams(("arbitrary", "arbitrary")),
        name="rwkv7",
    )(u3, u3, shift0, s0, *params, *consts)
    return y, s_out


def _state_to_lanes(s):
    n = s.shape[0]
    return jnp.transpose(s, (0, 2, 1, 3)).reshape(n, HEAD_DIM, RWKV_WIDTH)


def _lanes_to_state(s):
    n = s.shape[0]
    return jnp.transpose(s.reshape(n, HEAD_DIM, N_RWKV_HEADS, HEAD_DIM), (0, 2, 1, 3))


def _att_natural_index(dil):
    runs = ATT_RES // dil
    p = np.arange(BLOCK)
    return runs * (p % (BLOCK // runs)) + p // (BLOCK // runs)


def _attn_prompt_kernel(q_ref, kc_ref, kp_ref, vc_ref, vp_ref, bias_ref, o_ref, kcat, vcat, acc, m0, m1, l0, l1):
    u = pl.program_id(2)
    unit = ATT_UNIT
    scale = HEAD_DIM ** -0.5
    kcat[0:unit, :] = kp_ref[...]
    kcat[unit:2 * unit, :] = kc_ref[...]
    vcat[0:unit, :] = vp_ref[...]
    vcat[unit:2 * unit, :] = vc_ref[...]
    lo = lax.broadcasted_iota(jnp.int32, (BLOCK, LANES), 1) < HEAD_DIM
    n_br = len(DILATIONS)
    rows2 = lambda a, b: jnp.concatenate([a, b], axis=0)
    for bi, (window, dil) in enumerate(DILATIONS):
        first = bi == 0
        last = bi == n_br - 1
        shift = int(round(math.log2(dil)))
        n_runs = ATT_RES // dil
        run_len = BLOCK // n_runs
        n_blk = unit // (BLOCK * dil)

        def runs_of(r, g):
            base = (g // n_blk) * unit + (g % n_blk) * run_len
            return [base + (dil * m + r) * BLOCK for m in range(n_runs)]

        def gather(ref, starts):
            return jnp.concatenate([ref[pl.ds(pl.multiple_of(s0, SUBLANES), run_len), :] for s0 in starts], axis=0)

        def sub_block(j, bi, dil, shift, first, last):
            r = j & (dil - 1)
            blk = j >> shift
            q_rows = runs_of(r, blk)
            k_rows = runs_of(r, n_blk + blk - 1) + runs_of(r, n_blk + blk)
            q = gather(q_ref, q_rows) * scale
            kk = gather(kcat, k_rows).astype(BF16)
            vv = gather(vcat, k_rows).astype(BF16)
            q2 = rows2(jnp.where(lo, q, 0.0), jnp.where(lo, 0.0, q)).astype(BF16)
            s = lax.dot_general(q2, kk, NT, preferred_element_type=F32)
            if not first:
                m_old = rows2(gather(m0, q_rows), gather(m1, q_rows))
                l_old = rows2(gather(l0, q_rows), gather(l1, q_rows))
                acc_old = gather(acc, q_rows)
            yield None
            no_prev = ((u == 0) & (blk == 0)).astype(jnp.int32)
            s = s + bias_ref[bi + n_br * no_prev]
            m_cur = jnp.max(s, axis=-1, keepdims=True)
            if first:
                m_new = jnp.broadcast_to(m_cur, (2 * BLOCK, LANES))
            else:
                m_new = jnp.maximum(m_old, m_cur)
                alpha = jnp.exp(m_old - m_new)
            p = jnp.exp(s - jnp.concatenate([m_new, m_new], axis=1))
            l_new = jnp.sum(p, axis=-1, keepdims=True)
            if first:
                l_new = jnp.broadcast_to(l_new, (2 * BLOCK, LANES))
            else:
                l_new = alpha * l_old + l_new
            o2 = lax.dot_general(p.astype(BF16), vv, NN, preferred_element_type=F32)
            yield None
            o_pair = jnp.where(lo, o2[:BLOCK], o2[BLOCK:])
            if not first:
                o_pair = o_pair + jnp.where(lo, alpha[:BLOCK], alpha[BLOCK:]) * acc_old
            if last:
                linv = 1.0 / l_new
                yield [(o_ref, q_rows, o_pair * jnp.where(lo, linv[:BLOCK], linv[BLOCK:]))]
            else:
                yield [(m0, q_rows, m_new[:BLOCK]), (m1, q_rows, m_new[BLOCK:]),
                       (l0, q_rows, l_new[:BLOCK]), (l1, q_rows, l_new[BLOCK:]), (acc, q_rows, o_pair)]

        group = ATT_GROUP[bi]

        def body(jg, carry, bi=bi, dil=dil, shift=shift, first=first, last=last, group=group, run_len=run_len):
            groups = _lockstep([sub_block(jg * group + g, bi, dil, shift, first, last) for g in range(group)])
            for stores in groups:
                for ref, starts, val in stores:
                    for m, s0 in enumerate(starts):
                        ref[pl.ds(pl.multiple_of(s0, SUBLANES), run_len), :] = val[m * run_len:(m + 1) * run_len]
            return carry

        lax.fori_loop(0, unit // BLOCK // group, body, 0)


def _attn_prompt_bias(rel_bias):
    tabs = []
    for _, dil in DILATIONS:
        a = _att_natural_index(dil)
        c = np.concatenate([a, BLOCK + a])
        delta = a[:, None] + BLOCK - c[None, :]
        band = (delta >= 0) & (delta <= BLOCK)
        b = _bias_rows(rel_bias, np.clip(delta, 0, BLOCK) * dil)
        b = jnp.where(jnp.asarray(band)[:, :, None], b, NEG_INF)
        tabs.append(jnp.transpose(b, (2, 0, 1)))
    w = jnp.stack(tabs)
    first = jnp.where(jnp.asarray(np.arange(2 * BLOCK) < BLOCK), NEG_INF, w)
    w = jnp.concatenate([w, first], axis=0)
    w = w.reshape(2 * len(DILATIONS), N_PAIRS, 2 * BLOCK, 2 * BLOCK)
    return jnp.transpose(w, (1, 0, 2, 3))


def _attn_prompt(q, k, v, btab):
    unit = ATT_UNIT
    n_pairs, n, units = q.shape[:3]
    r5 = lambda x: x.reshape(n_pairs, n, units, unit, LANES)
    cur = pl.BlockSpec((None, None, None, unit, LANES), lambda a, p, u: (p, a, u, 0, 0))
    prv = pl.BlockSpec((None, None, None, unit, LANES), lambda a, p, u: (p, a, jnp.maximum(u - 1, 0), 0, 0))
    stat = lambda: pltpu.VMEM((unit, LANES), F32)
    out = pl.pallas_call(
        _attn_prompt_kernel,
        out_shape=jax.ShapeDtypeStruct((n_pairs, n, units, unit, LANES), F32),
        grid=(n, n_pairs, units),
        in_specs=[cur, cur, prv, cur, prv,
                  pl.BlockSpec((None, 2 * len(DILATIONS), 2 * BLOCK, 2 * BLOCK), lambda a, p, u: (p, 0, 0, 0))],
        out_specs=cur,
        scratch_shapes=[pltpu.VMEM((2 * unit, LANES), F32), pltpu.VMEM((2 * unit, LANES), F32),
                        stat(), stat(), stat(), stat(), stat()],
        compiler_params=_cparams(("arbitrary", "arbitrary", "arbitrary")),
        name="attn_prompt",
    )(r5(q), r5(k), r5(k), r5(v), r5(v), btab)
    return out.reshape(q.shape)


def _attn_sample_kernel(q_ref, kn_ref, vn_ref, kc_ref, vc_ref, bt_ref, bn_ref, cnt_ref, cn_ref, hm_ref, o_ref,
                        kn_s, vn_s, *, s_new):
    scale = HEAD_DIM ** -0.5
    hm = hm_ref[...]
    qm = jnp.concatenate([(q_ref[s:s + 1, :] * scale) * hm for s in range(s_new)], axis=0).astype(BF16)
    kn_s[...] = jnp.zeros_like(kn_s)
    vn_s[...] = jnp.zeros_like(vn_s)
    kn_s[0:s_new, :] = kn_ref[...]
    vn_s[0:s_new, :] = vn_ref[...]
    sc = lax.dot_general(qm, kc_ref[...].astype(BF16), NN, preferred_element_type=F32) + bt_ref[...]
    sn = lax.dot_general(qm, kn_s[...].astype(BF16), NT, preferred_element_type=F32) + bn_ref[...]
    cnt = cnt_ref[...]
    cn = cn_ref[...]
    sc = jnp.where(cnt > 0.0, sc, NEG_INF)
    sn = jnp.where(cn > 0.0, sn, NEG_INF)
    m = jnp.maximum(jnp.max(sc, axis=-1, keepdims=True), jnp.max(sn, axis=-1, keepdims=True))
    p = cnt * jnp.exp(sc - m)
    pn = cn * jnp.exp(sn - m)
    l = jnp.sum(p, axis=-1, keepdims=True) + jnp.sum(pn, axis=-1, keepdims=True)
    o = (lax.dot_general(p.astype(BF16), vc_ref[...].astype(BF16), NT, preferred_element_type=F32)
         + lax.dot_general(pn.astype(BF16), vn_s[...].astype(BF16), NN, preferred_element_type=F32))
    o = o / l
    for s in range(s_new):
        o_ref[s:s + 1, :] = jnp.sum(o[s * N_ATT_HEADS:(s + 1) * N_ATT_HEADS] * hm, axis=0, keepdims=True)


def _attn_sample_tables(rel_bias, s_new, win):
    s = np.arange(s_new)[:, None]
    dist_c = win + s - np.arange(win)[None, :]
    dist_n = s - np.arange(LANES)[None, :]
    cnt_c = np.zeros(dist_c.shape, np.float32)
    cnt_n = np.zeros(dist_n.shape, np.float32)
    for window, dil in DILATIONS:
        cnt_c += ((dist_c % dil == 0) & (dist_c <= window)).astype(np.float32)
        cnt_n += ((dist_n >= 0) & (dist_n % dil == 0) & (dist_n <= window)).astype(np.float32)
    rep = lambda x: jnp.asarray(np.repeat(x, N_ATT_HEADS, axis=0))
    rows = lambda b: jnp.transpose(b, (0, 2, 1)).reshape(s_new * N_ATT_HEADS, -1)
    return (rows(_bias_rows(rel_bias, dist_c)), rows(_bias_rows(rel_bias, np.maximum(dist_n, 0))),
            rep(cnt_c), rep(cnt_n))


def _attn_sample(q, k, v, k_cache, v_cache, li, tables, n, s_new):
    win = k_cache.shape[-1]
    assert win == MAX_WINDOW
    lanes = np.arange(ATT_WIDTH)
    hm = jnp.asarray((np.arange(N_ATT_HEADS)[:, None] == lanes[None, :] // HEAD_DIM).astype(np.float32))
    new = pl.BlockSpec((None, s_new, ATT_WIDTH), lambda b: (b, 0, 0))
    buf = pl.BlockSpec((None, None, ATT_WIDTH, win), lambda b: (li, b, 0, 0))
    full = lambda arr: pl.BlockSpec(arr.shape, lambda b: (0, 0))
    r3 = lambda x: x.reshape(n, s_new, ATT_WIDTH)
    out = pl.pallas_call(
        functools.partial(_attn_sample_kernel, s_new=s_new),
        out_shape=jax.ShapeDtypeStruct((n, s_new, ATT_WIDTH), F32),
        grid=(n,),
        in_specs=[new, new, new, buf, buf] + [full(x) for x in tables] + [full(hm)],
        out_specs=new,
        scratch_shapes=[pltpu.VMEM((LANES, ATT_WIDTH), F32), pltpu.VMEM((LANES, ATT_WIDTH), F32)],
        compiler_params=_cparams(("arbitrary",)),
        name="attn_sample",
    )(r3(q), r3(k), r3(v), k_cache, v_cache, *tables, hm)
    return out.reshape(n * s_new, ATT_WIDTH)


def _mix_ffn_kernel(ya_ref, yb_ref, yc_ref, x_ref, gate1_ref, gmix_ref, wout_ref,
                    sc_ref, sh_ref, gate2_ref, gpre_ref, gpost_ref, wg_ref, wu_ref, wo_ref, *refs, residue_major):
    a = CONV_WIDTH
    b = CONV_WIDTH + RWKV_WIDTH
    if residue_major:
        permt_ref, o_ref = refs
        yc = jnp.concatenate([jnp.concatenate([yc_ref[p, r] for r in range(ATT_RES)], axis=0)
                              for p in range(N_PAIRS)], axis=1).astype(BF16)
        yc = lax.dot_general(permt_ref[...], yc, NN, preferred_element_type=F32)
    else:
        o_ref, = refs
        yc = jnp.concatenate([yc_ref[p] for p in range(N_PAIRS)], axis=1)
    mix = (_dot1(ya_ref[...], wout_ref[0:a, :]) + _dot1(yb_ref[...], wout_ref[a:b, :])
           + _dot1(yc, wout_ref[b:, :]))
    ms = jnp.mean(mix * mix, axis=-1, keepdims=True)
    x = x_ref[...] + gate1_ref[...] * (mix * lax.rsqrt(ms + RMS_EPS) * gmix_ref[...])

    ms = jnp.mean(x * x, axis=-1, keepdims=True)
    h = x * lax.rsqrt(ms + RMS_EPS) * gpre_ref[...]
    hb = (h * (1.0 + sc_ref[...]) + sh_ref[...]).astype(BF16)
    y = None
    for c in range(FFN_SPLIT):
        cols = slice(c * (D_FF // FFN_SPLIT), (c + 1) * (D_FF // FFN_SPLIT))
        gg = lax.dot_general(hb, wg_ref[:, cols], NN, preferred_element_type=F32)
        uu = lax.dot_general(hb, wu_ref[:, cols], NN, preferred_element_type=F32)
        act = (_silu(gg) * uu).astype(BF16)
        part = lax.dot_general(act, wo_ref[cols, :], NN, preferred_element_type=F32)
        y = part if y is None else y + part
    ms = jnp.mean(y * y, axis=-1, keepdims=True)
    o_ref[...] = x + gate2_ref[...] * (y * lax.rsqrt(ms + RMS_EPS) * gpost_ref[...])


def _mix_ffn(ya, yb, yc, x2, gate1, g_mix, w_out, scale2, shift2, gate2, g_pre, g_post, wg, wu, wo, tm,
             tiles_per_group):
    m = x2.shape[0]
    rows = lambda w: pl.BlockSpec((tm, w), lambda i: (i, 0))
    vec = pl.BlockSpec((1, D_MODEL), lambda i: (0, 0))
    resident = lambda arr: pl.BlockSpec(arr.shape, lambda i: (0, 0), pipeline_mode=pl.Buffered(1))
    residue_major = yc.ndim == 6
    extra, extra_specs = [], []
    if residue_major:
        t = yc.shape[2] * ATT_UNIT
        per_unit = ATT_UNIT // tm
        per_seq = t // tm
        yc_spec = pl.BlockSpec((N_PAIRS, None, None, ATT_RES, tm // ATT_RES, LANES),
                               lambda i: (0, i // per_seq, (i % per_seq) // per_unit, 0, i % per_unit, 0))
        _, permt = _tile_perm(tm)
        extra, extra_specs = [permt], [resident(permt)]
    else:
        yc_spec = pl.BlockSpec((N_PAIRS, tm, LANES), lambda i: (0, i, 0))
    mod = lambda arr: _mod_spec(arr, tiles_per_group)
    return pl.pallas_call(
        functools.partial(_mix_ffn_kernel, residue_major=residue_major),
        out_shape=jax.ShapeDtypeStruct((m, D_MODEL), F32),
        grid=(m // tm,),
        in_specs=[rows(CONV_WIDTH), rows(RWKV_WIDTH), yc_spec, rows(D_MODEL), mod(gate1), vec, resident(w_out),
                  mod(scale2), mod(shift2), mod(gate2), vec, vec, resident(wg), resident(wu), resident(wo)]
                 + extra_specs,
        out_specs=rows(D_MODEL),
        compiler_params=_cparams(("arbitrary",)),
        name="mix_ffn",
    )(ya, yb, yc, x2, gate1, g_mix.reshape(1, D_MODEL), w_out, scale2, shift2, gate2,
      g_pre.reshape(1, D_MODEL), g_post.reshape(1, D_MODEL), wg, wu, wo, *extra)


def _window_to_heads(xt):
    n, _, win = xt.shape
    return jnp.transpose(xt.reshape(n, N_ATT_HEADS, HEAD_DIM, win), (0, 3, 1, 2))


def _layer(x2, n, t, mods, lp, att, state, li):
    shift1, scale1, gate1, shift2, scale2, gate2 = mods
    prompt = state is None
    tm = ROW_TILE if prompt else n * t
    tpg = t // tm if prompt else 1
    conv_params = _conv_params(lp['conv_w'], lp['conv_b'], lp['conv_ln_g'], lp['conv_ln_b'])
    rwkv_params = _rwkv_params(lp)
    if prompt:
        ya, zt, ur, q, k, v, *windows = _in_proj(x2, scale1, shift1, lp['g_pre_mix'], lp['w_in_bf'], tm, tpg,
                                                 seq_shape=(n, t), conv_params=conv_params)
        conv_cache = zt[:, CONV_HALO - (CONV_K - 1):]
    else:
        uc, ur, q, k, v = _in_proj(x2, scale1, shift1, lp['g_pre_mix'], lp['w_in_bf'], tm, tpg)
    ur3 = ur.reshape(n, t, RWKV_COLS)
    new_shift = ur3[:, -1]
    if prompt:
        shift0 = jnp.zeros((n, 1, RWKV_COLS), F32)
        s0 = jnp.zeros((n, HEAD_DIM, RWKV_WIDTH), F32)
        yb, s_out = _rwkv(ur3, shift0, s0, rwkv_params, tr=ROW_TILE, t_valid=ROW_TILE, nseq=RWKV_SEQS_PROMPT)
        yc = _attn_prompt(q, k, v, att['prompt'])
        assert min(MAX_WINDOW, t) == ATT_UNIT
        new_k, new_v = (_window_to_heads(w) for w in windows)
    else:
        cache_conv, state_shift, state_wkv, k_cache, v_cache = state
        ya, conv_cache = _conv_sample(uc, cache_conv[li], n, t, conv_params)
        tr = RWKV_CHUNK
        ur_pad = jnp.pad(ur3, ((0, 0), (0, tr - t), (0, 0)))
        yb, s_out = _rwkv(ur_pad, state_shift[li][:, None, :], _state_to_lanes(state_wkv[li]), rwkv_params,
                          tr=tr, t_valid=t, nseq=RWKV_SEQS_SAMPLE)
        yb = yb[:, :t]
        q_r, k_r, v_r = _pairs_to_rows(q), _pairs_to_rows(k), _pairs_to_rows(v)
        yc = _rows_to_pairs(_attn_sample(q_r, k_r, v_r, k_cache, v_cache, li, att['sample'], n, t))
        new_k = k_r.reshape(n, t, N_ATT_HEADS, HEAD_DIM)
        new_v = v_r.reshape(n, t, N_ATT_HEADS, HEAD_DIM)
    yb = yb.reshape(n * t, RWKV_WIDTH)
    x2 = _mix_ffn(ya, yb, yc, x2, gate1, lp['g_post_mix'], lp['w_out_bf'], scale2, shift2, gate2,
                  lp['g_pre_ffn'], lp['g_post_ffn'], lp['w_ffn_g_bf'], lp['w_ffn_u_bf'], lp['w_ffn_o_bf'], tm, tpg)
    return x2, (conv_cache, new_shift, _lanes_to_state(s_out), new_k, new_v)


def kernel(x_prompt, x_sample, c_prompt, c_sample, cache_conv, state_shift, state_wkv, cache_k_win, cache_v_win,
           w_ada, b_ada, g_pre_mix, g_post_mix, g_pre_ffn, g_post_ffn, w_in, w_out,
           conv_w, conv_b, conv_ln_g, conv_ln_b,
           rwkv_mu, rwkv_w0, rwkv_w2, rwkv_a0, rwkv_a2, rwkv_g2, rwkv_k_k, rwkv_k_a, rwkv_r_k,
           rwkv_ln_g, rwkv_ln_b, rel_bias, w_ffn_in, w_ffn_out):
    depth = w_in.shape[0]
    nb, tp, _ = x_prompt.shape
    ns, ts, _ = x_sample.shape
    rows = nb + ns
    rows_pad = -(-rows // SUBLANES) * SUBLANES
    c_all = jnp.pad(jnp.concatenate([c_prompt, c_sample], axis=0), ((0, rows_pad - rows), (0, 0)))
    mod = _ada_modulation(c_all, w_ada, b_ada)
    mod = mod.reshape(depth, rows_pad, 6, D_MODEL)

    win = cache_k_win.shape[2]
    att = {'prompt': _attn_prompt_bias(rel_bias), 'sample': _attn_sample_tables(rel_bias, ts, win)}
    to_pos_minor = lambda c: jnp.transpose(c, (0, 1, 3, 4, 2)).reshape(depth, ns, ATT_WIDTH, win)
    state = (cache_conv, state_shift, state_wkv, to_pos_minor(cache_k_win), to_pos_minor(cache_v_win))

    yp = x_prompt.reshape(nb * tp, D_MODEL)
    ys = x_sample.reshape(ns * ts, D_MODEL)
    outs_p, outs_s = [], []
    for li in range(depth):
        lp = dict(g_pre_mix=g_pre_mix[li], g_post_mix=g_post_mix[li], g_pre_ffn=g_pre_ffn[li],
                  g_post_ffn=g_post_ffn[li],
                  w_in_bf=w_in[li].astype(BF16), w_out_bf=w_out[li].astype(BF16),
                  w_ffn_g_bf=w_ffn_in[li, :, :D_FF].astype(BF16), w_ffn_u_bf=w_ffn_in[li, :, D_FF:].astype(BF16),
                  w_ffn_o_bf=w_ffn_out[li].astype(BF16),
                  conv_w=conv_w[li], conv_b=conv_b[li], conv_ln_g=conv_ln_g[li], conv_ln_b=conv_ln_b[li],
                  rwkv_mu=rwkv_mu[li], rwkv_w0=rwkv_w0[li], rwkv_w2=rwkv_w2[li], rwkv_a0=rwkv_a0[li],
                  rwkv_a2=rwkv_a2[li], rwkv_g2=rwkv_g2[li], rwkv_k_k=rwkv_k_k[li], rwkv_k_a=rwkv_k_a[li],
                  rwkv_r_k=rwkv_r_k[li], rwkv_ln_g=rwkv_ln_g[li], rwkv_ln_b=rwkv_ln_b[li])
        mods_p = tuple(mod[li, :nb, j][:, None, :] for j in range(6))
        mods_s = tuple(jnp.repeat(mod[li, nb:rows, j], ts, axis=0)[None] for j in range(6))
        yp, st_p = _layer(yp, nb, tp, mods_p, lp, att, None, li)
        ys, st_s = _layer(ys, ns, ts, mods_s, lp, att, state, li)
        outs_p.append(st_p)
        outs_s.append(st_s)
    stack = lambda outs, i: jnp.stack([o[i] for o in outs])
    return (yp.reshape(nb, tp, D_MODEL), ys.reshape(ns, ts, D_MODEL),
            stack(outs_p, 0), stack(outs_s, 0),
            stack(outs_p, 1), stack(outs_s, 1),
            stack(outs_p, 2), stack(outs_s, 2),
            stack(outs_p, 3), stack(outs_s, 3),
            stack(outs_p, 4), stack(outs_s, 4))
```
